```python
import jax
import jax.numpy as jnp
from jax import lax
import numpy as np

D_MODEL = 1024
BATCH = 8
SEQ = 2048
DEPTH = 2

GRID_W = 64
CTX_LEN = 256
EPS = 1e-6
ROPE_BASE = 10000.0

GLA_HEADS = 4
GLA_DK = 128
GLA_DV = 256
GLA_GATE_RANK = 16
GLA_GATE_NORM = 16.0
GLA_CHUNK = 64
SWA_HEADS = 16
SWA_KV_HEADS = 2
SWA_GROUP = SWA_HEADS // SWA_KV_HEADS
SWA_HEAD_DIM = 64
WINDOW = 128
SWA_BLOCK = 128
MLA_HEADS = 8
MLA_Q_RANK = 384
MLA_KV_RANK = 256
MLA_NOPE = 128
MLA_ROPE = 64
MLA_V = 128
MLA_BLOCK = 128
D_FF = -(-(8 * D_MODEL) // (3 * 256)) * 256

IN_SPLITS = (
    GLA_HEADS * GLA_DK, GLA_HEADS * GLA_DK, GLA_HEADS * GLA_DV, GLA_HEADS * GLA_DV,
    GLA_GATE_RANK, GLA_GATE_RANK,
    SWA_HEADS * SWA_HEAD_DIM, SWA_KV_HEADS * SWA_HEAD_DIM, SWA_KV_HEADS * SWA_HEAD_DIM,
    MLA_Q_RANK, MLA_KV_RANK, MLA_ROPE,
    3 * D_MODEL,
)
D_IN = sum(IN_SPLITS)

kernel_name = 'hybrid_gla_swa_mla_dit_trunk'


def rmsnorm(x, g):
    x32 = x.astype(jnp.float32)
    y = x32 * lax.rsqrt(jnp.mean(x32 * x32, axis=-1, keepdims=True) + EPS)
    return (y * g.astype(jnp.float32)).astype(x.dtype)


def modulate(x, shift, scale):
    return x * (1 + scale) + shift


def split_cols(z):
    out, idx = [], 0
    for n in IN_SPLITS:
        out.append(z[..., idx:idx + n])
        idx += n
    return out


def to_heads(z, n):
    b_, t_, _ = z.shape
    return z.reshape(b_, t_, n, -1).transpose(0, 2, 1, 3)


def from_heads(z):
    b_, n, t_, d = z.shape
    return z.transpose(0, 2, 1, 3).reshape(b_, t_, n * d)


def grid_positions(n_tokens):
    rows = n_tokens // GRID_W
    row = jnp.broadcast_to(jnp.arange(rows, dtype=jnp.int32)[:, None], (rows, GRID_W)).reshape(-1)
    col = jnp.broadcast_to(jnp.arange(GRID_W, dtype=jnp.int32)[None, :], (rows, GRID_W)).reshape(-1)
    return row, col


def rope_1d(x, pos):
    half = x.shape[-1] // 2
    inv = jnp.power(ROPE_BASE, -jnp.arange(half, dtype=jnp.float32) / half)
    ang = pos.astype(jnp.float32)[:, None] * inv[None, :]
    cos, sin = jnp.cos(ang), jnp.sin(ang)
    x1 = x[..., :half].astype(jnp.float32)
    x2 = x[..., half:].astype(jnp.float32)
    return jnp.concatenate([x1 * cos - x2 * sin, x2 * cos + x1 * sin], axis=-1).astype(x.dtype)


def rope_2d(x, row, col):
    h = x.shape[-1] // 2
    return jnp.concatenate([rope_1d(x[..., :h], row), rope_1d(x[..., h:], col)], axis=-1)


def gla_chunked(q, k, v, log_a, s0, strict):
    b_, h_, t_, _ = q.shape
    dv = v.shape[-1]
    n = t_ // GLA_CHUNK
    ch = lambda z: z.astype(jnp.float32).reshape(b_, h_, n, GLA_CHUNK, z.shape[-1])
    q, k, v, log_a = ch(q), ch(k), ch(v), ch(log_a)
    cum = jnp.cumsum(log_a, axis=3)
    last = cum[:, :, :, -1:, :]
    q_dec = q * jnp.exp(cum)
    k_inv = k * jnp.exp(-cum)
    k_end = k * jnp.exp(last - cum)
    mask = jnp.tril(jnp.ones((GLA_CHUNK, GLA_CHUNK), dtype=bool), k=-1 if strict else 0)
    scores = jnp.where(mask, jnp.einsum('bhncd,bhnsd->bhncs', q_dec, k_inv), 0.0)
    o_intra = jnp.einsum('bhncs,bhnsv->bhncv', scores, v)
    kv_add = jnp.einsum('bhnsd,bhnsv->bhndv', k_end, v)
    decay = jnp.exp(last[:, :, :, 0, :])

    def step(state, xs):
        q_c, kv_c, dec_c = xs
        o_c = jnp.einsum('bhcd,bhdv->bhcv', q_c, state)
        return dec_c[..., None] * state + kv_c, o_c

    xs = (jnp.moveaxis(q_dec, 2, 0), jnp.moveaxis(kv_add, 2, 0), jnp.moveaxis(decay, 2, 0))
    s_fin, o_inter = lax.scan(step, s0.astype(jnp.float32), xs)
    o = o_intra + jnp.moveaxis(o_inter, 0, 2)
    return o.reshape(b_, h_, t_, dv), s_fin


def gla_bidir(q, k, v, la_f, la_b, s0_f, s0_b):
    flip = lambda z: jnp.flip(z, axis=2)
    o_f, s_f = gla_chunked(q, k, v, la_f, s0_f, strict=False)
    o_b, s_b = gla_chunked(flip(q), flip(k), flip(v), flip(la_b), s0_b, strict=True)
    return o_f + flip(o_b), s_f, s_b


def sink_attend(q, k, v, mask, sink):
    scale = q.shape[-1] ** -0.5
    s = jnp.einsum('bgrqd,bgkd->bgrqk', q, k, preferred_element_type=jnp.float32) * scale
    s = jnp.where(mask, s, -jnp.inf)
    sk = sink.astype(jnp.float32)[None, :, :, None, None]
    m = jnp.maximum(jnp.max(s, axis=-1, keepdims=True), sk)
    p = jnp.exp(s - m)
    den = jnp.sum(p, axis=-1, keepdims=True) + jnp.exp(sk - m)
    o = jnp.einsum('bgrqk,bgkd->bgrqd', p, v.astype(jnp.float32)) / den
    return o.astype(q.dtype)


def swa_latent(q, k, v, kc, vc, sink):
    b_, g_, r_, t_, d = q.shape
    nb = t_ // SWA_BLOCK
    span = 3 * SWA_BLOCK
    pad = ((0, 0), (0, 0), (SWA_BLOCK, SWA_BLOCK), (0, 0))
    kp, vp = jnp.pad(k, pad), jnp.pad(v, pad)
    off = jnp.arange(span) - SWA_BLOCK
    rel = off[None, :] - jnp.arange(SWA_BLOCK)[:, None]
    ctx_ok = jnp.ones((SWA_BLOCK, kc.shape[2]), dtype=bool)
    q_blocks = jnp.moveaxis(q.reshape(b_, g_, r_, nb, SWA_BLOCK, d), 3, 0)

    def one_block(args):
        i, q_i = args
        start = i * SWA_BLOCK
        k_i = jnp.concatenate([lax.dynamic_slice_in_dim(kp, start, span, axis=2), kc], axis=2)
        v_i = jnp.concatenate([lax.dynamic_slice_in_dim(vp, start, span, axis=2), vc], axis=2)
        key_pos = start + off
        band = (jnp.abs(rel) <= WINDOW) & ((key_pos >= 0) & (key_pos < t_))[None, :]
        return sink_attend(q_i, k_i, v_i, jnp.concatenate([band, ctx_ok], axis=1), sink)

    o = lax.map(one_block, (jnp.arange(nb), q_blocks))
    return jnp.moveaxis(o, 0, 3).reshape(b_, g_, r_, t_, d)


def mla_attend(qn, qr, kn, kr, v):
    b_, h_, t_, _ = qn.shape
    nb = t_ // MLA_BLOCK
    scale = (MLA_NOPE + MLA_ROPE) ** -0.5
    blocks = lambda z: jnp.moveaxis(z.reshape(b_, h_, nb, MLA_BLOCK, z.shape[-1]), 2, 0)

    def one_block(args):
        qn_i, qr_i = args
        s = jnp.einsum('bhqd,bhkd->bhqk', qn_i, kn, preferred_element_type=jnp.float32)
        s = s + jnp.einsum('bhqd,bkd->bhqk', qr_i, kr, preferred_element_type=jnp.float32)
        p = jax.nn.softmax(s * scale, axis=-1)
        return jnp.einsum('bhqk,bhkd->bhqd', p.astype(v.dtype), v)

    o = lax.map(one_block, (blocks(qn), blocks(qr)))
    return jnp.moveaxis(o, 0, 2).reshape(b_, h_, t_, v.shape[-1])


def token_mixer(h, hc, row, col, with_ctx_out, w_in, w_gk_fwd, b_gk_fwd, w_gk_bwd, b_gk_bwd,
                gla_norm, sinks, q_norm, w_q_up, kv_norm, w_kv_up, w_pa, w_pb, w_pc, w_o):
    (qa, ka, va, ga, gkf, gkb, qs, ks, vs, cq, ckv, kr, mg) = split_cols(h @ w_in)
    (qa_c, ka_c, va_c, ga_c, gkf_c, gkb_c, qs_c, ks_c, vs_c, cq_c, ckv_c, kr_c, mg_c) = split_cols(hc @ w_in)

    def gla_prep(q, k, v, gf, gb):
        la_f = jax.nn.log_sigmoid(gf @ w_gk_fwd + b_gk_fwd) / GLA_GATE_NORM
        la_b = jax.nn.log_sigmoid(gb @ w_gk_bwd + b_gk_bwd) / GLA_GATE_NORM
        return (to_heads(q, GLA_HEADS) * GLA_DK ** -0.5, to_heads(k, GLA_HEADS), to_heads(v, GLA_HEADS),
                to_heads(la_f, GLA_HEADS), to_heads(la_b, GLA_HEADS))

    def gla_post(o, g):
        return from_heads(rmsnorm(o, gla_norm)).astype(g.dtype) * jax.nn.silu(g)

    zero = jnp.zeros((hc.shape[0], GLA_HEADS, GLA_DK, GLA_DV), jnp.float32)
    o_a_c, s_f, s_b = gla_bidir(*gla_prep(qa_c, ka_c, va_c, gkf_c, gkb_c), zero, zero)
    o_a, _, _ = gla_bidir(*gla_prep(qa, ka, va, gkf, gkb), s_f, s_b)

    def swa_prep(q, k, v, rotate):
        q, k, v = to_heads(q, SWA_HEADS), to_heads(k, SWA_KV_HEADS), to_heads(v, SWA_KV_HEADS)
        if rotate:
            q, k = rope_2d(q, row, col), rope_2d(k, row, col)
        b_, _, t_, d = q.shape
        return q.reshape(b_, SWA_KV_HEADS, SWA_GROUP, t_, d), k, v

    def swa_post(o):
        b_, g_, r_, t_, d = o.shape
        return from_heads(o.reshape(b_, g_ * r_, t_, d))

    sink = sinks.reshape(SWA_KV_HEADS, SWA_GROUP)
    q_b, k_b, v_b = swa_prep(qs, ks, vs, True)
    q_bc, k_bc, v_bc = swa_prep(qs_c, ks_c, vs_c, False)
    o_b = swa_latent(q_b, k_b, v_b, k_bc, v_bc, sink)

    def mla_prep(cq_, ckv_, kr_, rotate):
        qf = to_heads(rmsnorm(cq_, q_norm) @ w_q_up, MLA_HEADS)
        kvf = to_heads(rmsnorm(ckv_, kv_norm) @ w_kv_up, MLA_HEADS)
        qn_, qr_ = qf[..., :MLA_NOPE], qf[..., MLA_NOPE:]
        kn_, v_ = kvf[..., :MLA_NOPE], kvf[..., MLA_NOPE:]
        if rotate:
            qr_, kr_ = rope_2d(qr_, row, col), rope_2d(kr_, row, col)
        return qn_, qr_, kn_, kr_, v_

    qn, qr, kn, kro, vm = mla_prep(cq, ckv, kr, True)
    qn_c, qr_c, kn_c, kro_c, vm_c = mla_prep(cq_c, ckv_c, kr_c, False)
    o_c = mla_attend(qn, qr, jnp.concatenate([kn, kn_c], axis=2), jnp.concatenate([kro, kro_c], axis=1),
                     jnp.concatenate([vm, vm_c], axis=2))

    def merge(y_a, y_b, y_c, gates):
        g_a, g_b, g_c = jnp.split(jax.nn.sigmoid(gates), 3, axis=-1)
        return (g_a * (y_a @ w_pa) + g_b * (y_b @ w_pb) + g_c * (y_c @ w_pc)) @ w_o

    y = merge(gla_post(o_a, ga), swa_post(o_b), from_heads(o_c), mg)
    if not with_ctx_out:
        return y, None
    l_c = hc.shape[1]
    o_b_c = sink_attend(q_bc, k_bc, v_bc, jnp.ones((l_c, l_c), dtype=bool), sink)
    o_c_c = mla_attend(qn_c, qr_c, kn_c, kro_c, vm_c)
    y_ctx = merge(gla_post(o_a_c, ga_c), swa_post(o_b_c), from_heads(o_c_c), mg_c)
    return y, y_ctx


def swiglu(h, w_in, w_out):
    gate, up = jnp.split(h @ w_in, 2, axis=-1)
    return (jax.nn.silu(gate) * up) @ w_out


def setup_inputs(seed: int = 0) -> dict:
    key = jax.random.key(seed)
    keys = iter(jax.random.split(key, 32))
    f32 = jnp.float32

    def w(shape, fan_in, gain=1.0):
        return jax.random.normal(next(keys), shape, f32) * (gain * fan_in ** -0.5)

    def norm_gain(shape):
        return 1.0 + 0.02 * jax.random.normal(next(keys), shape, f32)

    def small(shape, s):
        return s * jax.random.normal(next(keys), shape, f32)

    L = DEPTH
    return {
        'x': jax.random.normal(next(keys), (BATCH, SEQ, D_MODEL), f32),
        'c': jax.random.normal(next(keys), (BATCH, D_MODEL), f32),
        'ctx': jax.random.normal(next(keys), (BATCH, CTX_LEN, D_MODEL), f32),
        'c_ctx': jax.random.normal(next(keys), (D_MODEL,), f32),
        'w_mod': w((L, D_MODEL, 6 * D_MODEL), D_MODEL, 0.5),
        'b_mod': small((L, 6 * D_MODEL), 0.02),
        'norm_mix': norm_gain((L, D_MODEL)),
        'w_in': w((L, D_MODEL, D_IN), D_MODEL),
        'w_gk_fwd': w((L, GLA_GATE_RANK, GLA_HEADS * GLA_DK), GLA_GATE_RANK),
        'b_gk_fwd': small((L, GLA_HEADS * GLA_DK), 0.1),
        'w_gk_bwd': w((L, GLA_GATE_RANK, GLA_HEADS * GLA_DK), GLA_GATE_RANK),
        'b_gk_bwd': small((L, GLA_HEADS * GLA_DK), 0.1),
        'gla_norm': norm_gain((L, GLA_DV)),
        'sinks': small((L, SWA_HEADS), 0.5),
        'q_norm': norm_gain((L, MLA_Q_RANK)),
        'w_q_up': w((L, MLA_Q_RANK, MLA_HEADS * (MLA_NOPE + MLA_ROPE)), MLA_Q_RANK),
        'kv_norm': norm_gain((L, MLA_KV_RANK)),
        'w_kv_up': w((L, MLA_KV_RANK, MLA_HEADS * (MLA_NOPE + MLA_V)), MLA_KV_RANK),
        'w_pa': w((L, GLA_HEADS * GLA_DV, D_MODEL), GLA_HEADS * GLA_DV),
        'w_pb': w((L, SWA_HEADS * SWA_HEAD_DIM, D_MODEL), SWA_HEADS * SWA_HEAD_DIM),
        'w_pc': w((L, MLA_HEADS * MLA_V, D_MODEL), MLA_HEADS * MLA_V),
        'w_o': w((L, D_MODEL, D_MODEL), D_MODEL),
        'norm_ffn': norm_gain((L, D_MODEL)),
        'w_ffn_in': w((L, D_MODEL, 2 * D_FF), D_MODEL),
        'w_ffn_out': w((L, D_FF, D_MODEL), D_FF),
        'final_norm': norm_gain((D_MODEL,)),
    }


def reference(x, c, ctx, c_ctx, w_mod, b_mod, norm_mix, w_in, w_gk_fwd, b_gk_fwd, w_gk_bwd, b_gk_bwd,
              gla_norm, sinks, q_norm, w_q_up, kv_norm, w_kv_up, w_pa, w_pb, w_pc, w_o,
              norm_ffn, w_ffn_in, w_ffn_out, final_norm):
    row, col = grid_positions(x.shape[1])
    xc = ctx
    for l in range(DEPTH):
        last = l == DEPTH - 1
        mod = jax.nn.silu(c) @ w_mod[l] + b_mod[l]
        mod_c = jax.nn.silu(c_ctx) @ w_mod[l] + b_mod[l]
        sh1, sc1, g1, sh2, sc2, g2 = [m[:, None, :] for m in jnp.split(mod, 6, axis=-1)]
        sh1c, sc1c, g1c, sh2c, sc2c, g2c = jnp.split(mod_c, 6, axis=-1)
        h = modulate(rmsnorm(x, norm_mix[l]), sh1, sc1)
        hc = modulate(rmsnorm(xc, norm_mix[l]), sh1c, sc1c)
        y, y_ctx = token_mixer(h, hc, row, col, not last, w_in[l], w_gk_fwd[l], b_gk_fwd[l],
                               w_gk_bwd[l], b_gk_bwd[l], gla_norm[l], sinks[l], q_norm[l], w_q_up[l],
                               kv_norm[l], w_kv_up[l], w_pa[l], w_pb[l], w_pc[l], w_o[l])
        x = x + g1 * y
        x = x + g2 * swiglu(modulate(rmsnorm(x, norm_ffn[l]), sh2, sc2), w_ffn_in[l], w_ffn_out[l])
        if not last:
            xc = xc + g1c * y_ctx
            xc = xc + g2c * swiglu(modulate(rmsnorm(xc, norm_ffn[l]), sh2c, sc2c), w_ffn_in[l], w_ffn_out[l])
    return rmsnorm(x, final_norm)
```

```python
import functools

import jax
import jax.numpy as jnp
import numpy as np
from jax import lax
from jax.experimental import pallas as pl
from jax.experimental.pallas import tpu as pltpu

F32 = jnp.float32
BF16 = jnp.bfloat16

D_MODEL = 1024
GRID_W = 64
EPS = 1e-6
ROPE_BASE = 10000.0

GLA_HEADS = 4
GLA_DK = 128
GLA_DV = 256
GLA_GATE_RANK = 16
GLA_GATE_NORM = 16.0
GLA_CHUNK = 64

SWA_HEADS = 16
SWA_KV_HEADS = 2
SWA_GROUP = SWA_HEADS // SWA_KV_HEADS
SWA_HEAD_DIM = 64
WINDOW = 128

MLA_HEADS = 8
MLA_Q_RANK = 384
MLA_KV_RANK = 256
MLA_NOPE = 128
MLA_ROPE = 64
MLA_V = 128
MLA_QK_PAD = 256

D_FF = -(-(8 * D_MODEL) // (3 * 256)) * 256

IN_SPLITS = (
    GLA_HEADS * GLA_DK, GLA_HEADS * GLA_DK, GLA_HEADS * GLA_DV, GLA_HEADS * GLA_DV,
    GLA_GATE_RANK, GLA_GATE_RANK,
    SWA_HEADS * SWA_HEAD_DIM, SWA_KV_HEADS * SWA_HEAD_DIM, SWA_KV_HEADS * SWA_HEAD_DIM,
    MLA_Q_RANK, MLA_KV_RANK, MLA_ROPE,
    3 * D_MODEL,
)

LANES = 128
TOKEN_TILE = 256
MISC_W = 768
MISC_GK_LANE = 64
VMEM_LIMIT = 56 * 1024 * 1024


def _cparams(n_axes):
    return pltpu.CompilerParams(
        dimension_semantics=("arbitrary",) * n_axes, vmem_limit_bytes=VMEM_LIMIT)


def _const_spec(shape):
    nd = len(shape)
    return pl.BlockSpec(shape, lambda *_: (0,) * nd, pipeline_mode=pl.Buffered(1))


def _rope(x, tab_ref):
    return (x * tab_ref[0] + pltpu.roll(x, 16, 1) * tab_ref[1]
            + pltpu.roll(x, LANES - 16, 1) * tab_ref[2])


def _rms(x):
    return x * lax.rsqrt(jnp.mean(x * x, axis=-1, keepdims=True) + EPS)


def _dot(a, b):
    return jnp.dot(a, b, preferred_element_type=F32)


def _dot_nt(a, b):
    return lax.dot_general(a, b, (((1,), (1,)), ((), ())), preferred_element_type=F32)


def _dot_tn(a, b):
    return lax.dot_general(a, b, (((0,), (0,)), ((), ())), preferred_element_type=F32)


def _mod_kernel(c_ref, w_ref, b_ref, o_ref):
    c = c_ref[...]
    a = (c * jax.nn.sigmoid(c)).astype(BF16)
    o_ref[0] = _dot(a, w_ref[0].astype(BF16)) + b_ref[0]


def _modulation(cc, w_mod, b_mod):
    depth, d, n = w_mod.shape
    tn = 1536
    return pl.pallas_call(
        _mod_kernel,
        grid=(depth, n // tn),
        in_specs=[
            pl.BlockSpec(cc.shape, lambda l, j: (0, 0)),
            pl.BlockSpec((1, d, tn), lambda l, j: (l, 0, j)),
            pl.BlockSpec((1, 1, tn), lambda l, j: (l, 0, j)),
        ],
        out_specs=pl.BlockSpec((1, cc.shape[0], tn), lambda l, j: (l, 0, j)),
        out_shape=jax.ShapeDtypeStruct((depth, cc.shape[0], n), F32),
        compiler_params=_cparams(2),
        name="modulation",
    )(cc, w_mod, b_mod.reshape(depth, 1, n))


def _inproj_kernel(x_ref, mod_ref, gain_ref, w_ref, rq_ref, rk_ref, rm_ref,
                   qk_ref, va_ref, ga_ref, qs_ref, kvx_ref, misc_ref, mg_ref):
    x = x_ref[0]
    y = _rms(x) * gain_ref[...]
    h = (y * (1.0 + mod_ref[0, 0, 1:2, :]) + mod_ref[0, 0, 0:1, :]).astype(BF16)

    def proj(c0, c1):
        return _dot(h, w_ref[:, c0:c1])

    qk_ref[0] = proj(0, 1024).astype(BF16)
    va_ref[0] = proj(1024, 2048).astype(BF16)
    r = proj(2048, 3072)
    ga_ref[0] = (r * jax.nn.sigmoid(r)).astype(BF16)

    r = proj(3072, 4096)
    for j in range(1024 // LANES):
        sl = slice(j * LANES, (j + 1) * LANES)
        qs_ref[0, :, sl] = _rope(r[:, sl], rq_ref).astype(BF16)

    r = proj(4096, 4352)
    kk = _rope(r[:, 0:LANES], rk_ref)
    vv = r[:, LANES:2 * LANES]
    kx = pltpu.roll(kk, 64, 1)
    vx = pltpu.roll(vv, 64, 1)
    lo = lax.broadcasted_iota(jnp.int32, kk.shape, 1) < 64
    zero = jnp.zeros_like(kk)
    blocks = (
        jnp.where(lo, kk, zero), jnp.where(lo, zero, kx),
        jnp.where(lo, vv, zero), jnp.where(lo, zero, vx),
        jnp.where(lo, kx, zero), jnp.where(lo, zero, kk),
        jnp.where(lo, vx, zero), jnp.where(lo, zero, vv),
    )
    for j, blk in enumerate(blocks):
        kvx_ref[0, :, j * LANES:(j + 1) * LANES] = blk.astype(BF16)

    r = proj(4352, 4352 + MISC_W)
    misc_ref[0, :, 0:MISC_W - LANES] = r[:, 0:MISC_W - LANES]
    misc_ref[0, :, MISC_W - LANES:MISC_W] = _rope(r[:, MISC_W - LANES:MISC_W], rm_ref)

    for j in range(3):
        r = proj(5120 + j * 1024, 6144 + j * 1024)
        mg_ref[0, :, j * 1024:(j + 1) * 1024] = jax.nn.sigmoid(r).astype(BF16)


def _inproj(xs, modtab, gain, w, rq, rk, rm, n_lat):
    b, s, d = xs.shape
    tm = TOKEN_TILE
    nt = s // tm
    row = lambda bi, i: (bi, i, 0)
    tab = pl.BlockSpec((3, tm, LANES), lambda bi, i: (0, i, 0))
    widths = (1024, 1024, 1024, 1024, 1024, MISC_W, 3072)
    dtypes = (BF16, BF16, BF16, BF16, BF16, F32, BF16)
    return pl.pallas_call(
        _inproj_kernel,
        grid=(b, nt),
        in_specs=[
            pl.BlockSpec((1, tm, d), row),
            pl.BlockSpec((1, 1, 8, d), lambda bi, i: (bi, i // n_lat, 0, 0)),
            _const_spec((1, d)),
            _const_spec(w.shape),
            tab, tab, tab,
        ],
        out_specs=[pl.BlockSpec((1, tm, n), row) for n in widths],
        out_shape=[jax.ShapeDtypeStruct((b, s, n), dt) for n, dt in zip(widths, dtypes)],
        compiler_params=_cparams(2),
        name="inproj",
    )(xs, modtab, gain, w, rq, rk, rm)


def _gla_kernel(q_ref, k_ref, v_ref, g_ref, gk_ref, wf_ref, wb_ref, bf_ref, bb_ref, gn_ref,
                o_ref, of_ref, ob_ref, sf_ref, sb_ref, *, n_lat, n_ctx):
    c = GLA_CHUNK
    t_lat = n_lat * c
    ri = lax.broadcasted_iota(jnp.int32, (c, c), 0)
    ci = lax.broadcasted_iota(jnp.int32, (c, c), 1)
    tri_f = (ci <= ri).astype(BF16)
    tri_b = (ci >= ri).astype(BF16)
    keep_f = ci <= ri
    keep_b = ci > ri
    q_scale = GLA_DK ** -0.5

    sf_ref[...] = jnp.zeros_like(sf_ref)
    sb_ref[...] = jnp.zeros_like(sb_ref)

    def one_direction(r0, w_ref, b_ref, tri, keep, total_row, s_ref, out_ref):
        rows = pl.ds(r0, c)
        gk = gk_ref[0, rows, :].astype(BF16)
        z = _dot(gk, w_ref[0]) + b_ref[0]
        la = (jnp.minimum(z, 0.0) - jnp.log1p(jnp.exp(-jnp.abs(z)))) * (1.0 / GLA_GATE_NORM)
        la_hi = la.astype(BF16)
        la_lo = (la - la_hi.astype(F32)).astype(BF16)
        cum = _dot(tri, la_hi) + _dot(tri, la_lo)
        total = cum[total_row:total_row + 1, :]
        q = q_ref[0, rows, :].astype(F32) * q_scale
        k = k_ref[0, rows, :].astype(F32)
        v = v_ref[0, rows, :]
        q_dec = (q * jnp.exp(cum)).astype(BF16)
        k_inv = (k * jnp.exp(-cum)).astype(BF16)
        k_end = (k * jnp.exp(total - cum)).astype(BF16)
        scores = jnp.where(keep, _dot_nt(q_dec, k_inv), 0.0).astype(BF16)
        state = s_ref[...]
        out_ref[rows, :] = _dot(scores, v) + _dot(q_dec, state.astype(BF16))
        decay = jnp.exp(jnp.broadcast_to(total, (8, GLA_DK))).T[:, 0:1]
        s_ref[...] = decay * state + _dot_tn(k_end, v)

    def step(r0_f, r0_b):
        one_direction(r0_f, wf_ref, bf_ref, tri_f, keep_f, c - 1, sf_ref, of_ref)
        one_direction(r0_b, wb_ref, bb_ref, tri_b, keep_b, 0, sb_ref, ob_ref)

    def ctx_body(i, carry):
        step(pl.multiple_of(t_lat + i * c, c), pl.multiple_of(t_lat + (n_ctx - 1 - i) * c, c))
        return carry

    def lat_body(i, carry):
        step(pl.multiple_of(i * c, c), pl.multiple_of((n_lat - 1 - i) * c, c))
        return carry

    lax.fori_loop(0, n_ctx, ctx_body, 0)
    lax.fori_loop(0, n_lat, lat_body, 0)

    def post_body(i, carry):
        rows = pl.ds(pl.multiple_of(i * TOKEN_TILE, TOKEN_TILE), TOKEN_TILE)
        o = of_ref[rows, :] + ob_ref[rows, :]
        y = _rms(o) * gn_ref[...]
        o_ref[0, rows, :] = (y * g_ref[0, rows, :].astype(F32)).astype(BF16)
        return carry

    lax.fori_loop(0, (n_lat + n_ctx) * c // TOKEN_TILE, post_body, 0)


def _gla(qk, va, ga, misc, wf, wb, bf, bb, gn, t_lat):
    b, s, _ = qk.shape
    n_lat = t_lat // GLA_CHUNK
    n_ctx = (s - t_lat) // GLA_CHUNK
    h = GLA_HEADS
    gk_block = (MISC_W - LANES) // LANES
    return pl.pallas_call(
        functools.partial(_gla_kernel, n_lat=n_lat, n_ctx=n_ctx),
        grid=(b, h),
        in_specs=[
            pl.BlockSpec((1, s, GLA_DK), lambda bi, hi: (bi, 0, hi)),
            pl.BlockSpec((1, s, GLA_DK), lambda bi, hi: (bi, 0, h + hi)),
            pl.BlockSpec((1, s, GLA_DV), lambda bi, hi: (bi, 0, hi)),
            pl.BlockSpec((1, s, GLA_DV), lambda bi, hi: (bi, 0, hi)),
            pl.BlockSpec((1, s, LANES), lambda bi, hi: (bi, 0, gk_block)),
            pl.BlockSpec((1, LANES, GLA_DK), lambda bi, hi: (hi, 0, 0)),
            pl.BlockSpec((1, LANES, GLA_DK), lambda bi, hi: (hi, 0, 0)),
            pl.BlockSpec((1, 1, GLA_DK), lambda bi, hi: (hi, 0, 0)),
            pl.BlockSpec((1, 1, GLA_DK), lambda bi, hi: (hi, 0, 0)),
            pl.BlockSpec((1, GLA_DV), lambda bi, hi: (0, 0)),
        ],
        out_specs=pl.BlockSpec((1, s, GLA_DV), lambda bi, hi: (bi, 0, hi)),
        out_shape=jax.ShapeDtypeStruct((b, s, h * GLA_DV), BF16),
        scratch_shapes=[
            pltpu.VMEM((s, GLA_DV), F32), pltpu.VMEM((s, GLA_DV), F32),
            pltpu.VMEM((GLA_DK, GLA_DV), F32), pltpu.VMEM((GLA_DK, GLA_DV), F32),
        ],
        compiler_params=_cparams(2),
        name="gla",
    )(qk, qk, va, ga, misc, wf, wb, bf, bb, gn)


def _swa_kernel(sink_ref, q_ref, kv_ref, o_ref, *, n_lat, t_lat):
    tq = TOKEN_TILE
    span = tq + 2 * WINDOW
    g = pl.program_id(1)
    i = pl.program_id(2)
    s_tot = kv_ref.shape[1]
    ctx = slice(t_lat, s_tot)
    ka, kb, va, vb = (slice(j * LANES, (j + 1) * LANES) for j in range(4))

    def softmax_parts(parts, sink):
        m = jnp.maximum(functools.reduce(
            jnp.maximum, [jnp.max(p, axis=-1, keepdims=True) for p in parts]), sink)
        es = [jnp.exp(p - m) for p in parts]
        den = functools.reduce(
            lambda a, e: a + jnp.sum(e, axis=-1, keepdims=True), es, jnp.exp(sink - m))
        return es, 1.0 / den

    def head(qp, kcols, vcols, sink, win):
        sc = _dot_nt(qp, kv_ref[0, ctx, kcols])
        if win is None:
            (ec,), inv = softmax_parts([sc], sink)
            return _dot(ec.astype(BF16), kv_ref[0, ctx, vcols]) * inv
        rows, band = win
        sw = jnp.where(band, _dot_nt(qp, kv_ref[0, rows, kcols]), -1e30)
        (ew, ec), inv = softmax_parts([sw, sc], sink)
        acc = _dot(ew.astype(BF16), kv_ref[0, rows, vcols])
        acc += _dot(ec.astype(BF16), kv_ref[0, ctx, vcols])
        return acc * inv

    def tile(win):
        for p in range(SWA_GROUP // 2):
            cols = slice(p * LANES, (p + 1) * LANES)
            qp = q_ref[0, :, cols]
            sink_a = sink_ref[g * SWA_GROUP + 2 * p]
            sink_b = sink_ref[g * SWA_GROUP + 2 * p + 1]
            o_ref[0, :, cols] = (head(qp, ka, va, sink_a, win)
                                 + head(qp, kb, vb, sink_b, win)).astype(BF16)

    @pl.when(i < n_lat)
    def _():
        ws = pl.multiple_of(jnp.clip(i * tq - WINDOW, 0, t_lat - span), WINDOW)
        q_pos = i * tq + lax.broadcasted_iota(jnp.int32, (tq, span), 0)
        k_pos = ws + lax.broadcasted_iota(jnp.int32, (tq, span), 1)
        tile((pl.ds(ws, span), jnp.abs(q_pos - k_pos) <= WINDOW))

    @pl.when(i >= n_lat)
    def _():
        tile(None)


def _swa(sinks, qs, kvx, t_lat, n_tiles):
    b, s, _ = qs.shape
    tq = TOKEN_TILE
    gw = SWA_GROUP * SWA_HEAD_DIM
    return pl.pallas_call(
        functools.partial(_swa_kernel, n_lat=t_lat // tq, t_lat=t_lat),
        grid=(b, SWA_KV_HEADS, n_tiles),
        in_specs=[
            pl.BlockSpec(memory_space=pltpu.SMEM),
            pl.BlockSpec((1, tq, gw), lambda bi, gi, i: (bi, i, gi)),
            pl.BlockSpec((1, s, 4 * LANES), lambda bi, gi, i: (bi, 0, gi)),
        ],
        out_specs=pl.BlockSpec((1, tq, gw), lambda bi, gi, i: (bi, i, gi)),
        out_shape=jax.ShapeDtypeStruct((b, n_tiles * tq, SWA_HEADS * SWA_HEAD_DIM), BF16),
        compiler_params=_cparams(3),
        name="swa",
    )(sinks, qs, kvx)


def _mla_prep_kernel(m_ref, wq_ref, wkv_ref, qn_ref, kvn_ref, rq_ref, q_ref, k_ref, v_ref):
    scale = (MLA_NOPE + MLA_ROPE) ** -0.5
    cq = m_ref[0, :, 0:MLA_Q_RANK]
    ckv = m_ref[0, :, MLA_Q_RANK:MLA_Q_RANK + MLA_KV_RANK]
    kr = m_ref[0, :, MISC_W - LANES:MISC_W]
    lo = lax.broadcasted_iota(jnp.int32, kr.shape, 1) < MLA_ROPE
    kr = jnp.where(lo, kr, 0.0).astype(BF16)

    qf = _dot((_rms(cq) * qn_ref[...]).astype(BF16), wq_ref[...])
    kvf = _dot((_rms(ckv) * kvn_ref[...]).astype(BF16), wkv_ref[...])
    for h in range(MLA_HEADS):
        c0 = h * MLA_QK_PAD
        q_ref[0, :, c0:c0 + LANES] = (qf[:, c0:c0 + LANES] * scale).astype(BF16)
        q_ref[0, :, c0 + LANES:c0 + 2 * LANES] = _rope(
            qf[:, c0 + LANES:c0 + 2 * LANES], rq_ref).astype(BF16)
        k_ref[0, :, c0:c0 + LANES] = kvf[:, c0:c0 + LANES].astype(BF16)
        k_ref[0, :, c0 + LANES:c0 + 2 * LANES] = kr
        v_ref[0, :, h * MLA_V:(h + 1) * MLA_V] = kvf[:, c0 + LANES:c0 + 2 * LANES].astype(BF16)


def _mla_prep(misc, wq, wkv, qn, kvn, rq):
    b, s, _ = misc.shape
    tm = TOKEN_TILE
    row = lambda bi, i: (bi, i, 0)
    widths = (MLA_HEADS * MLA_QK_PAD, MLA_HEADS * MLA_QK_PAD, MLA_HEADS * MLA_V)
    return pl.pallas_call(
        _mla_prep_kernel,
        grid=(b, s // tm),
        in_specs=[
            pl.BlockSpec((1, tm, MISC_W), row),
            _const_spec(wq.shape), _const_spec(wkv.shape),
            _const_spec(qn.shape), _const_spec(kvn.shape),
            pl.BlockSpec((3, tm, LANES), lambda bi, i: (0, i, 0)),
        ],
        out_specs=[pl.BlockSpec((1, tm, n), row) for n in widths],
        out_shape=[jax.ShapeDtypeStruct((b, s, n), BF16) for n in widths],
        compiler_params=_cparams(2),
        name="mla_prep",
    )(misc, wq, wkv, qn, kvn, rq)


def _mla_attn_kernel(q_ref, k_ref, v_ref, o_ref, *, n_lat, t_lat):
    i = pl.program_id(2)
    q = q_ref[0]

    def attend(k, v):
        s = _dot_nt(q, k)
        p = jnp.exp(s - jnp.max(s, axis=-1, keepdims=True))
        inv = 1.0 / jnp.sum(p, axis=-1, keepdims=True)
        o_ref[0] = (_dot(p.astype(BF16), v) * inv).astype(BF16)

    @pl.when(i < n_lat)
    def _():
        attend(k_ref[0], v_ref[0])

    @pl.when(i >= n_lat)
    def _():
        attend(k_ref[0, t_lat:, :], v_ref[0, t_lat:, :])


def _mla_attn(q, k, v, t_lat, n_tiles):
    b, s, _ = q.shape
    tq = TOKEN_TILE
    return pl.pallas_call(
        functools.partial(_mla_attn_kernel, n_lat=t_lat // tq, t_lat=t_lat),
        grid=(b, MLA_HEADS, n_tiles),
        in_specs=[
            pl.BlockSpec((1, tq, MLA_QK_PAD), lambda bi, hi, i: (bi, i, hi)),
            pl.BlockSpec((1, s, MLA_QK_PAD), lambda bi, hi, i: (bi, 0, hi)),
            pl.BlockSpec((1, s, MLA_V), lambda bi, hi, i: (bi, 0, hi)),
        ],
        out_specs=pl.BlockSpec((1, tq, MLA_V), lambda bi, hi, i: (bi, i, hi)),
        out_shape=jax.ShapeDtypeStruct((b, n_tiles * tq, MLA_HEADS * MLA_V), BF16),
        compiler_params=_cparams(3),
        name="mla_attn",
    )(q, k, v)


def _merge_kernel(x_ref, ya_ref, yb_ref, yc_ref, mg_ref, mod_ref,
                  wa_ref, wb_ref, wc_ref, wo_ref, o_ref):
    d = D_MODEL
    u = mg_ref[0, :, 0:d].astype(F32) * _dot(ya_ref[0], wa_ref[...])
    u += mg_ref[0, :, d:2 * d].astype(F32) * _dot(yb_ref[0], wb_ref[...])
    u += mg_ref[0, :, 2 * d:3 * d].astype(F32) * _dot(yc_ref[0], wc_ref[...])
    y = _dot(u.astype(BF16), wo_ref[...])
    o_ref[0] = x_ref[0] + mod_ref[0, 0, 2:3, :] * y


def _merge(xs, ya, yb, yc, mg, modtab, wa, wb, wc, wo, n_lat, n_tiles):
    b, _, d = xs.shape
    tm = TOKEN_TILE
    row = lambda bi, i: (bi, i, 0)
    return pl.pallas_call(
        _merge_kernel,
        grid=(b, n_tiles),
        in_specs=[
            pl.BlockSpec((1, tm, d), row),
            pl.BlockSpec((1, tm, d), row), pl.BlockSpec((1, tm, d), row),
            pl.BlockSpec((1, tm, d), row), pl.BlockSpec((1, tm, 3 * d), row),
            pl.BlockSpec((1, 1, 8, d), lambda bi, i: (bi, i // n_lat, 0, 0)),
            _const_spec(wa.shape), _const_spec(wb.shape),
            _const_spec(wc.shape), _const_spec(wo.shape),
        ],
        out_specs=pl.BlockSpec((1, tm, d), row),
        out_shape=jax.ShapeDtypeStruct((b, n_tiles * tm, d), F32),
        compiler_params=_cparams(2),
        name="merge",
    )(xs, ya, yb, yc, mg, modtab, wa, wb, wc, wo)


def _ffn_kernel(x_ref, mod_ref, gain_ref, wi_ref, wo_ref, fin_ref, o_ref, *, final):
    x = x_ref[0]
    y = _rms(x) * gain_ref[...]
    h = (y * (1.0 + mod_ref[0, 0, 4:5, :]) + mod_ref[0, 0, 3:4, :]).astype(BF16)
    gate = _dot(h, wi_ref[:, 0:D_FF])
    up = _dot(h, wi_ref[:, D_FF:2 * D_FF])
    a = (gate * jax.nn.sigmoid(gate) * up).astype(BF16)
    x = x + mod_ref[0, 0, 5:6, :] * _dot(a, wo_ref[...])
    if final:
        x = _rms(x) * fin_ref[...]
    o_ref[0] = x


def _ffn(xs, modtab, gain, wi, wo, fin, n_lat, n_tiles, final):
    b, _, d = xs.shape
    tm = TOKEN_TILE
    row = lambda bi, i: (bi, i, 0)
    return pl.pallas_call(
        functools.partial(_ffn_kernel, final=final),
        grid=(b, n_tiles),
        in_specs=[
            pl.BlockSpec((1, tm, d), row),
            pl.BlockSpec((1, 1, 8, d), lambda bi, i: (bi, i // n_lat, 0, 0)),
            _const_spec((1, d)),
            _const_spec(wi.shape), _const_spec(wo.shape),
            _const_spec((1, d)),
        ],
        out_specs=pl.BlockSpec((1, tm, d), row),
        out_shape=jax.ShapeDtypeStruct((b, n_tiles * tm, d), F32),
        compiler_params=_cparams(2),
        name="ffn",
    )(xs, modtab, gain, wi, wo, fin)


def _rope_tables(t_lat, s_tot):
    half = SWA_HEAD_DIM // 4
    inv = np.power(ROPE_BASE, -np.arange(half, dtype=np.float32) / half).astype(np.float32)
    pos = np.arange(t_lat)
    lane = np.arange(SWA_HEAD_DIM)
    p = np.where(lane[None, :] < 2 * half, (pos // GRID_W)[:, None], (pos % GRID_W)[:, None])
    ang = p.astype(np.float32) * inv[lane % half][None, :]
    cos, sin = np.cos(ang), np.sin(ang)
    upper = (lane % (2 * half)) >= half
    tabs = np.stack([cos, np.where(upper, sin, 0.0), np.where(upper, 0.0, -sin)]).astype(np.float32)
    ident = np.zeros((3, s_tot - t_lat, SWA_HEAD_DIM), np.float32)
    ident[0] = 1.0
    return np.concatenate([tabs, ident], axis=1)


def _prep_w_in(w_in):
    parts, idx = [], 0
    for n in IN_SPLITS:
        parts.append(w_in[:, idx:idx + n])
        idx += n
    qa, ka, va, ga, gkf, gkb, qs, ks, vs, cq, ckv, kr, mg = parts
    pad = jnp.zeros((w_in.shape[0], 32), w_in.dtype)
    return jnp.concatenate([qa, ka, va, ga, qs, ks, vs, cq, ckv, kr, gkf, gkb, pad, mg],
                           axis=1).astype(BF16)


def kernel(x, c, ctx, c_ctx, w_mod, b_mod, norm_mix, w_in, w_gk_fwd, b_gk_fwd, w_gk_bwd, b_gk_bwd,
           gla_norm, sinks, q_norm, w_q_up, kv_norm, w_kv_up, w_pa, w_pb, w_pc, w_o,
           norm_ffn, w_ffn_in, w_ffn_out, final_norm):
    b, t_lat, d = x.shape
    l_ctx = ctx.shape[1]
    s_tot = t_lat + l_ctx
    depth = w_mod.shape[0]
    tm = TOKEN_TILE
    assert t_lat % tm == 0 and l_ctx == tm and t_lat % GRID_W == 0 and d == D_MODEL
    n_lat = t_lat // tm
    n_all = s_tot // tm

    cc = jnp.concatenate([c, c_ctx[None, :], jnp.zeros((16 - b - 1, d), F32)], axis=0)
    mod = _modulation(cc, w_mod, b_mod)
    mod_lat = mod[:, :b].reshape(depth, b, 1, 6, d)
    mod_ctx = jnp.broadcast_to(mod[:, b].reshape(depth, 1, 1, 6, d), (depth, b, 1, 6, d))
    modtab = jnp.pad(jnp.concatenate([mod_lat, mod_ctx], axis=2),
                     ((0, 0), (0, 0), (0, 0), (0, 2), (0, 0)))

    t64 = _rope_tables(t_lat, s_tot)
    ident = np.zeros_like(t64)
    ident[0] = 1.0
    rk = np.concatenate([t64, t64], axis=2)
    rq = rk * np.float32(SWA_HEAD_DIM ** -0.5)
    rm = np.concatenate([t64, ident], axis=2)
    rmq = rm * np.float32((MLA_NOPE + MLA_ROPE) ** -0.5)
    rq, rk, rm, rmq = (jnp.asarray(a) for a in (rq, rk, rm, rmq))

    xs = jnp.concatenate([x, ctx], axis=1)
    out = None
    for l in range(depth):
        last = l == depth - 1
        n_out = n_lat if last else n_all
        w_in_l = _prep_w_in(w_in[l])
        wf = jnp.zeros((LANES, GLA_HEADS * GLA_DK), F32).at[
            MISC_GK_LANE:MISC_GK_LANE + GLA_GATE_RANK].set(w_gk_fwd[l])
        wb = jnp.zeros((LANES, GLA_HEADS * GLA_DK), F32).at[
            MISC_GK_LANE + GLA_GATE_RANK:MISC_GK_LANE + 2 * GLA_GATE_RANK].set(w_gk_bwd[l])
        to_heads = lambda w: w.reshape(LANES, GLA_HEADS, GLA_DK).transpose(1, 0, 2).astype(BF16)
        wq = jnp.pad(w_q_up[l].reshape(MLA_Q_RANK, MLA_HEADS, MLA_NOPE + MLA_ROPE),
                     ((0, 0), (0, 0), (0, MLA_QK_PAD - MLA_NOPE - MLA_ROPE))
                     ).reshape(MLA_Q_RANK, MLA_HEADS * MLA_QK_PAD).astype(BF16)

        qk, va, ga, qs, kvx, misc, mg = _inproj(
            xs, modtab[l], norm_mix[l].reshape(1, d), w_in_l, rq, rk, rm, n_lat)
        ya = _gla(qk, va, ga, misc, to_heads(wf), to_heads(wb),
                  b_gk_fwd[l].reshape(GLA_HEADS, 1, GLA_DK), b_gk_bwd[l].reshape(GLA_HEADS, 1, GLA_DK),
                  gla_norm[l].reshape(1, GLA_DV), t_lat)
        yb = _swa(sinks[l], qs, kvx, t_lat, n_out)
        qm, km, vm = _mla_prep(misc, wq, w_kv_up[l].astype(BF16), q_norm[l].reshape(1, -1),
                               kv_norm[l].reshape(1, -1), rmq)
        yc = _mla_attn(qm, km, vm, t_lat, n_out)
        xs = _merge(xs, ya, yb, yc, mg, modtab[l], w_pa[l].astype(BF16), w_pb[l].astype(BF16),
                    w_pc[l].astype(BF16), w_o[l].astype(BF16), n_lat, n_out)
        xs = _ffn(xs, modtab[l], norm_ffn[l].reshape(1, d), w_ffn_in[l].astype(BF16),
                  w_ffn_out[l].astype(BF16), final_norm.reshape(1, d), n_lat, n_out, last)
    return xs
```

```python
import functools

import jax
import jax.numpy as jnp
import numpy as np
from jax import lax
from jax.experimental import pallas as pl
from jax.experimental.pallas import tpu as pltpu

F32 = jnp.float32
BF16 = jnp.bfloat16

D_MODEL = 1024
GRID_W = 64
EPS = 1e-6
ROPE_BASE = 10000.0

GLA_HEADS = 4
GLA_DK = 128
GLA_DV = 256
GLA_GATE_RANK = 16
GLA_GATE_NORM = 16.0
GLA_CHUNK = 64

SWA_HEADS = 16
SWA_KV_HEADS = 2
SWA_GROUP = SWA_HEADS // SWA_KV_HEADS
SWA_HEAD_DIM = 64
WINDOW = 128

MLA_HEADS = 8
MLA_Q_RANK = 384
MLA_KV_RANK = 256
MLA_NOPE = 128
MLA_ROPE = 64
MLA_V = 128
MLA_QK_PAD = 256
MLA_VT_ROWS = MLA_V + 16
LOG2_E = 1.4426950408889634

D_FF = -(-(8 * D_MODEL) // (3 * 256)) * 256

IN_SPLITS = (
    GLA_HEADS * GLA_DK, GLA_HEADS * GLA_DK, GLA_HEADS * GLA_DV, GLA_HEADS * GLA_DV,
    GLA_GATE_RANK, GLA_GATE_RANK,
    SWA_HEADS * SWA_HEAD_DIM, SWA_KV_HEADS * SWA_HEAD_DIM, SWA_KV_HEADS * SWA_HEAD_DIM,
    MLA_Q_RANK, MLA_KV_RANK, MLA_ROPE,
    3 * D_MODEL,
)

LANES = 128
TOKEN_TILE = 256
MISC_W = 768
MISC_GK_LANE = 64
VMEM_LIMIT = 56 * 1024 * 1024


def _cparams(n_axes):
    return pltpu.CompilerParams(
        dimension_semantics=("arbitrary",) * n_axes, vmem_limit_bytes=VMEM_LIMIT)


def _const_spec(shape):
    nd = len(shape)
    return pl.BlockSpec(shape, lambda *_: (0,) * nd, pipeline_mode=pl.Buffered(1))


def _rope(x, tab_ref):
    return (x * tab_ref[0] + pltpu.roll(x, 16, 1) * tab_ref[1]
            + pltpu.roll(x, LANES - 16, 1) * tab_ref[2])


def _rms(x):
    return x * lax.rsqrt(jnp.mean(x * x, axis=-1, keepdims=True) + EPS)


def _dot(a, b):
    return jnp.dot(a, b, preferred_element_type=F32)


def _dot_nt(a, b):
    return lax.dot_general(a, b, (((1,), (1,)), ((), ())), preferred_element_type=F32)


def _dot_tn(a, b):
    return lax.dot_general(a, b, (((0,), (0,)), ((), ())), preferred_element_type=F32)


def _mod_kernel(c_ref, w_ref, b_ref, o_ref):
    c = c_ref[...]
    a = (c * jax.nn.sigmoid(c)).astype(BF16)
    o_ref[0] = _dot(a, w_ref[0].astype(BF16)) + b_ref[0]


def _modulation(cc, w_mod, b_mod):
    depth, d, n = w_mod.shape
    tn = 1536
    return pl.pallas_call(
        _mod_kernel,
        grid=(depth, n // tn),
        in_specs=[
            pl.BlockSpec(cc.shape, lambda l, j: (0, 0)),
            pl.BlockSpec((1, d, tn), lambda l, j: (l, 0, j)),
            pl.BlockSpec((1, 1, tn), lambda l, j: (l, 0, j)),
        ],
        out_specs=pl.BlockSpec((1, cc.shape[0], tn), lambda l, j: (l, 0, j)),
        out_shape=jax.ShapeDtypeStruct((depth, cc.shape[0], n), F32),
        compiler_params=_cparams(2),
        name="modulation",
    )(cc, w_mod, b_mod.reshape(depth, 1, n))


def _inproj_kernel(x_ref, mod_ref, gain_ref, w_ref, rq_ref, rk_ref, rm_ref,
                   qk_ref, va_ref, ga_ref, qs_ref, kvx_ref, misc_ref, mg_ref):
    x = x_ref[0]
    y = _rms(x) * gain_ref[...]
    h = (y * (1.0 + mod_ref[0, 0, 1:2, :]) + mod_ref[0, 0, 0:1, :]).astype(BF16)

    def proj(c0, c1):
        return _dot(h, w_ref[:, c0:c1])

    qk_ref[0] = proj(0, 1024).astype(BF16)
    va_ref[0] = proj(1024, 2048).astype(BF16)
    r = proj(2048, 3072)
    ga_ref[0] = (r * jax.nn.sigmoid(r)).astype(BF16)

    r = proj(3072, 4096)
    for j in range(1024 // LANES):
        sl = slice(j * LANES, (j + 1) * LANES)
        qs_ref[0, :, sl] = _rope(r[:, sl], rq_ref).astype(BF16)

    r = proj(4096, 4352)
    kk = _rope(r[:, 0:LANES], rk_ref)
    vv = r[:, LANES:2 * LANES]
    kx = pltpu.roll(kk, 64, 1)
    vx = pltpu.roll(vv, 64, 1)
    lo = lax.broadcasted_iota(jnp.int32, kk.shape, 1) < 64
    zero = jnp.zeros_like(kk)
    blocks = (
        jnp.where(lo, kk, zero), jnp.where(lo, zero, kx),
        jnp.where(lo, vv, zero), jnp.where(lo, zero, vx),
        jnp.where(lo, kx, zero), jnp.where(lo, zero, kk),
        jnp.where(lo, vx, zero), jnp.where(lo, zero, vv),
    )
    for j, blk in enumerate(blocks):
        kvx_ref[0, :, j * LANES:(j + 1) * LANES] = blk.astype(BF16)

    r = proj(4352, 4352 + MISC_W)
    misc_ref[0, :, 0:MISC_W - LANES] = r[:, 0:MISC_W - LANES]
    misc_ref[0, :, MISC_W - LANES:MISC_W] = _rope(r[:, MISC_W - LANES:MISC_W], rm_ref)

    for j in range(3):
        r = proj(5120 + j * 1024, 6144 + j * 1024)
        mg_ref[0, :, j * 1024:(j + 1) * 1024] = jax.nn.sigmoid(r).astype(BF16)


def _inproj(xs, modtab, gain, w, rq, rk, rm, n_lat):
    b, s, d = xs.shape
    tm = TOKEN_TILE
    nt = s // tm
    row = lambda bi, i: (bi, i, 0)
    tab = pl.BlockSpec((3, tm, LANES), lambda bi, i: (0, i, 0))
    widths = (1024, 1024, 1024, 1024, 1024, MISC_W, 3072)
    dtypes = (BF16, BF16, BF16, BF16, BF16, F32, BF16)
    return pl.pallas_call(
        _inproj_kernel,
        grid=(b, nt),
        in_specs=[
            pl.BlockSpec((1, tm, d), row),
            pl.BlockSpec((1, 1, 8, d), lambda bi, i: (bi, i // n_lat, 0, 0)),
            _const_spec((1, d)),
            _const_spec(w.shape),
            tab, tab, tab,
        ],
        out_specs=[pl.BlockSpec((1, tm, n), row) for n in widths],
        out_shape=[jax.ShapeDtypeStruct((b, s, n), dt) for n, dt in zip(widths, dtypes)],
        compiler_params=_cparams(2),
        name="inproj",
    )(xs, modtab, gain, w, rq, rk, rm)


def _gla_kernel(q_ref, k_ref, v_ref, g_ref, gk_ref, wf_ref, wb_ref, bf_ref, bb_ref, gn_ref,
                sum_ref, ones_ref, o_ref,
                of_ref, ob_ref, qdf_ref, qdb_ref, af_ref, ab_ref, stf_ref, stb_ref,
                decf_ref, decb_ref, keepf_ref, keepb_ref, *, n_lat):
    c = GLA_CHUNK
    tm = TOKEN_TILE
    per = tm // c
    q_scale = GLA_DK ** -0.5

    ri = lax.broadcasted_iota(jnp.int32, (tm, tm), 0)
    ci = lax.broadcasted_iota(jnp.int32, (tm, tm), 1)
    same = (ri // c) == (ci // c)
    keepf_ref[...] = jnp.where(same & (ci <= ri), 1.0, 0.0)
    keepb_ref[...] = jnp.where(same & (ci > ri), 1.0, 0.0)

    def bulk(t, d, w_ref, b_ref, keep_ref, qd_ref, a_ref, dec_ref, out_ref):
        rows = pl.ds(pl.multiple_of(t * tm, tm), tm)
        z = _dot(gk_ref[0, rows, :].astype(BF16), w_ref[0]) + b_ref[0]
        la = (jnp.minimum(z, 0.0) - jnp.log1p(jnp.exp(-jnp.abs(z)))) * (1.0 / GLA_GATE_NORM)
        hi = la.astype(BF16)
        lo = (la - hi.astype(F32)).astype(BF16)
        cat = jnp.concatenate([hi, lo], axis=1)
        cr = _dot(sum_ref[d], cat)
        cum = cr[0:tm, 0:GLA_DK] + cr[0:tm, GLA_DK:]
        rest = cr[tm:, 0:GLA_DK] + cr[tm:, GLA_DK:]
        tt = _dot_tn(cat, ones_ref[...])
        dec_ref[t] = jnp.exp(tt[0:GLA_DK] + tt[GLA_DK:])
        q = q_ref[0, rows, :].astype(F32) * q_scale
        k = k_ref[0, rows, :].astype(F32)
        v = v_ref[0, rows, :]
        q_dec = (q * jnp.exp(cum)).astype(BF16)
        k_inv = (k * jnp.exp(-cum)).astype(BF16)
        k_end = (k * jnp.exp(rest)).astype(BF16)
        qd_ref[rows, :] = q_dec
        for j in range(per):
            a_ref[t * per + j] = _dot_tn(k_end[j * c:(j + 1) * c], v[j * c:(j + 1) * c])
        scores = jnp.where(keep_ref[...] > 0.5, _dot_nt(q_dec, k_inv), 0.0).astype(BF16)
        out_ref[rows, :] = _dot(scores, v)

    def bulk_body(t, carry):
        bulk(t, 0, wf_ref, bf_ref, keepf_ref, qdf_ref, af_ref, decf_ref, of_ref)
        bulk(t, 1, wb_ref, bb_ref, keepb_ref, qdb_ref, ab_ref, decb_ref, ob_ref)
        return carry

    lax.fori_loop(0, n_lat + 1, bulk_body, 0, unroll=3)

    def scan_tile(t, order, state, a_ref, dec_ref, st_ref):
        dec = dec_ref[t]
        for j in order:
            st_ref[t * per + j] = state.astype(BF16)
            state = dec[:, j:j + 1] * state + a_ref[t * per + j]
        return state

    fwd_order = range(per)
    bwd_order = range(per - 1, -1, -1)
    zero = jnp.zeros((GLA_DK, GLA_DV), F32)
    s_f = scan_tile(n_lat, fwd_order, zero, af_ref, decf_ref, stf_ref)
    s_b = scan_tile(n_lat, bwd_order, zero, ab_ref, decb_ref, stb_ref)
    lax.fori_loop(0, n_lat, lambda i, s: scan_tile(
        i, fwd_order, s, af_ref, decf_ref, stf_ref), s_f)
    lax.fori_loop(0, n_lat, lambda i, s: scan_tile(
        n_lat - 1 - i, bwd_order, s, ab_ref, decb_ref, stb_ref), s_b)

    def post_body(t, carry):
        for j in range(per):
            rows = pl.ds(pl.multiple_of(t * tm + j * c, c), c)
            o = of_ref[rows, :] + ob_ref[rows, :]
            o += _dot(qdf_ref[rows, :], stf_ref[t * per + j])
            o += _dot(qdb_ref[rows, :], stb_ref[t * per + j])
            y = _rms(o) * gn_ref[...]
            o_ref[0, rows, :] = (y * g_ref[0, rows, :].astype(F32)).astype(BF16)
        return carry

    lax.fori_loop(0, n_lat + 1, post_body, 0, unroll=3)


def _gla_constants():
    tm, c = TOKEN_TILE, GLA_CHUNK
    r = np.arange(tm)[:, None]
    col = np.arange(tm)[None, :]
    same = (r // c) == (col // c)
    fwd = np.concatenate([same & (col <= r), same & (col > r)], axis=0)
    bwd = np.concatenate([same & (col >= r), same & (col < r)], axis=0)
    member = (np.arange(tm)[:, None] // c) == np.arange(LANES)[None, :]
    return (jnp.asarray(np.stack([fwd, bwd]).astype(np.float32), BF16),
            jnp.asarray(member.astype(np.float32), BF16))


def _gla(qk, va, ga, misc, wf, wb, bf, bb, gn, t_lat):
    b, s, _ = qk.shape
    tm = TOKEN_TILE
    n_lat = t_lat // tm
    n_tiles = s // tm
    n_chunks = s // GLA_CHUNK
    h = GLA_HEADS
    gk_block = (MISC_W - LANES) // LANES
    sums, member = _gla_constants()
    return pl.pallas_call(
        functools.partial(_gla_kernel, n_lat=n_lat),
        grid=(b, h),
        in_specs=[
            pl.BlockSpec((1, s, GLA_DK), lambda bi, hi: (bi, 0, hi)),
            pl.BlockSpec((1, s, GLA_DK), lambda bi, hi: (bi, 0, h + hi)),
            pl.BlockSpec((1, s, GLA_DV), lambda bi, hi: (bi, 0, hi)),
            pl.BlockSpec((1, s, GLA_DV), lambda bi, hi: (bi, 0, hi)),
            pl.BlockSpec((1, s, LANES), lambda bi, hi: (bi, 0, gk_block)),
            pl.BlockSpec((1, LANES, GLA_DK), lambda bi, hi: (hi, 0, 0)),
            pl.BlockSpec((1, LANES, GLA_DK), lambda bi, hi: (hi, 0, 0)),
            pl.BlockSpec((1, 1, GLA_DK), lambda bi, hi: (hi, 0, 0)),
            pl.BlockSpec((1, 1, GLA_DK), lambda bi, hi: (hi, 0, 0)),
            pl.BlockSpec((1, GLA_DV), lambda bi, hi: (0, 0)),
            _const_spec(sums.shape), _const_spec(member.shape),
        ],
        out_specs=pl.BlockSpec((1, s, GLA_DV), lambda bi, hi: (bi, 0, hi)),
        out_shape=jax.ShapeDtypeStruct((b, s, h * GLA_DV), BF16),
        scratch_shapes=[
            pltpu.VMEM((s, GLA_DV), F32), pltpu.VMEM((s, GLA_DV), F32),
            pltpu.VMEM((s, GLA_DK), BF16), pltpu.VMEM((s, GLA_DK), BF16),
            pltpu.VMEM((n_chunks, GLA_DK, GLA_DV), F32), pltpu.VMEM((n_chunks, GLA_DK, GLA_DV), F32),
            pltpu.VMEM((n_chunks, GLA_DK, GLA_DV), BF16), pltpu.VMEM((n_chunks, GLA_DK, GLA_DV), BF16),
            pltpu.VMEM((n_tiles, GLA_DK, LANES), F32), pltpu.VMEM((n_tiles, GLA_DK, LANES), F32),
            pltpu.VMEM((tm, tm), F32), pltpu.VMEM((tm, tm), F32),
        ],
        compiler_params=_cparams(2),
        name="gla",
    )(qk, qk, va, ga, misc, wf, wb, bf, bb, gn, sums, member)


def _swa_kernel(sink_ref, q_ref, kv_ref, o_ref, *, n_lat, t_lat):
    tq = TOKEN_TILE
    span = tq + 2 * WINDOW
    g = pl.program_id(1)
    i = pl.program_id(2)
    s_tot = kv_ref.shape[1]
    ctx = slice(t_lat, s_tot)
    ka, kb, va, vb = (slice(j * LANES, (j + 1) * LANES) for j in range(4))

    def softmax_parts(parts, sink):
        m = jnp.maximum(functools.reduce(
            jnp.maximum, [jnp.max(p, axis=-1, keepdims=True) for p in parts]), sink)
        es = [jnp.exp(p - m) for p in parts]
        den = functools.reduce(
            lambda a, e: a + jnp.sum(e, axis=-1, keepdims=True), es, jnp.exp(sink - m))
        return es, 1.0 / den

    def head(qp, kcols, vcols, sink, win):
        sc = _dot_nt(qp, kv_ref[0, ctx, kcols])
        if win is None:
            (ec,), inv = softmax_parts([sc], sink)
            return _dot(ec.astype(BF16), kv_ref[0, ctx, vcols]) * inv
        rows, band = win
        sw = jnp.where(band, _dot_nt(qp, kv_ref[0, rows, kcols]), -1e30)
        (ew, ec), inv = softmax_parts([sw, sc], sink)
        acc = _dot(ew.astype(BF16), kv_ref[0, rows, vcols])
        acc += _dot(ec.astype(BF16), kv_ref[0, ctx, vcols])
        return acc * inv

    def tile(win):
        for p in range(SWA_GROUP // 2):
            cols = slice(p * LANES, (p + 1) * LANES)
            qp = q_ref[0, :, cols]
            sink_a = sink_ref[g * SWA_GROUP + 2 * p]
            sink_b = sink_ref[g * SWA_GROUP + 2 * p + 1]
            o_ref[0, :, cols] = (head(qp, ka, va, sink_a, win)
                                 + head(qp, kb, vb, sink_b, win)).astype(BF16)

    @pl.when(i < n_lat)
    def _():
        ws = pl.multiple_of(jnp.clip(i * tq - WINDOW, 0, t_lat - span), WINDOW)
        q_pos = i * tq + lax.broadcasted_iota(jnp.int32, (tq, span), 0)
        k_pos = ws + lax.broadcasted_iota(jnp.int32, (tq, span), 1)
        tile((pl.ds(ws, span), jnp.abs(q_pos - k_pos) <= WINDOW))

    @pl.when(i >= n_lat)
    def _():
        tile(None)


def _swa(sinks, qs, kvx, t_lat, n_tiles):
    b, s, _ = qs.shape
    tq = TOKEN_TILE
    gw = SWA_GROUP * SWA_HEAD_DIM
    return pl.pallas_call(
        functools.partial(_swa_kernel, n_lat=t_lat // tq, t_lat=t_lat),
        grid=(b, SWA_KV_HEADS, n_tiles),
        in_specs=[
            pl.BlockSpec(memory_space=pltpu.SMEM),
            pl.BlockSpec((1, tq, gw), lambda bi, gi, i: (bi, i, gi)),
            pl.BlockSpec((1, s, 4 * LANES), lambda bi, gi, i: (bi, 0, gi)),
        ],
        out_specs=pl.BlockSpec((1, tq, gw), lambda bi, gi, i: (bi, i, gi)),
        out_shape=jax.ShapeDtypeStruct((b, n_tiles * tq, SWA_HEADS * SWA_HEAD_DIM), BF16),
        compiler_params=_cparams(3),
        name="swa",
    )(sinks, qs, kvx)


def _mla_prep_kernel(m_ref, wq_ref, wk_ref, wvt_ref, qn_ref, kvn_ref, rq_ref, q_ref, k_ref, vt_ref):
    scale = (MLA_NOPE + MLA_ROPE) ** -0.5 * LOG2_E
    cq = m_ref[0, :, 0:MLA_Q_RANK]
    ckv = m_ref[0, :, MLA_Q_RANK:MLA_Q_RANK + MLA_KV_RANK]
    kr = m_ref[0, :, MISC_W - LANES:MISC_W]
    lo = lax.broadcasted_iota(jnp.int32, kr.shape, 1) < MLA_ROPE
    kr = jnp.where(lo, kr, 0.0).astype(BF16)

    qf = _dot((_rms(cq) * qn_ref[...]).astype(BF16), wq_ref[...])
    ckvn = (_rms(ckv) * kvn_ref[...]).astype(BF16)
    kn = _dot(ckvn, wk_ref[...])
    vt = _dot_nt(wvt_ref[...], ckvn)
    ones = jnp.ones((MLA_VT_ROWS - MLA_V, vt.shape[1]), BF16)
    for h in range(MLA_HEADS):
        c0 = h * MLA_QK_PAD
        q_ref[0, :, c0:c0 + LANES] = (qf[:, c0:c0 + LANES] * scale).astype(BF16)
        q_ref[0, :, c0 + LANES:c0 + 2 * LANES] = _rope(
            qf[:, c0 + LANES:c0 + 2 * LANES], rq_ref).astype(BF16)
        k_ref[0, :, c0:c0 + LANES] = kn[:, h * MLA_NOPE:(h + 1) * MLA_NOPE].astype(BF16)
        k_ref[0, :, c0 + LANES:c0 + 2 * LANES] = kr
        vt_ref[0, h, 0:MLA_V, :] = vt[h * MLA_V:(h + 1) * MLA_V, :].astype(BF16)
        vt_ref[0, h, MLA_V:MLA_VT_ROWS, :] = ones


def _mla_prep(misc, wq, wk, wvt, qn, kvn, rq):
    b, s, _ = misc.shape
    tm = TOKEN_TILE
    row = lambda bi, i: (bi, i, 0)
    qk_w = MLA_HEADS * MLA_QK_PAD
    return pl.pallas_call(
        _mla_prep_kernel,
        grid=(b, s // tm),
        in_specs=[
            pl.BlockSpec((1, tm, MISC_W), row),
            _const_spec(wq.shape), _const_spec(wk.shape), _const_spec(wvt.shape),
            _const_spec(qn.shape), _const_spec(kvn.shape),
            pl.BlockSpec((3, tm, LANES), lambda bi, i: (0, i, 0)),
        ],
        out_specs=[
            pl.BlockSpec((1, tm, qk_w), row), pl.BlockSpec((1, tm, qk_w), row),
            pl.BlockSpec((1, MLA_HEADS, MLA_VT_ROWS, tm), lambda bi, i: (bi, 0, 0, i)),
        ],
        out_shape=[
            jax.ShapeDtypeStruct((b, s, qk_w), BF16), jax.ShapeDtypeStruct((b, s, qk_w), BF16),
            jax.ShapeDtypeStruct((b, MLA_HEADS, MLA_VT_ROWS, s), BF16),
        ],
        compiler_params=_cparams(2),
        name="mla_prep",
    )(misc, wq, wk, wvt, qn, kvn, rq)


def _mla_attn_kernel(q_ref, k_ref, vt_ref, o_ref, s0_ref, s1_ref, m0_ref, m1_ref,
                     *, n_lat, t_lat, with_ctx):
    tq = TOKEN_TILE

    def q_rows(i):
        return pl.ds(pl.multiple_of(i * tq, tq), tq)

    def finish(acc, rows):
        o_t = acc[0:MLA_V] / acc[MLA_V:MLA_V + 1]
        o_ref[0, rows, :] = o_t.T.astype(BF16)

    def scores(i, s_ref, m_ref):
        q = q_ref[0, q_rows(i), :]
        half = k_ref.shape[1] // 2
        lo = _dot_nt(k_ref[0, 0:half, :], q)
        hi = _dot_nt(k_ref[0, half:, :], q)
        s_ref[0:half, :] = lo
        s_ref[half:, :] = hi
        m = jnp.maximum(jnp.max(lo, axis=0, keepdims=True), jnp.max(hi, axis=0, keepdims=True))
        m_ref[...] = jnp.broadcast_to(m, m_ref.shape)

    def values(i, s_ref, m_ref):
        p_t = jnp.exp2(s_ref[...] - m_ref[0:1, :]).astype(BF16)
        finish(_dot(vt_ref[0, 0], p_t), q_rows(i))

    scores(0, s0_ref, m0_ref)

    def body(j, carry):
        scores(2 * j + 1, s1_ref, m1_ref)
        values(2 * j, s0_ref, m0_ref)
        scores(2 * j + 2, s0_ref, m0_ref)
        values(2 * j + 1, s1_ref, m1_ref)
        return carry

    lax.fori_loop(0, n_lat // 2 - 1, body, 0)
    scores(n_lat - 1, s1_ref, m1_ref)
    values(n_lat - 2, s0_ref, m0_ref)
    values(n_lat - 1, s1_ref, m1_ref)

    if with_ctx:
        s_t = _dot_nt(k_ref[0, t_lat:, :], q_ref[0, t_lat:, :])
        p_t = jnp.exp2(s_t - jnp.max(s_t, axis=0, keepdims=True)).astype(BF16)
        finish(_dot(vt_ref[0, 0, :, t_lat:], p_t), slice(t_lat, t_lat + tq))


def _mla_attn(q, k, vt, t_lat, n_tiles):
    b, s, _ = q.shape
    tq = TOKEN_TILE
    n_lat = t_lat // tq
    return pl.pallas_call(
        functools.partial(_mla_attn_kernel, n_lat=n_lat, t_lat=t_lat, with_ctx=n_tiles > n_lat),
        grid=(b, MLA_HEADS),
        in_specs=[
            pl.BlockSpec((1, s, MLA_QK_PAD), lambda bi, hi: (bi, 0, hi)),
            pl.BlockSpec((1, s, MLA_QK_PAD), lambda bi, hi: (bi, 0, hi)),
            pl.BlockSpec((1, 1, MLA_VT_ROWS, s), lambda bi, hi: (bi, hi, 0, 0)),
        ],
        out_specs=pl.BlockSpec((1, n_tiles * tq, MLA_V), lambda bi, hi: (bi, 0, hi)),
        out_shape=jax.ShapeDtypeStruct((b, n_tiles * tq, MLA_HEADS * MLA_V), BF16),
        scratch_shapes=[pltpu.VMEM((s, tq), F32), pltpu.VMEM((s, tq), F32),
                        pltpu.VMEM((8, tq), F32), pltpu.VMEM((8, tq), F32)],
        compiler_params=_cparams(2),
        name="mla_attn",
    )(q, k, vt)


def _merge_kernel(x_ref, ya_ref, yb_ref, yc_ref, mg_ref, mod_ref,
                  wa_ref, wb_ref, wc_ref, wo_ref, o_ref):
    d = D_MODEL
    u = mg_ref[0, :, 0:d].astype(F32) * _dot(ya_ref[0], wa_ref[...])
    u += mg_ref[0, :, d:2 * d].astype(F32) * _dot(yb_ref[0], wb_ref[...])
    u += mg_ref[0, :, 2 * d:3 * d].astype(F32) * _dot(yc_ref[0], wc_ref[...])
    y = _dot(u.astype(BF16), wo_ref[...])
    o_ref[0] = x_ref[0] + mod_ref[0, 0, 2:3, :] * y


def _merge(xs, ya, yb, yc, mg, modtab, wa, wb, wc, wo, n_lat, n_tiles):
    b, _, d = xs.shape
    tm = TOKEN_TILE
    row = lambda bi, i: (bi, i, 0)
    return pl.pallas_call(
        _merge_kernel,
        grid=(b, n_tiles),
        in_specs=[
            pl.BlockSpec((1, tm, d), row),
            pl.BlockSpec((1, tm, d), row), pl.BlockSpec((1, tm, d), row),
            pl.BlockSpec((1, tm, d), row), pl.BlockSpec((1, tm, 3 * d), row),
            pl.BlockSpec((1, 1, 8, d), lambda bi, i: (bi, i // n_lat, 0, 0)),
            _const_spec(wa.shape), _const_spec(wb.shape),
            _const_spec(wc.shape), _const_spec(wo.shape),
        ],
        out_specs=pl.BlockSpec((1, tm, d), row),
        out_shape=jax.ShapeDtypeStruct((b, n_tiles * tm, d), F32),
        compiler_params=_cparams(2),
        name="merge",
    )(xs, ya, yb, yc, mg, modtab, wa, wb, wc, wo)


def _ffn_kernel(x_ref, mod_ref, gain_ref, wi_ref, wo_ref, fin_ref, o_ref, *, final):
    x = x_ref[0]
    y = _rms(x) * gain_ref[...]
    h = (y * (1.0 + mod_ref[0, 0, 4:5, :]) + mod_ref[0, 0, 3:4, :]).astype(BF16)
    gate = _dot(h, wi_ref[:, 0:D_FF])
    up = _dot(h, wi_ref[:, D_FF:2 * D_FF])
    a = (gate * jax.nn.sigmoid(gate) * up).astype(BF16)
    x = x + mod_ref[0, 0, 5:6, :] * _dot(a, wo_ref[...])
    if final:
        x = _rms(x) * fin_ref[...]
    o_ref[0] = x


def _ffn(xs, modtab, gain, wi, wo, fin, n_lat, n_tiles, final):
    b, _, d = xs.shape
    tm = TOKEN_TILE
    row = lambda bi, i: (bi, i, 0)
    return pl.pallas_call(
        functools.partial(_ffn_kernel, final=final),
        grid=(b, n_tiles),
        in_specs=[
            pl.BlockSpec((1, tm, d), row),
            pl.BlockSpec((1, 1, 8, d), lambda bi, i: (bi, i // n_lat, 0, 0)),
            _const_spec((1, d)),
            _const_spec(wi.shape), _const_spec(wo.shape),
            _const_spec((1, d)),
        ],
        out_specs=pl.BlockSpec((1, tm, d), row),
        out_shape=jax.ShapeDtypeStruct((b, n_tiles * tm, d), F32),
        compiler_params=_cparams(2),
        name="ffn",
    )(xs, modtab, gain, wi, wo, fin)


def _rope_tables(t_lat, s_tot):
    half = SWA_HEAD_DIM // 4
    inv = np.power(ROPE_BASE, -np.arange(half, dtype=np.float32) / half).astype(np.float32)
    pos = np.arange(t_lat)
    lane = np.arange(SWA_HEAD_DIM)
    p = np.where(lane[None, :] < 2 * half, (pos // GRID_W)[:, None], (pos % GRID_W)[:, None])
    ang = p.astype(np.float32) * inv[lane % half][None, :]
    cos, sin = np.cos(ang), np.sin(ang)
    upper = (lane % (2 * half)) >= half
    tabs = np.stack([cos, np.where(upper, sin, 0.0), np.where(upper, 0.0, -sin)]).astype(np.float32)
    ident = np.zeros((3, s_tot - t_lat, SWA_HEAD_DIM), np.float32)
    ident[0] = 1.0
    return np.concatenate([tabs, ident], axis=1)


def _prep_w_in(w_in):
    parts, idx = [], 0
    for n in IN_SPLITS:
        parts.append(w_in[:, idx:idx + n])
        idx += n
    qa, ka, va, ga, gkf, gkb, qs, ks, vs, cq, ckv, kr, mg = parts
    pad = jnp.zeros((w_in.shape[0], 32), w_in.dtype)
    return jnp.concatenate([qa, ka, va, ga, qs, ks, vs, cq, ckv, kr, gkf, gkb, pad, mg],
                           axis=1).astype(BF16)


def kernel(x, c, ctx, c_ctx, w_mod, b_mod, norm_mix, w_in, w_gk_fwd, b_gk_fwd, w_gk_bwd, b_gk_bwd,
           gla_norm, sinks, q_norm, w_q_up, kv_norm, w_kv_up, w_pa, w_pb, w_pc, w_o,
           norm_ffn, w_ffn_in, w_ffn_out, final_norm):
    b, t_lat, d = x.shape
    l_ctx = ctx.shape[1]
    s_tot = t_lat + l_ctx
    depth = w_mod.shape[0]
    tm = TOKEN_TILE
    assert t_lat % tm == 0 and l_ctx == tm and t_lat % GRID_W == 0 and d == D_MODEL
    n_lat = t_lat // tm
    n_all = s_tot // tm

    cc = jnp.concatenate([c, c_ctx[None, :], jnp.zeros((16 - b - 1, d), F32)], axis=0)
    mod = _modulation(cc, w_mod, b_mod)
    mod_lat = mod[:, :b].reshape(depth, b, 1, 6, d)
    mod_ctx = jnp.broadcast_to(mod[:, b].reshape(depth, 1, 1, 6, d), (depth, b, 1, 6, d))
    modtab = jnp.pad(jnp.concatenate([mod_lat, mod_ctx], axis=2),
                     ((0, 0), (0, 0), (0, 0), (0, 2), (0, 0)))

    t64 = _rope_tables(t_lat, s_tot)
    ident = np.zeros_like(t64)
    ident[0] = 1.0
    rk = np.concatenate([t64, t64], axis=2)
    rq = rk * np.float32(SWA_HEAD_DIM ** -0.5)
    rm = np.concatenate([t64, ident], axis=2)
    rmq = rm * np.float32((MLA_NOPE + MLA_ROPE) ** -0.5 * LOG2_E)
    rq, rk, rm, rmq = (jnp.asarray(a) for a in (rq, rk, rm, rmq))

    xs = jnp.concatenate([x, ctx], axis=1)
    out = None
    for l in range(depth):
        last = l == depth - 1
        n_out = n_lat if last else n_all
        w_in_l = _prep_w_in(w_in[l])
        wf = jnp.zeros((LANES, GLA_HEADS * GLA_DK), F32).at[
            MISC_GK_LANE:MISC_GK_LANE + GLA_GATE_RANK].set(w_gk_fwd[l])
        wb = jnp.zeros((LANES, GLA_HEADS * GLA_DK), F32).at[
            MISC_GK_LANE + GLA_GATE_RANK:MISC_GK_LANE + 2 * GLA_GATE_RANK].set(w_gk_bwd[l])
        to_heads = lambda w: w.reshape(LANES, GLA_HEADS, GLA_DK).transpose(1, 0, 2).astype(BF16)
        wq = jnp.pad(w_q_up[l].reshape(MLA_Q_RANK, MLA_HEADS, MLA_NOPE + MLA_ROPE),
                     ((0, 0), (0, 0), (0, MLA_QK_PAD - MLA_NOPE - MLA_ROPE))
                     ).reshape(MLA_Q_RANK, MLA_HEADS * MLA_QK_PAD).astype(BF16)

        qk, va, ga, qs, kvx, misc, mg = _inproj(
            xs, modtab[l], norm_mix[l].reshape(1, d), w_in_l, rq, rk, rm, n_lat)
        ya = _gla(qk, va, ga, misc, to_heads(wf), to_heads(wb),
                  b_gk_fwd[l].reshape(GLA_HEADS, 1, GLA_DK), b_gk_bwd[l].reshape(GLA_HEADS, 1, GLA_DK),
                  gla_norm[l].reshape(1, GLA_DV), t_lat)
        yb = _swa(sinks[l], qs, kvx, t_lat, n_out)
        wkv = w_kv_up[l].reshape(MLA_KV_RANK, MLA_HEADS, MLA_NOPE + MLA_V)
        wk = wkv[:, :, :MLA_NOPE].reshape(MLA_KV_RANK, MLA_HEADS * MLA_NOPE).astype(BF16)
        wvt = wkv[:, :, MLA_NOPE:].reshape(MLA_KV_RANK, MLA_HEADS * MLA_V).T.astype(BF16)
        qm, km, vm = _mla_prep(misc, wq, wk, wvt, q_norm[l].reshape(1, -1),
                               kv_norm[l].reshape(1, -1), rmq)
        yc = _mla_attn(qm, km, vm, t_lat, n_out)
        xs = _merge(xs, ya, yb, yc, mg, modtab[l], w_pa[l].astype(BF16), w_pb[l].astype(BF16),
                    w_pc[l].astype(BF16), w_o[l].astype(BF16), n_lat, n_out)
        xs = _ffn(xs, modtab[l], norm_ffn[l].reshape(1, d), w_ffn_in[l].astype(BF16),
                  w_ffn_out[l].astype(BF16), final_norm.reshape(1, d), n_lat, n_out, last)
    return xs
```

```python
import functools

import jax
import jax.numpy as jnp
import numpy as np
from jax import lax
from jax.experimental import pallas as pl
from jax.experimental.pallas import tpu as pltpu

F32 = jnp.float32
BF16 = jnp.bfloat16

D_MODEL = 1024
GRID_W = 64
EPS = 1e-6
ROPE_BASE = 10000.0

GLA_HEADS = 4
GLA_DK = 128
GLA_DV = 256
GLA_GATE_RANK = 16
GLA_GATE_NORM = 16.0
GLA_CHUNK = 64

SWA_HEADS = 16
SWA_KV_HEADS = 2
SWA_GROUP = SWA_HEADS // SWA_KV_HEADS
SWA_HEAD_DIM = 64
WINDOW = 128
SWA_VT_ROWS = SWA_HEAD_DIM + 16

MLA_HEADS = 8
MLA_Q_RANK = 384
MLA_KV_RANK = 256
MLA_NOPE = 128
MLA_ROPE = 64
MLA_V = 128
MLA_QK_PAD = 256
MLA_VT_ROWS = MLA_V + 16
LOG2_E = 1.4426950408889634

D_FF = -(-(8 * D_MODEL) // (3 * 256)) * 256

IN_SPLITS = (
    GLA_HEADS * GLA_DK, GLA_HEADS * GLA_DK, GLA_HEADS * GLA_DV, GLA_HEADS * GLA_DV,
    GLA_GATE_RANK, GLA_GATE_RANK,
    SWA_HEADS * SWA_HEAD_DIM, SWA_KV_HEADS * SWA_HEAD_DIM, SWA_KV_HEADS * SWA_HEAD_DIM,
    MLA_Q_RANK, MLA_KV_RANK, MLA_ROPE,
    3 * D_MODEL,
)

LANES = 128
TOKEN_TILE = 256
MISC_W = 768
MISC_GK_LANE = 64
VMEM_LIMIT = 56 * 1024 * 1024


def _cparams(n_axes):
    return pltpu.CompilerParams(
        dimension_semantics=("arbitrary",) * n_axes, vmem_limit_bytes=VMEM_LIMIT)


def _const_spec(shape):
    nd = len(shape)
    return pl.BlockSpec(shape, lambda *_: (0,) * nd, pipeline_mode=pl.Buffered(1))


def _rope(x, tab_ref):
    return (x * tab_ref[0] + pltpu.roll(x, 16, 1) * tab_ref[1]
            + pltpu.roll(x, LANES - 16, 1) * tab_ref[2])


def _rms(x):
    return x * lax.rsqrt(jnp.mean(x * x, axis=-1, keepdims=True) + EPS)


def _dot(a, b):
    return jnp.dot(a, b, preferred_element_type=F32)


def _dot_nt(a, b):
    return lax.dot_general(a, b, (((1,), (1,)), ((), ())), preferred_element_type=F32)


def _dot_tn(a, b):
    return lax.dot_general(a, b, (((0,), (0,)), ((), ())), preferred_element_type=F32)


def _mod_kernel(c_ref, w_ref, b_ref, o_ref):
    c = c_ref[...]
    a = (c * jax.nn.sigmoid(c)).astype(BF16)
    o_ref[0] = _dot(a, w_ref[0].astype(BF16)) + b_ref[0]


def _modulation(cc, w_mod, b_mod):
    depth, d, n = w_mod.shape
    tn = 1536
    return pl.pallas_call(
        _mod_kernel,
        grid=(depth, n // tn),
        in_specs=[
            pl.BlockSpec(cc.shape, lambda l, j: (0, 0)),
            pl.BlockSpec((1, d, tn), lambda l, j: (l, 0, j)),
            pl.BlockSpec((1, 1, tn), lambda l, j: (l, 0, j)),
        ],
        out_specs=pl.BlockSpec((1, cc.shape[0], tn), lambda l, j: (l, 0, j)),
        out_shape=jax.ShapeDtypeStruct((depth, cc.shape[0], n), F32),
        compiler_params=_cparams(2),
        name="modulation",
    )(cc, w_mod, b_mod.reshape(depth, 1, n))


def _inproj_kernel(x_ref, mod_ref, gain_ref, w_ref, wvt_ref, rq_ref, rk_ref, rm_ref,
                   qk_ref, va_ref, ga_ref, qs_ref, kx_ref, vx_ref, misc_ref, mg_ref):
    x = x_ref[0]
    y = _rms(x) * gain_ref[...]
    h = (y * (1.0 + mod_ref[0, 0, 1:2, :]) + mod_ref[0, 0, 0:1, :]).astype(BF16)

    def proj(c0, c1):
        return _dot(h, w_ref[:, c0:c1])

    qk_ref[0] = proj(0, 1024).astype(BF16)
    va_ref[0] = proj(1024, 2048).astype(BF16)
    r = proj(2048, 3072)
    ga_ref[0] = (r * jax.nn.sigmoid(r)).astype(BF16)

    r = proj(3072, 4096)
    for j in range(1024 // LANES):
        sl = slice(j * LANES, (j + 1) * LANES)
        qs_ref[0, :, sl] = _rope(r[:, sl], rq_ref).astype(BF16)

    kk = _rope(proj(4096, 4224), rk_ref)
    kx = pltpu.roll(kk, 64, 1)
    lo = lax.broadcasted_iota(jnp.int32, kk.shape, 1) < 64
    zero = jnp.zeros_like(kk)
    blocks = (jnp.where(lo, kk, zero), jnp.where(lo, zero, kx),
              jnp.where(lo, kx, zero), jnp.where(lo, zero, kk))
    for j, blk in enumerate(blocks):
        kx_ref[0, :, j * LANES:(j + 1) * LANES] = blk.astype(BF16)

    vt = _dot_nt(wvt_ref[...], h).astype(BF16)
    hd = SWA_HEAD_DIM
    ones = jnp.ones((SWA_VT_ROWS - hd, vt.shape[1]), BF16)
    for g in range(SWA_KV_HEADS):
        vx_ref[0, g * SWA_VT_ROWS:g * SWA_VT_ROWS + hd, :] = vt[g * hd:(g + 1) * hd, :]
        vx_ref[0, g * SWA_VT_ROWS + hd:(g + 1) * SWA_VT_ROWS, :] = ones

    r = proj(4224, 4224 + MISC_W)
    misc_ref[0, :, 0:MISC_W - LANES] = r[:, 0:MISC_W - LANES]
    misc_ref[0, :, MISC_W - LANES:MISC_W] = _rope(r[:, MISC_W - LANES:MISC_W], rm_ref)

    c0 = 4224 + MISC_W
    for j in range(3):
        r = proj(c0 + j * 1024, c0 + (j + 1) * 1024)
        mg_ref[0, :, j * 1024:(j + 1) * 1024] = jax.nn.sigmoid(r).astype(BF16)


def _inproj(xs, modtab, gain, w, wvt, rq, rk, rm, n_lat):
    b, s, d = xs.shape
    tm = TOKEN_TILE
    nt = s // tm
    row = lambda bi, i: (bi, i, 0)
    tab = pl.BlockSpec((3, tm, LANES), lambda bi, i: (0, i, 0))
    vx_rows = SWA_KV_HEADS * SWA_VT_ROWS
    rows_out = lambda n, dt: (pl.BlockSpec((1, tm, n), row), jax.ShapeDtypeStruct((b, s, n), dt))
    outs = [rows_out(1024, BF16), rows_out(1024, BF16), rows_out(1024, BF16), rows_out(1024, BF16),
            rows_out(SWA_KV_HEADS * 2 * LANES, BF16),
            (pl.BlockSpec((1, vx_rows, tm), lambda bi, i: (bi, 0, i)),
             jax.ShapeDtypeStruct((b, vx_rows, s), BF16)),
            rows_out(MISC_W, F32), rows_out(3072, BF16)]
    return pl.pallas_call(
        _inproj_kernel,
        grid=(b, nt),
        in_specs=[
            pl.BlockSpec((1, tm, d), row),
            pl.BlockSpec((1, 1, 8, d), lambda bi, i: (bi, i // n_lat, 0, 0)),
            _const_spec((1, d)),
            _const_spec(w.shape), _const_spec(wvt.shape),
            tab, tab, tab,
        ],
        out_specs=[o[0] for o in outs],
        out_shape=[o[1] for o in outs],
        compiler_params=_cparams(2),
        name="inproj",
    )(xs, modtab, gain, w, wvt, rq, rk, rm)


def _gla_kernel(q_ref, k_ref, v_ref, g_ref, gk_ref, wf_ref, wb_ref, bf_ref, bb_ref, gn_ref,
                sum_ref, ones_ref, o_ref,
                of_ref, ob_ref, qdf_ref, qdb_ref, af_ref, ab_ref, stf_ref, stb_ref,
                decf_ref, decb_ref, keepf_ref, keepb_ref, pf0_ref, pb0_ref, pf1_ref, pb1_ref,
                *, n_lat):
    c = GLA_CHUNK
    tm = TOKEN_TILE
    per = tm // c
    q_scale = GLA_DK ** -0.5

    ri = lax.broadcasted_iota(jnp.int32, (tm, tm), 0)
    ci = lax.broadcasted_iota(jnp.int32, (tm, tm), 1)
    same = (ri // c) == (ci // c)
    keepf_ref[...] = jnp.where(same & (ci <= ri), 1.0, 0.0)
    keepb_ref[...] = jnp.where(same & (ci > ri), 1.0, 0.0)

    def tile_rows(t):
        return pl.ds(pl.multiple_of(t * tm, tm), tm)

    def decays(t, d, w_ref, b_ref, qd_ref, dec_ref, stage):
        rows = tile_rows(t)
        z = _dot(gk_ref[0, rows, :].astype(BF16), w_ref[0]) + b_ref[0]
        la = (jnp.minimum(z, 0.0) - jnp.log1p(jnp.exp(-jnp.abs(z)))) * (1.0 / GLA_GATE_NORM)
        hi = la.astype(BF16)
        lo = (la - hi.astype(F32)).astype(BF16)
        cat = jnp.concatenate([hi, lo], axis=1)
        cr = _dot(sum_ref[d], cat)
        cum = cr[0:tm, 0:GLA_DK] + cr[0:tm, GLA_DK:]
        rest = cr[tm:, 0:GLA_DK] + cr[tm:, GLA_DK:]
        tt = _dot_tn(cat, ones_ref[...])
        dec_ref[t] = jnp.exp(tt[0:GLA_DK] + tt[GLA_DK:])
        q = q_ref[0, rows, :].astype(F32) * q_scale
        k = k_ref[0, rows, :].astype(F32)
        q_dec = (q * jnp.exp(cum)).astype(BF16)
        qd_ref[rows, :] = q_dec
        stage[0] = q_dec
        stage[1] = (k * jnp.exp(-cum)).astype(BF16)
        stage[2] = (k * jnp.exp(rest)).astype(BF16)

    def in_chunk(t, keep_ref, a_ref, out_ref, stage):
        v = v_ref[0, tile_rows(t), :]
        k_end = stage[2]
        for j in range(per):
            a_ref[t * per + j] = _dot_tn(k_end[j * c:(j + 1) * c], v[j * c:(j + 1) * c])
        scores = jnp.where(keep_ref[...] > 0.5, _dot_nt(stage[0], stage[1]), 0.0).astype(BF16)
        out_ref[tile_rows(t), :] = _dot(scores, v)

    def stage1(t, bufs):
        decays(t, 0, wf_ref, bf_ref, qdf_ref, decf_ref, bufs[0])
        decays(t, 1, wb_ref, bb_ref, qdb_ref, decb_ref, bufs[1])

    def stage2(t, bufs):
        in_chunk(t, keepf_ref, af_ref, of_ref, bufs[0])
        in_chunk(t, keepb_ref, ab_ref, ob_ref, bufs[1])

    even, odd = (pf0_ref, pb0_ref), (pf1_ref, pb1_ref)
    n_tiles = n_lat + 1
    stage1(0, even)

    def bulk_body(j, carry):
        stage1(2 * j + 1, odd)
        stage2(2 * j, even)
        stage1(2 * j + 2, even)
        stage2(2 * j + 1, odd)
        return carry

    lax.fori_loop(0, (n_tiles - 1) // 2, bulk_body, 0)
    stage2(n_tiles - 1, even)

    def scan_tile(t, order, state, a_ref, dec_ref, st_ref):
        for j in order:
            st_ref[t * per + j] = state.astype(BF16)
            dec = dec_ref[t, :, j * LANES:(j + 1) * LANES]
            state = jnp.concatenate([dec] * (GLA_DV // LANES), axis=1) * state + a_ref[t * per + j]
        return state

    fwd_order = range(per)
    bwd_order = range(per - 1, -1, -1)
    zero = jnp.zeros((GLA_DK, GLA_DV), F32)
    s_f = scan_tile(n_lat, fwd_order, zero, af_ref, decf_ref, stf_ref)
    s_b = scan_tile(n_lat, bwd_order, zero, ab_ref, decb_ref, stb_ref)
    lax.fori_loop(0, n_lat, lambda i, s: scan_tile(
        i, fwd_order, s, af_ref, decf_ref, stf_ref), s_f)
    lax.fori_loop(0, n_lat, lambda i, s: scan_tile(
        n_lat - 1 - i, bwd_order, s, ab_ref, decb_ref, stb_ref), s_b)

    def post_body(t, carry):
        for j in range(per):
            rows = pl.ds(pl.multiple_of(t * tm + j * c, c), c)
            o = of_ref[rows, :] + ob_ref[rows, :]
            o += _dot(qdf_ref[rows, :], stf_ref[t * per + j])
            o += _dot(qdb_ref[rows, :], stb_ref[t * per + j])
            y = _rms(o) * gn_ref[...]
            o_ref[0, rows, :] = (y * g_ref[0, rows, :].astype(F32)).astype(BF16)
        return carry

    lax.fori_loop(0, n_lat + 1, post_body, 0, unroll=3)


def _gla_constants():
    tm, c = TOKEN_TILE, GLA_CHUNK
    r = np.arange(tm)[:, None]
    col = np.arange(tm)[None, :]
    same = (r // c) == (col // c)
    fwd = np.concatenate([same & (col <= r), same & (col > r)], axis=0)
    bwd = np.concatenate([same & (col >= r), same & (col < r)], axis=0)
    member = (np.arange(tm)[:, None] // c) == (np.arange(tm // c * LANES)[None, :] // LANES)
    return (jnp.asarray(np.stack([fwd, bwd]).astype(np.float32), BF16),
            jnp.asarray(member.astype(np.float32), BF16))


def _gla(qk, va, ga, misc, wf, wb, bf, bb, gn, t_lat):
    b, s, _ = qk.shape
    tm = TOKEN_TILE
    n_lat = t_lat // tm
    n_tiles = s // tm
    n_chunks = s // GLA_CHUNK
    h = GLA_HEADS
    gk_block = (MISC_W - LANES) // LANES
    sums, member = _gla_constants()
    return pl.pallas_call(
        functools.partial(_gla_kernel, n_lat=n_lat),
        grid=(b, h),
        in_specs=[
            pl.BlockSpec((1, s, GLA_DK), lambda bi, hi: (bi, 0, hi)),
            pl.BlockSpec((1, s, GLA_DK), lambda bi, hi: (bi, 0, h + hi)),
            pl.BlockSpec((1, s, GLA_DV), lambda bi, hi: (bi, 0, hi)),
            pl.BlockSpec((1, s, GLA_DV), lambda bi, hi: (bi, 0, hi)),
            pl.BlockSpec((1, s, LANES), lambda bi, hi: (bi, 0, gk_block)),
            pl.BlockSpec((1, LANES, GLA_DK), lambda bi, hi: (hi, 0, 0)),
            pl.BlockSpec((1, LANES, GLA_DK), lambda bi, hi: (hi, 0, 0)),
            pl.BlockSpec((1, 1, GLA_DK), lambda bi, hi: (hi, 0, 0)),
            pl.BlockSpec((1, 1, GLA_DK), lambda bi, hi: (hi, 0, 0)),
            pl.BlockSpec((1, GLA_DV), lambda bi, hi: (0, 0)),
            _const_spec(sums.shape), _const_spec(member.shape),
        ],
        out_specs=pl.BlockSpec((1, s, GLA_DV), lambda bi, hi: (bi, 0, hi)),
        out_shape=jax.ShapeDtypeStruct((b, s, h * GLA_DV), BF16),
        scratch_shapes=[
            pltpu.VMEM((s, GLA_DV), F32), pltpu.VMEM((s, GLA_DV), F32),
            pltpu.VMEM((s, GLA_DK), BF16), pltpu.VMEM((s, GLA_DK), BF16),
            pltpu.VMEM((n_chunks, GLA_DK, GLA_DV), F32), pltpu.VMEM((n_chunks, GLA_DK, GLA_DV), F32),
            pltpu.VMEM((n_chunks, GLA_DK, GLA_DV), BF16), pltpu.VMEM((n_chunks, GLA_DK, GLA_DV), BF16),
            pltpu.VMEM((n_tiles, GLA_DK, tm // GLA_CHUNK * LANES), F32),
            pltpu.VMEM((n_tiles, GLA_DK, tm // GLA_CHUNK * LANES), F32),
            pltpu.VMEM((tm, tm), F32), pltpu.VMEM((tm, tm), F32),
        ] + [pltpu.VMEM((3, tm, GLA_DK), BF16)] * 4,
        compiler_params=_cparams(2),
        name="gla",
    )(qk, qk, va, ga, misc, wf, wb, bf, bb, gn, sums, member)


def _swa_kernel(sink_ref, q_ref, kx_ref, vx_ref, bias_ref, o_ref, s0_ref, s1_ref, m0_ref, m1_ref,
                *, n_lat, t_lat, with_ctx):
    tq = TOKEN_TILE
    span = tq + 2 * WINDOW
    hd = SWA_HEAD_DIM
    g = pl.program_id(1)
    ka, kb = slice(0, LANES), slice(LANES, 2 * LANES)

    def q_rows(i):
        return pl.ds(pl.multiple_of(i * tq, tq), tq)

    def sink_of(head):
        return sink_ref[g * SWA_GROUP + head] * LOG2_E

    def keys(i):
        ws = pl.multiple_of(jnp.clip(i * tq - WINDOW, 0, t_lat - span), WINDOW)
        win = pl.ds(ws, span)
        kc = [jnp.concatenate([kx_ref[0, win, c], kx_ref[0, t_lat:, c]], axis=0) for c in (ka, kb)]
        vc = jnp.concatenate([vx_ref[0, :, win], vx_ref[0, :, t_lat:]], axis=1)
        return kc, vc, (i * tq - ws) // WINDOW

    def scores(i, kc, bias_idx, head, s_ref, m_ref):
        qp = q_ref[0, q_rows(i), (head // 2) * LANES:(head // 2 + 1) * LANES]
        s_t = _dot_nt(kc[head % 2], qp)
        top = s_t[0:span] + bias_ref[bias_idx]
        bot = s_t[span:]
        s_ref[0:span, :] = top
        s_ref[span:, :] = bot
        m = jnp.maximum(jnp.max(top, axis=0, keepdims=True), jnp.max(bot, axis=0, keepdims=True))
        m_ref[...] = jnp.broadcast_to(jnp.maximum(m, sink_of(head)), m_ref.shape)

    def normalise(acc, m, head):
        return acc[0:hd] / (acc[hd:hd + 1] + jnp.exp2(sink_of(head) - m))

    def values(vc, head, s_ref, m_ref):
        m = m_ref[0:1, :]
        p_t = jnp.exp2(s_ref[...] - m).astype(BF16)
        return normalise(_dot(vc, p_t), m, head)

    def store(i, pair, o_a, o_b):
        o_t = jnp.concatenate([o_a, o_b], axis=0)
        o_ref[0, q_rows(i), pair * LANES:(pair + 1) * LANES] = o_t.T.astype(BF16)

    def tile(i, last):
        kc, vc, bias_idx = keys(i)
        for pair in range(SWA_GROUP // 2):
            a, b = 2 * pair, 2 * pair + 1
            scores(i, kc, bias_idx, b, s1_ref, m1_ref)
            o_a = values(vc, a, s0_ref, m0_ref)
            if b + 1 < SWA_GROUP:
                scores(i, kc, bias_idx, b + 1, s0_ref, m0_ref)
            elif not last:
                kn, _, bn = keys(i + 1)
                scores(i + 1, kn, bn, 0, s0_ref, m0_ref)
            o_b = values(vc, b, s1_ref, m1_ref)
            store(i, pair, o_a, o_b)

    kc0, _, b0 = keys(0)
    scores(0, kc0, b0, 0, s0_ref, m0_ref)

    def body(i, carry):
        tile(i, False)
        return carry

    lax.fori_loop(0, n_lat - 1, body, 0)
    tile(n_lat - 1, True)

    if with_ctx:
        rows = slice(t_lat, t_lat + tq)
        for pair in range(SWA_GROUP // 2):
            qp = q_ref[0, rows, pair * LANES:(pair + 1) * LANES]
            outs = []
            for head, kcols in ((2 * pair, ka), (2 * pair + 1, kb)):
                s_t = _dot_nt(kx_ref[0, t_lat:, kcols], qp)
                m = jnp.maximum(jnp.max(s_t, axis=0, keepdims=True), sink_of(head))
                p_t = jnp.exp2(s_t - m).astype(BF16)
                outs.append(normalise(_dot(vx_ref[0, :, t_lat:], p_t), m, head))
            o_t = jnp.concatenate(outs, axis=0)
            o_ref[0, rows, pair * LANES:(pair + 1) * LANES] = o_t.T.astype(BF16)


def _swa_bias():
    tq, span = TOKEN_TILE, TOKEN_TILE + 2 * WINDOW
    r = np.arange(span)[:, None]
    c = np.arange(tq)[None, :]
    tabs = [np.where(np.abs(r - off - c) <= WINDOW, 0.0, -1e30) for off in (0, WINDOW, 2 * WINDOW)]
    return jnp.asarray(np.stack(tabs).astype(np.float32))


def _swa(sinks, qs, kx, vx, t_lat, n_tiles):
    b, s, _ = qs.shape
    tq = TOKEN_TILE
    gw = SWA_GROUP * SWA_HEAD_DIM
    n_keys = tq + 2 * WINDOW + (s - t_lat)
    bias = _swa_bias()
    n_lat = t_lat // tq
    return pl.pallas_call(
        functools.partial(_swa_kernel, n_lat=n_lat, t_lat=t_lat, with_ctx=n_tiles > n_lat),
        grid=(b, SWA_KV_HEADS),
        in_specs=[
            pl.BlockSpec(memory_space=pltpu.SMEM),
            pl.BlockSpec((1, s, gw), lambda bi, gi: (bi, 0, gi)),
            pl.BlockSpec((1, s, 2 * LANES), lambda bi, gi: (bi, 0, gi)),
            pl.BlockSpec((1, SWA_VT_ROWS, s), lambda bi, gi: (bi, gi, 0)),
            _const_spec(bias.shape),
        ],
        out_specs=pl.BlockSpec((1, n_tiles * tq, gw), lambda bi, gi: (bi, 0, gi)),
        out_shape=jax.ShapeDtypeStruct((b, n_tiles * tq, SWA_HEADS * SWA_HEAD_DIM), BF16),
        scratch_shapes=[pltpu.VMEM((n_keys, tq), F32), pltpu.VMEM((n_keys, tq), F32),
                        pltpu.VMEM((8, tq), F32), pltpu.VMEM((8, tq), F32)],
        compiler_params=_cparams(2),
        name="swa",
    )(sinks, qs, kx, vx, bias)


def _mla_prep_kernel(m_ref, wq_ref, wk_ref, wvt_ref, qn_ref, kvn_ref, rq_ref, q_ref, k_ref, vt_ref):
    scale = (MLA_NOPE + MLA_ROPE) ** -0.5 * LOG2_E
    cq = m_ref[0, :, 0:MLA_Q_RANK]
    ckv = m_ref[0, :, MLA_Q_RANK:MLA_Q_RANK + MLA_KV_RANK]
    kr = m_ref[0, :, MISC_W - LANES:MISC_W]
    lo = lax.broadcasted_iota(jnp.int32, kr.shape, 1) < MLA_ROPE
    kr = jnp.where(lo, kr, 0.0).astype(BF16)

    qf = _dot((_rms(cq) * qn_ref[...]).astype(BF16), wq_ref[...])
    ckvn = (_rms(ckv) * kvn_ref[...]).astype(BF16)
    kn = _dot(ckvn, wk_ref[...])
    vt = _dot_nt(wvt_ref[...], ckvn)
    ones = jnp.ones((MLA_VT_ROWS - MLA_V, vt.shape[1]), BF16)
    for h in range(MLA_HEADS):
        c0 = h * MLA_QK_PAD
        q_ref[0, :, c0:c0 + LANES] = (qf[:, c0:c0 + LANES] * scale).astype(BF16)
        q_ref[0, :, c0 + LANES:c0 + 2 * LANES] = _rope(
            qf[:, c0 + LANES:c0 + 2 * LANES], rq_ref).astype(BF16)
        k_ref[0, :, c0:c0 + LANES] = kn[:, h * MLA_NOPE:(h + 1) * MLA_NOPE].astype(BF16)
        k_ref[0, :, c0 + LANES:c0 + 2 * LANES] = kr
        vt_ref[0, h, 0:MLA_V, :] = vt[h * MLA_V:(h + 1) * MLA_V, :].astype(BF16)
        vt_ref[0, h, MLA_V:MLA_VT_ROWS, :] = ones


def _mla_prep(misc, wq, wk, wvt, qn, kvn, rq):
    b, s, _ = misc.shape
    tm = TOKEN_TILE
    row = lambda bi, i: (bi, i, 0)
    qk_w = MLA_HEADS * MLA_QK_PAD
    return pl.pallas_call(
        _mla_prep_kernel,
        grid=(b, s // tm),
        in_specs=[
            pl.BlockSpec((1, tm, MISC_W), row),
            _const_spec(wq.shape), _const_spec(wk.shape), _const_spec(wvt.shape),
            _const_spec(qn.shape), _const_spec(kvn.shape),
            pl.BlockSpec((3, tm, LANES), lambda bi, i: (0, i, 0)),
        ],
        out_specs=[
            pl.BlockSpec((1, tm, qk_w), row), pl.BlockSpec((1, tm, qk_w), row),
            pl.BlockSpec((1, MLA_HEADS, MLA_VT_ROWS, tm), lambda bi, i: (bi, 0, 0, i)),
        ],
        out_shape=[
            jax.ShapeDtypeStruct((b, s, qk_w), BF16), jax.ShapeDtypeStruct((b, s, qk_w), BF16),
            jax.ShapeDtypeStruct((b, MLA_HEADS, MLA_VT_ROWS, s), BF16),
        ],
        compiler_params=_cparams(2),
        name="mla_prep",
    )(misc, wq, wk, wvt, qn, kvn, rq)


def _mla_attn_kernel(q_ref, k_ref, vt_ref, o_ref, s0_ref, s1_ref, m0_ref, m1_ref,
                     *, n_lat, t_lat, with_ctx):
    tq = TOKEN_TILE

    def q_rows(i):
        return pl.ds(pl.multiple_of(i * tq, tq), tq)

    def finish(acc, rows):
        o_t = acc[0:MLA_V] / acc[MLA_V:MLA_V + 1]
        o_ref[0, rows, :] = o_t.T.astype(BF16)

    def scores(i, s_ref, m_ref):
        q = q_ref[0, q_rows(i), :]
        half = k_ref.shape[1] // 2
        lo = _dot_nt(k_ref[0, 0:half, :], q)
        hi = _dot_nt(k_ref[0, half:, :], q)
        s_ref[0:half, :] = lo
        s_ref[half:, :] = hi
        m = jnp.maximum(jnp.max(lo, axis=0, keepdims=True), jnp.max(hi, axis=0, keepdims=True))
        m_ref[...] = jnp.broadcast_to(m, m_ref.shape)

    def values(i, s_ref, m_ref):
        p_t = jnp.exp2(s_ref[...] - m_ref[0:1, :]).astype(BF16)
        finish(_dot(vt_ref[0, 0], p_t), q_rows(i))

    scores(0, s0_ref, m0_ref)

    def body(j, carry):
        scores(2 * j + 1, s1_ref, m1_ref)
        values(2 * j, s0_ref, m0_ref)
        scores(2 * j + 2, s0_ref, m0_ref)
        values(2 * j + 1, s1_ref, m1_ref)
        return carry

    lax.fori_loop(0, n_lat // 2 - 1, body, 0)
    scores(n_lat - 1, s1_ref, m1_ref)
    values(n_lat - 2, s0_ref, m0_ref)
    values(n_lat - 1, s1_ref, m1_ref)

    if with_ctx:
        s_t = _dot_nt(k_ref[0, t_lat:, :], q_ref[0, t_lat:, :])
        p_t = jnp.exp2(s_t - jnp.max(s_t, axis=0, keepdims=True)).astype(BF16)
        finish(_dot(vt_ref[0, 0, :, t_lat:], p_t), slice(t_lat, t_lat + tq))


def _mla_attn(q, k, vt, t_lat, n_tiles):
    b, s, _ = q.shape
    tq = TOKEN_TILE
    n_lat = t_lat // tq
    return pl.pallas_call(
        functools.partial(_mla_attn_kernel, n_lat=n_lat, t_lat=t_lat, with_ctx=n_tiles > n_lat),
        grid=(b, MLA_HEADS),
        in_specs=[
            pl.BlockSpec((1, s, MLA_QK_PAD), lambda bi, hi: (bi, 0, hi)),
            pl.BlockSpec((1, s, MLA_QK_PAD), lambda bi, hi: (bi, 0, hi)),
            pl.BlockSpec((1, 1, MLA_VT_ROWS, s), lambda bi, hi: (bi, hi, 0, 0)),
        ],
        out_specs=pl.BlockSpec((1, n_tiles * tq, MLA_V), lambda bi, hi: (bi, 0, hi)),
        out_shape=jax.ShapeDtypeStruct((b, n_tiles * tq, MLA_HEADS * MLA_V), BF16),
        scratch_shapes=[pltpu.VMEM((s, tq), F32), pltpu.VMEM((s, tq), F32),
                        pltpu.VMEM((8, tq), F32), pltpu.VMEM((8, tq), F32)],
        compiler_params=_cparams(2),
        name="mla_attn",
    )(q, k, vt)


def _merge_kernel(x_ref, ya_ref, yb_ref, yc_ref, mg_ref, mod_ref,
                  wa_ref, wb_ref, wc_ref, wo_ref, o_ref):
    d = D_MODEL
    u = mg_ref[0, :, 0:d].astype(F32) * _dot(ya_ref[0], wa_ref[...])
    u += mg_ref[0, :, d:2 * d].astype(F32) * _dot(yb_ref[0], wb_ref[...])
    u += mg_ref[0, :, 2 * d:3 * d].astype(F32) * _dot(yc_ref[0], wc_ref[...])
    y = _dot(u.astype(BF16), wo_ref[...])
    o_ref[0] = x_ref[0] + mod_ref[0, 0, 2:3, :] * y


def _merge(xs, ya, yb, yc, mg, modtab, wa, wb, wc, wo, n_lat, n_tiles):
    b, _, d = xs.shape
    tm = TOKEN_TILE
    row = lambda bi, i: (bi, i, 0)
    return pl.pallas_call(
        _merge_kernel,
        grid=(b, n_tiles),
        in_specs=[
            pl.BlockSpec((1, tm, d), row),
            pl.BlockSpec((1, tm, d), row), pl.BlockSpec((1, tm, d), row),
            pl.BlockSpec((1, tm, d), row), pl.BlockSpec((1, tm, 3 * d), row),
            pl.BlockSpec((1, 1, 8, d), lambda bi, i: (bi, i // n_lat, 0, 0)),
            _const_spec(wa.shape), _const_spec(wb.shape),
            _const_spec(wc.shape), _const_spec(wo.shape),
        ],
        out_specs=pl.BlockSpec((1, tm, d), row),
        out_shape=jax.ShapeDtypeStruct((b, n_tiles * tm, d), F32),
        compiler_params=_cparams(2),
        name="merge",
    )(xs, ya, yb, yc, mg, modtab, wa, wb, wc, wo)


def _ffn_kernel(x_ref, mod_ref, gain_ref, wi_ref, wo_ref, fin_ref, o_ref, *, final):
    x = x_ref[0]
    y = _rms(x) * gain_ref[...]
    h = (y * (1.0 + mod_ref[0, 0, 4:5, :]) + mod_ref[0, 0, 3:4, :]).astype(BF16)
    gate = _dot(h, wi_ref[:, 0:D_FF])
    up = _dot(h, wi_ref[:, D_FF:2 * D_FF])
    a = (gate * jax.nn.sigmoid(gate) * up).astype(BF16)
    x = x + mod_ref[0, 0, 5:6, :] * _dot(a, wo_ref[...])
    if final:
        x = _rms(x) * fin_ref[...]
    o_ref[0] = x


def _ffn(xs, modtab, gain, wi, wo, fin, n_lat, n_tiles, final):
    b, _, d = xs.shape
    tm = TOKEN_TILE
    row = lambda bi, i: (bi, i, 0)
    return pl.pallas_call(
        functools.partial(_ffn_kernel, final=final),
        grid=(b, n_tiles),
        in_specs=[
            pl.BlockSpec((1, tm, d), row),
            pl.BlockSpec((1, 1, 8, d), lambda bi, i: (bi, i // n_lat, 0, 0)),
            _const_spec((1, d)),
            _const_spec(wi.shape), _const_spec(wo.shape),
            _const_spec((1, d)),
        ],
        out_specs=pl.BlockSpec((1, tm, d), row),
        out_shape=jax.ShapeDtypeStruct((b, n_tiles * tm, d), F32),
        compiler_params=_cparams(2),
        name="ffn",
    )(xs, modtab, gain, wi, wo, fin)


def _rope_tables(t_lat, s_tot):
    half = SWA_HEAD_DIM // 4
    inv = np.power(ROPE_BASE, -np.arange(half, dtype=np.float32) / half).astype(np.float32)
    pos = np.arange(t_lat)
    lane = np.arange(SWA_HEAD_DIM)
    p = np.where(lane[None, :] < 2 * half, (pos // GRID_W)[:, None], (pos % GRID_W)[:, None])
    ang = p.astype(np.float32) * inv[lane % half][None, :]
    cos, sin = np.cos(ang), np.sin(ang)
    upper = (lane % (2 * half)) >= half
    tabs = np.stack([cos, np.where(upper, sin, 0.0), np.where(upper, 0.0, -sin)]).astype(np.float32)
    ident = np.zeros((3, s_tot - t_lat, SWA_HEAD_DIM), np.float32)
    ident[0] = 1.0
    return np.concatenate([tabs, ident], axis=1)


def _prep_w_in(w_in):
    parts, idx = [], 0
    for n in IN_SPLITS:
        parts.append(w_in[:, idx:idx + n])
        idx += n
    qa, ka, va, ga, gkf, gkb, qs, ks, vs, cq, ckv, kr, mg = parts
    pad = jnp.zeros((w_in.shape[0], 32), w_in.dtype)
    rows = jnp.concatenate([qa, ka, va, ga, qs, ks, cq, ckv, kr, gkf, gkb, pad, mg], axis=1)
    return rows.astype(BF16), vs.T.astype(BF16)


def kernel(x, c, ctx, c_ctx, w_mod, b_mod, norm_mix, w_in, w_gk_fwd, b_gk_fwd, w_gk_bwd, b_gk_bwd,
           gla_norm, sinks, q_norm, w_q_up, kv_norm, w_kv_up, w_pa, w_pb, w_pc, w_o,
           norm_ffn, w_ffn_in, w_ffn_out, final_norm):
    b, t_lat, d = x.shape
    l_ctx = ctx.shape[1]
    s_tot = t_lat + l_ctx
    depth = w_mod.shape[0]
    tm = TOKEN_TILE
    assert t_lat % tm == 0 and l_ctx == tm and t_lat % GRID_W == 0 and d == D_MODEL
    n_lat = t_lat // tm
    n_all = s_tot // tm

    cc = jnp.concatenate([c, c_ctx[None, :], jnp.zeros((16 - b - 1, d), F32)], axis=0)
    mod = _modulation(cc, w_mod, b_mod)
    mod_lat = mod[:, :b].reshape(depth, b, 1, 6, d)
    mod_ctx = jnp.broadcast_to(mod[:, b].reshape(depth, 1, 1, 6, d), (depth, b, 1, 6, d))
    modtab = jnp.pad(jnp.concatenate([mod_lat, mod_ctx], axis=2),
                     ((0, 0), (0, 0), (0, 0), (0, 2), (0, 0)))

    t64 = _rope_tables(t_lat, s_tot)
    ident = np.zeros_like(t64)
    ident[0] = 1.0
    rk = np.concatenate([t64, t64], axis=2)
    rq = rk * np.float32(SWA_HEAD_DIM ** -0.5 * LOG2_E)
    rm = np.concatenate([t64, ident], axis=2)
    rmq = rm * np.float32((MLA_NOPE + MLA_ROPE) ** -0.5 * LOG2_E)
    rq, rk, rm, rmq = (jnp.asarray(a) for a in (rq, rk, rm, rmq))

    xs = jnp.concatenate([x, ctx], axis=1)
    out = None
    for l in range(depth):
        last = l == depth - 1
        n_out = n_lat if last else n_all
        w_in_l, w_vs_t = _prep_w_in(w_in[l])
        wf = jnp.zeros((LANES, GLA_HEADS * GLA_DK), F32).at[
            MISC_GK_LANE:MISC_GK_LANE + GLA_GATE_RANK].set(w_gk_fwd[l])
        wb = jnp.zeros((LANES, GLA_HEADS * GLA_DK), F32).at[
            MISC_GK_LANE + GLA_GATE_RANK:MISC_GK_LANE + 2 * GLA_GATE_RANK].set(w_gk_bwd[l])
        to_heads = lambda w: w.reshape(LANES, GLA_HEADS, GLA_DK).transpose(1, 0, 2).astype(BF16)
        wq = jnp.pad(w_q_up[l].reshape(MLA_Q_RANK, MLA_HEADS, MLA_NOPE + MLA_ROPE),
                     ((0, 0), (0, 0), (0, MLA_QK_PAD - MLA_NOPE - MLA_ROPE))
                     ).reshape(MLA_Q_RANK, MLA_HEADS * MLA_QK_PAD).astype(BF16)

        qk, va, ga, qs, kx, vx, misc, mg = _inproj(
            xs, modtab[l], norm_mix[l].reshape(1, d), w_in_l, w_vs_t, rq, rk, rm, n_lat)
        ya = _gla(qk, va, ga, misc, to_heads(wf), to_heads(wb),
                  b_gk_fwd[l].reshape(GLA_HEADS, 1, GLA_DK), b_gk_bwd[l].reshape(GLA_HEADS, 1, GLA_DK),
                  gla_norm[l].reshape(1, GLA_DV), t_lat)
        yb = _swa(sinks[l], qs, kx, vx, t_lat, n_out)
        wkv = w_kv_up[l].reshape(MLA_KV_RANK, MLA_HEADS, MLA_NOPE + MLA_V)
        wk = wkv[:, :, :MLA_NOPE].reshape(MLA_KV_RANK, MLA_HEADS * MLA_NOPE).astype(BF16)
        wvt = wkv[:, :, MLA_NOPE:].reshape(MLA_KV_RANK, MLA_HEADS * MLA_V).T.astype(BF16)
        qm, km, vm = _mla_prep(misc, wq, wk, wvt, q_norm[l].reshape(1, -1),
                               kv_norm[l].reshape(1, -1), rmq)
        yc = _mla_attn(qm, km, vm, t_lat, n_out)
        xs = _merge(xs, ya, yb, yc, mg, modtab[l], w_pa[l].astype(BF16), w_pb[l].astype(BF16),
                    w_pc[l].astype(BF16), w_o[l].astype(BF16), n_lat, n_out)
        xs = _ffn(xs, modtab[l], norm_ffn[l].reshape(1, d), w_ffn_in[l].astype(BF16),
                  w_ffn_out[l].astype(BF16), final_norm.reshape(1, d), n_lat, n_out, last)
    return xs
```

```python
import functools

import jax
import jax.numpy as jnp
import numpy as np
from jax import lax
from jax.experimental import pallas as pl
from jax.experimental.pallas import tpu as pltpu

F32 = jnp.float32
BF16 = jnp.bfloat16

D_MODEL = 1024
GRID_W = 64
EPS = 1e-6
ROPE_BASE = 10000.0

GLA_HEADS = 4
GLA_DK = 128
GLA_DV = 256
GLA_GATE_RANK = 16
GLA_GATE_NORM = 16.0
GLA_CHUNK = 64

SWA_HEADS = 16
SWA_KV_HEADS = 2
SWA_GROUP = SWA_HEADS // SWA_KV_HEADS
SWA_HEAD_DIM = 64
WINDOW = 128
SWA_VT_ROWS = SWA_HEAD_DIM + 16

MLA_HEADS = 8
MLA_Q_RANK = 384
MLA_KV_RANK = 256
MLA_NOPE = 128
MLA_ROPE = 64
MLA_V = 128
MLA_QK_PAD = 256
MLA_VT_ROWS = MLA_V + 16
LOG2_E = 1.4426950408889634

D_FF = -(-(8 * D_MODEL) // (3 * 256)) * 256

IN_SPLITS = (
    GLA_HEADS * GLA_DK, GLA_HEADS * GLA_DK, GLA_HEADS * GLA_DV, GLA_HEADS * GLA_DV,
    GLA_GATE_RANK, GLA_GATE_RANK,
    SWA_HEADS * SWA_HEAD_DIM, SWA_KV_HEADS * SWA_HEAD_DIM, SWA_KV_HEADS * SWA_HEAD_DIM,
    MLA_Q_RANK, MLA_KV_RANK, MLA_ROPE,
    3 * D_MODEL,
)

LANES = 128
TOKEN_TILE = 256
MISC_W = 768
MISC_GK_LANE = 64
VMEM_LIMIT = 56 * 1024 * 1024


def _cparams(n_axes):
    return pltpu.CompilerParams(
        dimension_semantics=("arbitrary",) * n_axes, vmem_limit_bytes=VMEM_LIMIT)


def _const_spec(shape):
    nd = len(shape)
    return pl.BlockSpec(shape, lambda *_: (0,) * nd, pipeline_mode=pl.Buffered(1))


def _rope(x, tab_ref):
    return (x * tab_ref[0] + pltpu.roll(x, 16, 1) * tab_ref[1]
            + pltpu.roll(x, LANES - 16, 1) * tab_ref[2])


def _rms(x):
    return x * lax.rsqrt(jnp.mean(x * x, axis=-1, keepdims=True) + EPS)


def _dot(a, b):
    return jnp.dot(a, b, preferred_element_type=F32)


def _dot_nt(a, b):
    return lax.dot_general(a, b, (((1,), (1,)), ((), ())), preferred_element_type=F32)


def _dot_tn(a, b):
    return lax.dot_general(a, b, (((0,), (0,)), ((), ())), preferred_element_type=F32)


def _mod_kernel(c_ref, w_ref, b_ref, o_ref):
    c = c_ref[...]
    a = (c * jax.nn.sigmoid(c)).astype(BF16)
    o_ref[0] = _dot(a, w_ref[0].astype(BF16)) + b_ref[0]


def _modulation(cc, w_mod, b_mod):
    depth, d, n = w_mod.shape
    tn = 1536
    return pl.pallas_call(
        _mod_kernel,
        grid=(depth, n // tn),
        in_specs=[
            pl.BlockSpec(cc.shape, lambda l, j: (0, 0)),
            pl.BlockSpec((1, d, tn), lambda l, j: (l, 0, j)),
            pl.BlockSpec((1, 1, tn), lambda l, j: (l, 0, j)),
        ],
        out_specs=pl.BlockSpec((1, cc.shape[0], tn), lambda l, j: (l, 0, j)),
        out_shape=jax.ShapeDtypeStruct((depth, cc.shape[0], n), F32),
        compiler_params=_cparams(2),
        name="modulation",
    )(cc, w_mod, b_mod.reshape(depth, 1, n))


def _inproj_kernel(xl_ref, xc_ref, mod_ref, gain_ref, w_ref, wvt_ref, rq_ref, rk_ref, rm_ref,
                   qk_ref, va_ref, ga_ref, qs_ref, kx_ref, vx_ref, misc_ref, mg_ref, *, n_lat):
    x = _pick_tile(xl_ref, xc_ref, n_lat)
    y = _rms(x) * gain_ref[...]
    h = (y * (1.0 + mod_ref[0, 0, 1:2, :]) + mod_ref[0, 0, 0:1, :]).astype(BF16)

    def proj(c0, c1):
        return _dot(h, w_ref[:, c0:c1])

    qk_ref[0] = proj(0, 1024).astype(BF16)
    va_ref[0] = proj(1024, 2048).astype(BF16)
    r = proj(2048, 3072)
    ga_ref[0] = (r * jax.nn.sigmoid(r)).astype(BF16)

    r = proj(3072, 4096)
    for j in range(1024 // LANES):
        sl = slice(j * LANES, (j + 1) * LANES)
        qs_ref[0, :, sl] = _rope(r[:, sl], rq_ref).astype(BF16)

    kk = _rope(proj(4096, 4224), rk_ref)
    kx = pltpu.roll(kk, 64, 1)
    lo = lax.broadcasted_iota(jnp.int32, kk.shape, 1) < 64
    zero = jnp.zeros_like(kk)
    blocks = (jnp.where(lo, kk, zero), jnp.where(lo, zero, kx),
              jnp.where(lo, kx, zero), jnp.where(lo, zero, kk))
    for j, blk in enumerate(blocks):
        kx_ref[0, :, j * LANES:(j + 1) * LANES] = blk.astype(BF16)

    vt = _dot_nt(wvt_ref[...], h).astype(BF16)
    hd = SWA_HEAD_DIM
    ones = jnp.ones((SWA_VT_ROWS - hd, vt.shape[1]), BF16)
    for g in range(SWA_KV_HEADS):
        vx_ref[0, g * SWA_VT_ROWS:g * SWA_VT_ROWS + hd, :] = vt[g * hd:(g + 1) * hd, :]
        vx_ref[0, g * SWA_VT_ROWS + hd:(g + 1) * SWA_VT_ROWS, :] = ones

    r = proj(4224, 4224 + MISC_W)
    misc_ref[0, :, 0:MISC_W - LANES] = r[:, 0:MISC_W - LANES]
    misc_ref[0, :, MISC_W - LANES:MISC_W] = _rope(r[:, MISC_W - LANES:MISC_W], rm_ref)

    c0 = 4224 + MISC_W
    for j in range(3):
        r = proj(c0 + j * 1024, c0 + (j + 1) * 1024)
        mg_ref[0, :, j * 1024:(j + 1) * 1024] = jax.nn.sigmoid(r).astype(BF16)


def _inproj(xs, modtab, gain, w, wvt, rq, rk, rm, n_lat):
    (xl, xc), x_specs = _residual_inputs(xs, n_lat)
    b, _, d = xl.shape
    tm = TOKEN_TILE
    nt = n_lat + 1
    s = nt * tm
    row = lambda bi, i: (bi, i, 0)
    tab = pl.BlockSpec((3, tm, LANES), lambda bi, i: (0, i, 0))
    vx_rows = SWA_KV_HEADS * SWA_VT_ROWS
    rows_out = lambda n, dt: (pl.BlockSpec((1, tm, n), row), jax.ShapeDtypeStruct((b, s, n), dt))
    outs = [rows_out(1024, BF16), rows_out(1024, BF16), rows_out(1024, BF16), rows_out(1024, BF16),
            rows_out(SWA_KV_HEADS * 2 * LANES, BF16),
            (pl.BlockSpec((1, vx_rows, tm), lambda bi, i: (bi, 0, i)),
             jax.ShapeDtypeStruct((b, vx_rows, s), BF16)),
            rows_out(MISC_W, F32), rows_out(3072, BF16)]
    return pl.pallas_call(
        functools.partial(_inproj_kernel, n_lat=n_lat),
        grid=(b, nt),
        in_specs=x_specs + [
            pl.BlockSpec((1, 1, 8, d), lambda bi, i: (bi, i // n_lat, 0, 0)),
            _const_spec((1, d)),
            _const_spec(w.shape), _const_spec(wvt.shape),
            tab, tab, tab,
        ],
        out_specs=[o[0] for o in outs],
        out_shape=[o[1] for o in outs],
        compiler_params=_cparams(2),
        name="inproj",
    )(xl, xc, modtab, gain, w, wvt, rq, rk, rm)


def _gla_kernel(q_ref, k_ref, v_ref, g_ref, gk_ref, wf_ref, wb_ref, bf_ref, bb_ref, gn_ref,
                sum_ref, ones_ref, o_ref,
                of_ref, ob_ref, qdf_ref, qdb_ref, af_ref, ab_ref, stf_ref, stb_ref,
                decf_ref, decb_ref, keepf_ref, keepb_ref, pf0_ref, pb0_ref, pf1_ref, pb1_ref,
                *, n_lat):
    c = GLA_CHUNK
    tm = TOKEN_TILE
    per = tm // c
    q_scale = GLA_DK ** -0.5

    ri = lax.broadcasted_iota(jnp.int32, (tm, tm), 0)
    ci = lax.broadcasted_iota(jnp.int32, (tm, tm), 1)
    same = (ri // c) == (ci // c)
    keepf_ref[...] = jnp.where(same & (ci <= ri), 1.0, 0.0)
    keepb_ref[...] = jnp.where(same & (ci > ri), 1.0, 0.0)

    def tile_rows(t):
        return pl.ds(pl.multiple_of(t * tm, tm), tm)

    def decays(t, d, w_ref, b_ref, qd_ref, dec_ref, stage):
        rows = tile_rows(t)
        z = _dot(gk_ref[0, rows, :].astype(BF16), w_ref[0]) + b_ref[0]
        la = (jnp.minimum(z, 0.0) - jnp.log1p(jnp.exp(-jnp.abs(z)))) * (1.0 / GLA_GATE_NORM)
        hi = la.astype(BF16)
        lo = (la - hi.astype(F32)).astype(BF16)
        cat = jnp.concatenate([hi, lo], axis=1)
        cr = _dot(sum_ref[d], cat)
        cum = cr[:, 0:GLA_DK] + cr[:, GLA_DK:]
        tt = _dot_tn(cat, ones_ref[...])
        dec_ref[t] = jnp.exp(tt[0:GLA_DK] + tt[GLA_DK:])
        q = q_ref[0, rows, :].astype(F32) * q_scale
        k = k_ref[0, rows, :].astype(F32)
        q_dec = (q * jnp.exp(cum)).astype(BF16)
        qd_ref[rows, :] = q_dec
        stage[0] = q_dec
        stage[1] = (k * jnp.exp(-cum)).astype(BF16)

    def in_chunk(t, keep_ref, a_ref, out_ref, stage):
        v = v_ref[0, tile_rows(t), :]
        k_inv = stage[1]
        for j in range(per):
            a_ref[t * per + j] = _dot_tn(k_inv[j * c:(j + 1) * c], v[j * c:(j + 1) * c])
        scores = jnp.where(keep_ref[...] > 0.5, _dot_nt(stage[0], stage[1]), 0.0).astype(BF16)
        out_ref[tile_rows(t), :] = _dot(scores, v)

    def stage1(t, bufs):
        decays(t, 0, wf_ref, bf_ref, qdf_ref, decf_ref, bufs[0])
        decays(t, 1, wb_ref, bb_ref, qdb_ref, decb_ref, bufs[1])

    def stage2(t, bufs):
        in_chunk(t, keepf_ref, af_ref, of_ref, bufs[0])
        in_chunk(t, keepb_ref, ab_ref, ob_ref, bufs[1])

    even, odd = (pf0_ref, pb0_ref), (pf1_ref, pb1_ref)
    n_tiles = n_lat + 1
    stage1(0, even)

    def bulk_body(j, carry):
        stage1(2 * j + 1, odd)
        stage2(2 * j, even)
        stage1(2 * j + 2, even)
        stage2(2 * j + 1, odd)
        return carry

    lax.fori_loop(0, (n_tiles - 1) // 2, bulk_body, 0)
    stage2(n_tiles - 1, even)

    def scan_tile(t, order, state, a_ref, dec_ref, st_ref):
        for j in order:
            st_ref[t * per + j] = state.astype(BF16)
            dec = dec_ref[t, :, j * LANES:(j + 1) * LANES]
            state = jnp.concatenate([dec] * (GLA_DV // LANES), axis=1) * (state + a_ref[t * per + j])
        return state

    fwd_order = range(per)
    bwd_order = range(per - 1, -1, -1)
    zero = jnp.zeros((GLA_DK, GLA_DV), F32)
    s_f = scan_tile(n_lat, fwd_order, zero, af_ref, decf_ref, stf_ref)
    s_b = scan_tile(n_lat, bwd_order, zero, ab_ref, decb_ref, stb_ref)
    lax.fori_loop(0, n_lat, lambda i, s: scan_tile(
        i, fwd_order, s, af_ref, decf_ref, stf_ref), s_f)
    lax.fori_loop(0, n_lat, lambda i, s: scan_tile(
        n_lat - 1 - i, bwd_order, s, ab_ref, decb_ref, stb_ref), s_b)

    def post_body(t, carry):
        for j in range(per):
            rows = pl.ds(pl.multiple_of(t * tm + j * c, c), c)
            o = of_ref[rows, :] + ob_ref[rows, :]
            o += _dot(qdf_ref[rows, :], stf_ref[t * per + j])
            o += _dot(qdb_ref[rows, :], stb_ref[t * per + j])
            y = _rms(o) * gn_ref[...]
            o_ref[0, rows, :] = (y * g_ref[0, rows, :].astype(F32)).astype(BF16)
        return carry

    lax.fori_loop(0, n_lat + 1, post_body, 0, unroll=3)


def _gla_constants():
    tm, c = TOKEN_TILE, GLA_CHUNK
    r = np.arange(tm)[:, None]
    col = np.arange(tm)[None, :]
    same = (r // c) == (col // c)
    fwd = same & (col <= r)
    bwd = same & (col >= r)
    member = (np.arange(tm)[:, None] // c) == (np.arange(tm // c * LANES)[None, :] // LANES)
    return (jnp.asarray(np.stack([fwd, bwd]).astype(np.float32), BF16),
            jnp.asarray(member.astype(np.float32), BF16))


def _gla(qk, va, ga, misc, wf, wb, bf, bb, gn, t_lat):
    b, s, _ = qk.shape
    tm = TOKEN_TILE
    n_lat = t_lat // tm
    n_tiles = s // tm
    n_chunks = s // GLA_CHUNK
    h = GLA_HEADS
    gk_block = (MISC_W - LANES) // LANES
    sums, member = _gla_constants()
    return pl.pallas_call(
        functools.partial(_gla_kernel, n_lat=n_lat),
        grid=(b, h),
        in_specs=[
            pl.BlockSpec((1, s, GLA_DK), lambda bi, hi: (bi, 0, hi)),
            pl.BlockSpec((1, s, GLA_DK), lambda bi, hi: (bi, 0, h + hi)),
            pl.BlockSpec((1, s, GLA_DV), lambda bi, hi: (bi, 0, hi)),
            pl.BlockSpec((1, s, GLA_DV), lambda bi, hi: (bi, 0, hi)),
            pl.BlockSpec((1, s, LANES), lambda bi, hi: (bi, 0, gk_block)),
            pl.BlockSpec((1, LANES, GLA_DK), lambda bi, hi: (hi, 0, 0)),
            pl.BlockSpec((1, LANES, GLA_DK), lambda bi, hi: (hi, 0, 0)),
            pl.BlockSpec((1, 1, GLA_DK), lambda bi, hi: (hi, 0, 0)),
            pl.BlockSpec((1, 1, GLA_DK), lambda bi, hi: (hi, 0, 0)),
            pl.BlockSpec((1, GLA_DV), lambda bi, hi: (0, 0)),
            _const_spec(sums.shape), _const_spec(member.shape),
        ],
        out_specs=pl.BlockSpec((1, s, GLA_DV), lambda bi, hi: (bi, 0, hi)),
        out_shape=jax.ShapeDtypeStruct((b, s, h * GLA_DV), BF16),
        scratch_shapes=[
            pltpu.VMEM((s, GLA_DV), F32), pltpu.VMEM((s, GLA_DV), F32),
            pltpu.VMEM((s, GLA_DK), BF16), pltpu.VMEM((s, GLA_DK), BF16),
            pltpu.VMEM((n_chunks, GLA_DK, GLA_DV), F32), pltpu.VMEM((n_chunks, GLA_DK, GLA_DV), F32),
            pltpu.VMEM((n_chunks, GLA_DK, GLA_DV), BF16), pltpu.VMEM((n_chunks, GLA_DK, GLA_DV), BF16),
            pltpu.VMEM((n_tiles, GLA_DK, tm // GLA_CHUNK * LANES), F32),
            pltpu.VMEM((n_tiles, GLA_DK, tm // GLA_CHUNK * LANES), F32),
            pltpu.VMEM((tm, tm), F32), pltpu.VMEM((tm, tm), F32),
        ] + [pltpu.VMEM((2, tm, GLA_DK), BF16)] * 4,
        compiler_params=_cparams(2),
        name="gla",
    )(qk, qk, va, ga, misc, wf, wb, bf, bb, gn, sums, member)


def _swa_kernel(sink_ref, q_ref, kx_ref, vx_ref, bias_ref, o_ref, s0_ref, s1_ref, m0_ref, m1_ref,
                *, n_lat, t_lat, with_ctx):
    tq = TOKEN_TILE
    span = tq + 2 * WINDOW
    hd = SWA_HEAD_DIM
    g = pl.program_id(1)
    ka, kb = slice(0, LANES), slice(LANES, 2 * LANES)

    def q_rows(i):
        return pl.ds(pl.multiple_of(i * tq, tq), tq)

    def sink_of(head):
        return sink_ref[g * SWA_GROUP + head] * LOG2_E

    def keys(i):
        ws = pl.multiple_of(jnp.clip(i * tq - WINDOW, 0, t_lat - span), WINDOW)
        win = pl.ds(ws, span)
        kc = [jnp.concatenate([kx_ref[0, win, c], kx_ref[0, t_lat:, c]], axis=0) for c in (ka, kb)]
        vc = jnp.concatenate([vx_ref[0, :, win], vx_ref[0, :, t_lat:]], axis=1)
        return kc, vc, (i * tq - ws) // WINDOW

    def scores(i, kc, bias_idx, head, s_ref, m_ref):
        qp = q_ref[0, q_rows(i), (head // 2) * LANES:(head // 2 + 1) * LANES]
        s_t = _dot_nt(kc[head % 2], qp)
        top = s_t[0:span] + bias_ref[bias_idx]
        bot = s_t[span:]
        s_ref[0:span, :] = top
        s_ref[span:, :] = bot
        m = jnp.maximum(jnp.max(top, axis=0, keepdims=True), jnp.max(bot, axis=0, keepdims=True))
        m_ref[...] = jnp.broadcast_to(jnp.maximum(m, sink_of(head)), m_ref.shape)

    def normalise(acc, m, head):
        return acc[0:hd] / (acc[hd:hd + 1] + jnp.exp2(sink_of(head) - m))

    def values(vc, head, s_ref, m_ref):
        m = m_ref[0:1, :]
        p_t = jnp.exp2(s_ref[...] - m).astype(BF16)
        return normalise(_dot(vc, p_t), m, head)

    def store(i, pair, o_a, o_b):
        o_t = jnp.concatenate([o_a, o_b], axis=0)
        o_ref[0, q_rows(i), pair * LANES:(pair + 1) * LANES] = o_t.T.astype(BF16)

    def scores_all(i, s_ref, m_ref):
        kc, _, bias_idx = keys(i)
        for head in range(SWA_GROUP):
            scores(i, kc, bias_idx, head, s_ref.at[head], m_ref.at[head])

    def values_all(i, s_ref, m_ref):
        _, vc, _ = keys(i)
        for pair in range(SWA_GROUP // 2):
            a, b = 2 * pair, 2 * pair + 1
            store(i, pair, values(vc, a, s_ref.at[a], m_ref.at[a]),
                  values(vc, b, s_ref.at[b], m_ref.at[b]))

    scores_all(0, s0_ref, m0_ref)

    def body(j, carry):
        scores_all(2 * j + 1, s1_ref, m1_ref)
        values_all(2 * j, s0_ref, m0_ref)
        scores_all(2 * j + 2, s0_ref, m0_ref)
        values_all(2 * j + 1, s1_ref, m1_ref)
        return carry

    lax.fori_loop(0, n_lat // 2 - 1, body, 0)
    scores_all(n_lat - 1, s1_ref, m1_ref)
    values_all(n_lat - 2, s0_ref, m0_ref)
    values_all(n_lat - 1, s1_ref, m1_ref)

    if with_ctx:
        rows = slice(t_lat, t_lat + tq)
        for pair in range(SWA_GROUP // 2):
            qp = q_ref[0, rows, pair * LANES:(pair + 1) * LANES]
            outs = []
            for head, kcols in ((2 * pair, ka), (2 * pair + 1, kb)):
                s_t = _dot_nt(kx_ref[0, t_lat:, kcols], qp)
                m = jnp.maximum(jnp.max(s_t, axis=0, keepdims=True), sink_of(head))
                p_t = jnp.exp2(s_t - m).astype(BF16)
                outs.append(normalise(_dot(vx_ref[0, :, t_lat:], p_t), m, head))
            o_t = jnp.concatenate(outs, axis=0)
            o_ref[0, rows, pair * LANES:(pair + 1) * LANES] = o_t.T.astype(BF16)


def _swa_bias():
    tq, span = TOKEN_TILE, TOKEN_TILE + 2 * WINDOW
    r = np.arange(span)[:, None]
    c = np.arange(tq)[None, :]
    tabs = [np.where(np.abs(r - off - c) <= WINDOW, 0.0, -1e30) for off in (0, WINDOW, 2 * WINDOW)]
    return jnp.asarray(np.stack(tabs).astype(np.float32))


def _swa(sinks, qs, kx, vx, t_lat, n_tiles):
    b, s, _ = qs.shape
    tq = TOKEN_TILE
    gw = SWA_GROUP * SWA_HEAD_DIM
    n_keys = tq + 2 * WINDOW + (s - t_lat)
    bias = _swa_bias()
    n_lat = t_lat // tq
    return pl.pallas_call(
        functools.partial(_swa_kernel, n_lat=n_lat, t_lat=t_lat, with_ctx=n_tiles > n_lat),
        grid=(b, SWA_KV_HEADS),
        in_specs=[
            pl.BlockSpec(memory_space=pltpu.SMEM),
            pl.BlockSpec((1, s, gw), lambda bi, gi: (bi, 0, gi)),
            pl.BlockSpec((1, s, 2 * LANES), lambda bi, gi: (bi, 0, gi)),
            pl.BlockSpec((1, SWA_VT_ROWS, s), lambda bi, gi: (bi, gi, 0)),
            _const_spec(bias.shape),
        ],
        out_specs=pl.BlockSpec((1, n_tiles * tq, gw), lambda bi, gi: (bi, 0, gi)),
        out_shape=jax.ShapeDtypeStruct((b, n_tiles * tq, SWA_HEADS * SWA_HEAD_DIM), BF16),
        scratch_shapes=[pltpu.VMEM((SWA_GROUP, n_keys, tq), F32), pltpu.VMEM((SWA_GROUP, n_keys, tq), F32),
                        pltpu.VMEM((SWA_GROUP, 8, tq), F32), pltpu.VMEM((SWA_GROUP, 8, tq), F32)],
        compiler_params=_cparams(2),
        name="swa",
    )(sinks, qs, kx, vx, bias)


def _mla_prep_kernel(m_ref, wq_ref, wk_ref, wvt_ref, qn_ref, kvn_ref, rq_ref, q_ref, k_ref, vt_ref):
    scale = (MLA_NOPE + MLA_ROPE) ** -0.5 * LOG2_E
    cq = m_ref[0, :, 0:MLA_Q_RANK]
    ckv = m_ref[0, :, MLA_Q_RANK:MLA_Q_RANK + MLA_KV_RANK]
    kr = m_ref[0, :, MISC_W - LANES:MISC_W]
    lo = lax.broadcasted_iota(jnp.int32, kr.shape, 1) < MLA_ROPE
    kr = jnp.where(lo, kr, 0.0).astype(BF16)

    qf = _dot((_rms(cq) * qn_ref[...]).astype(BF16), wq_ref[...])
    ckvn = (_rms(ckv) * kvn_ref[...]).astype(BF16)
    kn = _dot(ckvn, wk_ref[...])
    vt = _dot_nt(wvt_ref[...], ckvn)
    ones = jnp.ones((MLA_VT_ROWS - MLA_V, vt.shape[1]), BF16)
    for h in range(MLA_HEADS):
        c0 = h * MLA_QK_PAD
        q_ref[0, :, c0:c0 + LANES] = (qf[:, c0:c0 + LANES] * scale).astype(BF16)
        q_ref[0, :, c0 + LANES:c0 + 2 * LANES] = _rope(
            qf[:, c0 + LANES:c0 + 2 * LANES], rq_ref).astype(BF16)
        k_ref[0, :, c0:c0 + LANES] = kn[:, h * MLA_NOPE:(h + 1) * MLA_NOPE].astype(BF16)
        k_ref[0, :, c0 + LANES:c0 + 2 * LANES] = kr
        vt_ref[0, h, 0:MLA_V, :] = vt[h * MLA_V:(h + 1) * MLA_V, :].astype(BF16)
        vt_ref[0, h, MLA_V:MLA_VT_ROWS, :] = ones


def _mla_prep(misc, wq, wk, wvt, qn, kvn, rq):
    b, s, _ = misc.shape
    tm = TOKEN_TILE
    row = lambda bi, i: (bi, i, 0)
    qk_w = MLA_HEADS * MLA_QK_PAD
    return pl.pallas_call(
        _mla_prep_kernel,
        grid=(b, s // tm),
        in_specs=[
            pl.BlockSpec((1, tm, MISC_W), row),
            _const_spec(wq.shape), _const_spec(wk.shape), _const_spec(wvt.shape),
            _const_spec(qn.shape), _const_spec(kvn.shape),
            pl.BlockSpec((3, tm, LANES), lambda bi, i: (0, i, 0)),
        ],
        out_specs=[
            pl.BlockSpec((1, tm, qk_w), row), pl.BlockSpec((1, tm, qk_w), row),
            pl.BlockSpec((1, MLA_HEADS, MLA_VT_ROWS, tm), lambda bi, i: (bi, 0, 0, i)),
        ],
        out_shape=[
            jax.ShapeDtypeStruct((b, s, qk_w), BF16), jax.ShapeDtypeStruct((b, s, qk_w), BF16),
            jax.ShapeDtypeStruct((b, MLA_HEADS, MLA_VT_ROWS, s), BF16),
        ],
        compiler_params=_cparams(2),
        name="mla_prep",
    )(misc, wq, wk, wvt, qn, kvn, rq)


def _mla_attn_kernel(q_ref, k_ref, vt_ref, o_ref, s0_ref, s1_ref, m0_ref, m1_ref,
                     *, n_lat, t_lat, with_ctx):
    tq = TOKEN_TILE

    def q_rows(i):
        return pl.ds(pl.multiple_of(i * tq, tq), tq)

    def finish(acc, rows):
        o_t = acc[0:MLA_V] / acc[MLA_V:MLA_V + 1]
        o_ref[0, rows, :] = o_t.T.astype(BF16)

    def scores(i, s_ref, m_ref):
        q = q_ref[0, q_rows(i), :]
        half = k_ref.shape[1] // 2
        lo = _dot_nt(k_ref[0, 0:half, :], q)
        hi = _dot_nt(k_ref[0, half:, :], q)
        s_ref[0:half, :] = lo
        s_ref[half:, :] = hi
        m = jnp.maximum(jnp.max(lo, axis=0, keepdims=True), jnp.max(hi, axis=0, keepdims=True))
        m_ref[...] = jnp.broadcast_to(m, m_ref.shape)

    def values(i, s_ref, m_ref):
        p_t = jnp.exp2(s_ref[...] - m_ref[0:1, :]).astype(BF16)
        finish(_dot(vt_ref[0, 0], p_t), q_rows(i))

    scores(0, s0_ref, m0_ref)

    def body(j, carry):
        scores(2 * j + 1, s1_ref, m1_ref)
        values(2 * j, s0_ref, m0_ref)
        scores(2 * j + 2, s0_ref, m0_ref)
        values(2 * j + 1, s1_ref, m1_ref)
        return carry

    lax.fori_loop(0, n_lat // 2 - 1, body, 0)
    scores(n_lat - 1, s1_ref, m1_ref)
    values(n_lat - 2, s0_ref, m0_ref)
    values(n_lat - 1, s1_ref, m1_ref)

    if with_ctx:
        s_t = _dot_nt(k_ref[0, t_lat:, :], q_ref[0, t_lat:, :])
        p_t = jnp.exp2(s_t - jnp.max(s_t, axis=0, keepdims=True)).astype(BF16)
        finish(_dot(vt_ref[0, 0, :, t_lat:], p_t), slice(t_lat, t_lat + tq))


def _mla_attn(q, k, vt, t_lat, n_tiles):
    b, s, _ = q.shape
    tq = TOKEN_TILE
    n_lat = t_lat // tq
    return pl.pallas_call(
        functools.partial(_mla_attn_kernel, n_lat=n_lat, t_lat=t_lat, with_ctx=n_tiles > n_lat),
        grid=(b, MLA_HEADS),
        in_specs=[
            pl.BlockSpec((1, s, MLA_QK_PAD), lambda bi, hi: (bi, 0, hi)),
            pl.BlockSpec((1, s, MLA_QK_PAD), lambda bi, hi: (bi, 0, hi)),
            pl.BlockSpec((1, 1, MLA_VT_ROWS, s), lambda bi, hi: (bi, hi, 0, 0)),
        ],
        out_specs=pl.BlockSpec((1, n_tiles * tq, MLA_V), lambda bi, hi: (bi, 0, hi)),
        out_shape=jax.ShapeDtypeStruct((b, n_tiles * tq, MLA_HEADS * MLA_V), BF16),
        scratch_shapes=[pltpu.VMEM((s, tq), F32), pltpu.VMEM((s, tq), F32),
                        pltpu.VMEM((8, tq), F32), pltpu.VMEM((8, tq), F32)],
        compiler_params=_cparams(2),
        name="mla_attn",
    )(q, k, vt)


def _residual_inputs(xs, n_lat):
    tm = TOKEN_TILE
    if isinstance(xs, tuple):
        lat, ctx, ctx_block = xs[0], xs[1], 0
    else:
        lat, ctx, ctx_block = xs, xs, n_lat
    d = lat.shape[-1]
    specs = [pl.BlockSpec((1, tm, d), lambda bi, i: (bi, jnp.minimum(i, n_lat - 1), 0)),
             pl.BlockSpec((1, tm, d), lambda bi, i: (bi, ctx_block, 0))]
    return (lat, ctx), specs


def _pick_tile(xl_ref, xc_ref, n_lat):
    return jnp.where(pl.program_id(1) < n_lat, xl_ref[0], xc_ref[0])


def _mix_ffn_kernel(xl_ref, xc_ref, ya_ref, yb_ref, yc_ref, mg_ref, mod_ref,
                    wa_ref, wb_ref, wc_ref, wo_ref, gain_ref, wi_ref, wd_ref, fin_ref, o_ref,
                    *, n_lat, final):
    d = D_MODEL
    u = mg_ref[0, :, 0:d].astype(F32) * _dot(ya_ref[0], wa_ref[...])
    u += mg_ref[0, :, d:2 * d].astype(F32) * _dot(yb_ref[0], wb_ref[...])
    u += mg_ref[0, :, 2 * d:3 * d].astype(F32) * _dot(yc_ref[0], wc_ref[...])
    x = _pick_tile(xl_ref, xc_ref, n_lat) + mod_ref[0, 0, 2:3, :] * _dot(u.astype(BF16), wo_ref[...])

    y = _rms(x) * gain_ref[...]
    h = (y * (1.0 + mod_ref[0, 0, 4:5, :]) + mod_ref[0, 0, 3:4, :]).astype(BF16)
    gate = _dot(h, wi_ref[:, 0:D_FF])
    up = _dot(h, wi_ref[:, D_FF:2 * D_FF])
    a = (gate * jax.nn.sigmoid(gate) * up).astype(BF16)
    x = x + mod_ref[0, 0, 5:6, :] * _dot(a, wd_ref[...])
    if final:
        x = _rms(x) * fin_ref[...]
    o_ref[0] = x


def _mix_ffn(xs, ya, yb, yc, mg, modtab, wa, wb, wc, wo, gain, wi, wd, fin, n_lat, n_tiles, final):
    (xl, xc), x_specs = _residual_inputs(xs, n_lat)
    b, _, d = xl.shape
    tm = TOKEN_TILE
    row = lambda bi, i: (bi, i, 0)
    weights = (wa, wb, wc, wo, gain, wi, wd, fin)
    return pl.pallas_call(
        functools.partial(_mix_ffn_kernel, n_lat=n_lat, final=final),
        grid=(b, n_tiles),
        in_specs=x_specs + [
            pl.BlockSpec((1, tm, d), row), pl.BlockSpec((1, tm, d), row),
            pl.BlockSpec((1, tm, d), row), pl.BlockSpec((1, tm, 3 * d), row),
            pl.BlockSpec((1, 1, 8, d), lambda bi, i: (bi, i // n_lat, 0, 0)),
        ] + [_const_spec(w.shape) for w in weights],
        out_specs=pl.BlockSpec((1, tm, d), row),
        out_shape=jax.ShapeDtypeStruct((b, n_tiles * tm, d), F32),
        compiler_params=_cparams(2),
        name="mix_ffn",
    )(xl, xc, ya, yb, yc, mg, modtab, *weights)


def _rope_tables(t_lat, s_tot):
    half = SWA_HEAD_DIM // 4
    inv = np.power(ROPE_BASE, -np.arange(half, dtype=np.float32) / half).astype(np.float32)
    pos = np.arange(t_lat)
    lane = np.arange(SWA_HEAD_DIM)
    p = np.where(lane[None, :] < 2 * half, (pos // GRID_W)[:, None], (pos % GRID_W)[:, None])
    ang = p.astype(np.float32) * inv[lane % half][None, :]
    cos, sin = np.cos(ang), np.sin(ang)
    upper = (lane % (2 * half)) >= half
    tabs = np.stack([cos, np.where(upper, sin, 0.0), np.where(upper, 0.0, -sin)]).astype(np.float32)
    ident = np.zeros((3, s_tot - t_lat, SWA_HEAD_DIM), np.float32)
    ident[0] = 1.0
    return np.concatenate([tabs, ident], axis=1)


def _prep_w_in(w_in):
    parts, idx = [], 0
    for n in IN_SPLITS:
        parts.append(w_in[:, idx:idx + n])
        idx += n
    qa, ka, va, ga, gkf, gkb, qs, ks, vs, cq, ckv, kr, mg = parts
    pad = jnp.zeros((w_in.shape[0], 32), w_in.dtype)
    rows = jnp.concatenate([qa, ka, va, ga, qs, ks, cq, ckv, kr, gkf, gkb, pad, mg], axis=1)
    return rows.astype(BF16), vs.T.astype(BF16)


def kernel(x, c, ctx, c_ctx, w_mod, b_mod, norm_mix, w_in, w_gk_fwd, b_gk_fwd, w_gk_bwd, b_gk_bwd,
           gla_norm, sinks, q_norm, w_q_up, kv_norm, w_kv_up, w_pa, w_pb, w_pc, w_o,
           norm_ffn, w_ffn_in, w_ffn_out, final_norm):
    b, t_lat, d = x.shape
    l_ctx = ctx.shape[1]
    s_tot = t_lat + l_ctx
    depth = w_mod.shape[0]
    tm = TOKEN_TILE
    assert t_lat % tm == 0 and l_ctx == tm and t_lat % GRID_W == 0 and d == D_MODEL
    n_lat = t_lat // tm
    n_all = s_tot // tm

    cc = jnp.concatenate([c, c_ctx[None, :], jnp.zeros((16 - b - 1, d), F32)], axis=0)
    mod = _modulation(cc, w_mod, b_mod)
    mod_lat = mod[:, :b].reshape(depth, b, 1, 6, d)
    mod_ctx = jnp.broadcast_to(mod[:, b].reshape(depth, 1, 1, 6, d), (depth, b, 1, 6, d))
    modtab = jnp.pad(jnp.concatenate([mod_lat, mod_ctx], axis=2),
                     ((0, 0), (0, 0), (0, 0), (0, 2), (0, 0)))

    t64 = _rope_tables(t_lat, s_tot)
    ident = np.zeros_like(t64)
    ident[0] = 1.0
    rk = np.concatenate([t64, t64], axis=2)
    rq = rk * np.float32(SWA_HEAD_DIM ** -0.5 * LOG2_E)
    rm = np.concatenate([t64, ident], axis=2)
    rmq = rm * np.float32((MLA_NOPE + MLA_ROPE) ** -0.5 * LOG2_E)
    rq, rk, rm, rmq = (jnp.asarray(a) for a in (rq, rk, rm, rmq))

    xs = (x, ctx)
    for l in range(depth):
        last = l == depth - 1
        n_out = n_lat if last else n_all
        w_in_l, w_vs_t = _prep_w_in(w_in[l])
        wf = jnp.zeros((LANES, GLA_HEADS * GLA_DK), F32).at[
            MISC_GK_LANE:MISC_GK_LANE + GLA_GATE_RANK].set(w_gk_fwd[l])
        wb = jnp.zeros((LANES, GLA_HEADS * GLA_DK), F32).at[
            MISC_GK_LANE + GLA_GATE_RANK:MISC_GK_LANE + 2 * GLA_GATE_RANK].set(w_gk_bwd[l])
        to_heads = lambda w: w.reshape(LANES, GLA_HEADS, GLA_DK).transpose(1, 0, 2).astype(BF16)
        wq = jnp.pad(w_q_up[l].reshape(MLA_Q_RANK, MLA_HEADS, MLA_NOPE + MLA_ROPE),
                     ((0, 0), (0, 0), (0, MLA_QK_PAD - MLA_NOPE - MLA_ROPE))
                     ).reshape(MLA_Q_RANK, MLA_HEADS * MLA_QK_PAD).astype(BF16)

        qk, va, ga, qs, kx, vx, misc, mg = _inproj(
            xs, modtab[l], norm_mix[l].reshape(1, d), w_in_l, w_vs_t, rq, rk, rm, n_lat)
        ya = _gla(qk, va, ga, misc, to_heads(wf), to_heads(wb),
                  b_gk_fwd[l].reshape(GLA_HEADS, 1, GLA_DK), b_gk_bwd[l].reshape(GLA_HEADS, 1, GLA_DK),
                  gla_norm[l].reshape(1, GLA_DV), t_lat)
        yb = _swa(sinks[l], qs, kx, vx, t_lat, n_out)
        wkv = w_kv_up[l].reshape(MLA_KV_RANK, MLA_HEADS, MLA_NOPE + MLA_V)
        wk = wkv[:, :, :MLA_NOPE].reshape(MLA_KV_RANK, MLA_HEADS * MLA_NOPE).astype(BF16)
        wvt = wkv[:, :, MLA_NOPE:].reshape(MLA_KV_RANK, MLA_HEADS * MLA_V).T.astype(BF16)
        qm, km, vm = _mla_prep(misc, wq, wk, wvt, q_norm[l].reshape(1, -1),
                               kv_norm[l].reshape(1, -1), rmq)
        yc = _mla_attn(qm, km, vm, t_lat, n_out)
        xs = _mix_ffn(xs, ya, yb, yc, mg, modtab[l], w_pa[l].astype(BF16), w_pb[l].astype(BF16),
                      w_pc[l].astype(BF16), w_o[l].astype(BF16), norm_ffn[l].reshape(1, d),
                      w_ffn_in[l].astype(BF16), w_ffn_out[l].astype(BF16), final_norm.reshape(1, d),
                      n_lat, n_out, last)
    return xs
```

```python
import functools

import jax
import jax.numpy as jnp
import numpy as np
from jax import lax
from jax.experimental import pallas as pl
from jax.experimental.pallas import tpu as pltpu

F32 = jnp.float32
BF16 = jnp.bfloat16

D_MODEL = 1024
GRID_W = 64
EPS = 1e-6
ROPE_BASE = 10000.0

GLA_HEADS = 4
GLA_DK = 128
GLA_DV = 256
GLA_GATE_RANK = 16
GLA_GATE_NORM = 16.0
GLA_CHUNK = 64

SWA_HEADS = 16
SWA_KV_HEADS = 2
SWA_GROUP = SWA_HEADS // SWA_KV_HEADS
SWA_HEAD_DIM = 64
WINDOW = 128
SWA_VT_ROWS = SWA_HEAD_DIM + 16

MLA_HEADS = 8
MLA_Q_RANK = 384
MLA_KV_RANK = 256
MLA_NOPE = 128
MLA_ROPE = 64
MLA_V = 128
MLA_QK_PAD = 256
MLA_VT_ROWS = MLA_V + 16
LOG2_E = 1.4426950408889634

D_FF = -(-(8 * D_MODEL) // (3 * 256)) * 256

IN_SPLITS = (
    GLA_HEADS * GLA_DK, GLA_HEADS * GLA_DK, GLA_HEADS * GLA_DV, GLA_HEADS * GLA_DV,
    GLA_GATE_RANK, GLA_GATE_RANK,
    SWA_HEADS * SWA_HEAD_DIM, SWA_KV_HEADS * SWA_HEAD_DIM, SWA_KV_HEADS * SWA_HEAD_DIM,
    MLA_Q_RANK, MLA_KV_RANK, MLA_ROPE,
    3 * D_MODEL,
)

LANES = 128
TOKEN_TILE = 256
MISC_W = 768
MISC_GK_LANE = 64
VMEM_LIMIT = 56 * 1024 * 1024


def _cparams(n_axes):
    return pltpu.CompilerParams(
        dimension_semantics=("arbitrary",) * n_axes, vmem_limit_bytes=VMEM_LIMIT)


def _const_spec(shape):
    nd = len(shape)
    return pl.BlockSpec(shape, lambda *_: (0,) * nd, pipeline_mode=pl.Buffered(1))


def _rope(x, tab_ref):
    return (x * tab_ref[0] + pltpu.roll(x, 16, 1) * tab_ref[1]
            + pltpu.roll(x, LANES - 16, 1) * tab_ref[2])


def _rms(x):
    return x * lax.rsqrt(jnp.mean(x * x, axis=-1, keepdims=True) + EPS)


def _dot(a, b):
    return jnp.dot(a, b, preferred_element_type=F32)


def _dot_nt(a, b):
    return lax.dot_general(a, b, (((1,), (1,)), ((), ())), preferred_element_type=F32)


def _dot_tn(a, b):
    return lax.dot_general(a, b, (((0,), (0,)), ((), ())), preferred_element_type=F32)


def _mod_kernel(c_ref, w_ref, b_ref, o_ref):
    c = c_ref[...]
    a = (c * jax.nn.sigmoid(c)).astype(BF16)
    o_ref[0] = _dot(a, w_ref[0].astype(BF16)) + b_ref[0]


def _modulation(cc, w_mod, b_mod):
    depth, d, n = w_mod.shape
    tn = 1536
    return pl.pallas_call(
        _mod_kernel,
        grid=(depth, n // tn),
        in_specs=[
            pl.BlockSpec(cc.shape, lambda l, j: (0, 0)),
            pl.BlockSpec((1, d, tn), lambda l, j: (l, 0, j)),
            pl.BlockSpec((1, 1, tn), lambda l, j: (l, 0, j)),
        ],
        out_specs=pl.BlockSpec((1, cc.shape[0], tn), lambda l, j: (l, 0, j)),
        out_shape=jax.ShapeDtypeStruct((depth, cc.shape[0], n), F32),
        compiler_params=_cparams(2),
        name="modulation",
    )(cc, w_mod, b_mod.reshape(depth, 1, n))


def _mla_project(cq, ckv, tail, wq_ref, wk_ref, wvt_ref, qn_ref, kvn_ref, rq_ref,
                 q_ref, k_ref, kr_ref, vt_ref):
    scale = (MLA_NOPE + MLA_ROPE) ** -0.5 * LOG2_E
    lo = lax.broadcasted_iota(jnp.int32, tail.shape, 1) < MLA_ROPE
    kr_ref[0] = jnp.where(lo, tail, 0.0).astype(BF16)

    qf = _dot((_rms(cq) * qn_ref[...]).astype(BF16), wq_ref[...])
    ckvn = (_rms(ckv) * kvn_ref[...]).astype(BF16)
    k_ref[0] = _dot(ckvn, wk_ref[...]).astype(BF16)
    vt = _dot_nt(wvt_ref[...], ckvn)
    ones = jnp.ones((MLA_VT_ROWS - MLA_V, vt.shape[1]), BF16)
    for h in range(MLA_HEADS):
        c0 = h * MLA_QK_PAD
        q_ref[0, :, c0:c0 + LANES] = (qf[:, c0:c0 + LANES] * scale).astype(BF16)
        q_ref[0, :, c0 + LANES:c0 + 2 * LANES] = _rope(
            qf[:, c0 + LANES:c0 + 2 * LANES], rq_ref).astype(BF16)
        vt_ref[0, h, 0:MLA_V, :] = vt[h * MLA_V:(h + 1) * MLA_V, :].astype(BF16)
        vt_ref[0, h, MLA_V:MLA_VT_ROWS, :] = ones


def _inproj_kernel(xl_ref, xc_ref, mod_ref, gain_ref, w_ref, wvt_ref, rq_ref, rk_ref, rm_ref,
                   wq_ref, wk_ref, wmvt_ref, qn_ref, kvn_ref, rmq_ref,
                   qk_ref, va_ref, ga_ref, qs_ref, kx_ref, vx_ref, gk_ref, mg_ref,
                   mq_ref, mk_ref, mkr_ref, mvt_ref, *, n_lat):
    x = _pick_tile(xl_ref, xc_ref, n_lat)
    y = _rms(x) * gain_ref[...]
    h = (y * (1.0 + mod_ref[0, 0, 1:2, :]) + mod_ref[0, 0, 0:1, :]).astype(BF16)

    def proj(c0, c1):
        return _dot_nt(h, w_ref[c0:c1, :])

    qk_ref[0] = proj(0, 1024).astype(BF16)
    va_ref[0] = proj(1024, 2048).astype(BF16)
    r = proj(2048, 3072)
    ga_ref[0] = (r * jax.nn.sigmoid(r)).astype(BF16)

    r = proj(3072, 4096)
    for j in range(1024 // LANES):
        sl = slice(j * LANES, (j + 1) * LANES)
        qs_ref[0, :, sl] = _rope(r[:, sl], rq_ref).astype(BF16)

    kk = _rope(proj(4096, 4224), rk_ref)
    kx = pltpu.roll(kk, 64, 1)
    lo = lax.broadcasted_iota(jnp.int32, kk.shape, 1) < 64
    zero = jnp.zeros_like(kk)
    blocks = (jnp.where(lo, kk, zero), jnp.where(lo, zero, kx),
              jnp.where(lo, kx, zero), jnp.where(lo, zero, kk))
    for j, blk in enumerate(blocks):
        kx_ref[0, :, j * LANES:(j + 1) * LANES] = blk.astype(BF16)

    vt = _dot_nt(wvt_ref[...], h).astype(BF16)
    hd = SWA_HEAD_DIM
    ones = jnp.ones((SWA_VT_ROWS - hd, vt.shape[1]), BF16)
    for g in range(SWA_KV_HEADS):
        vx_ref[0, g * SWA_VT_ROWS:g * SWA_VT_ROWS + hd, :] = vt[g * hd:(g + 1) * hd, :]
        vx_ref[0, g * SWA_VT_ROWS + hd:(g + 1) * SWA_VT_ROWS, :] = ones

    r = proj(4224, 4224 + MISC_W)
    tail = _rope(r[:, MISC_W - LANES:MISC_W], rm_ref)
    gk_ref[0] = tail
    _mla_project(r[:, 0:MLA_Q_RANK], r[:, MLA_Q_RANK:MLA_Q_RANK + MLA_KV_RANK], tail,
                 wq_ref, wk_ref, wmvt_ref, qn_ref, kvn_ref, rmq_ref, mq_ref, mk_ref, mkr_ref, mvt_ref)

    c0 = 4224 + MISC_W
    for j in range(3):
        r = proj(c0 + j * 1024, c0 + (j + 1) * 1024)
        mg_ref[0, :, j * 1024:(j + 1) * 1024] = jax.nn.sigmoid(r).astype(BF16)


def _inproj(xs, modtab, gain, w, wvt, rq, rk, rm, mla_weights, rmq, n_lat):
    (xl, xc), x_specs = _residual_inputs(xs, n_lat)
    b, _, d = xl.shape
    tm = TOKEN_TILE
    nt = n_lat + 1
    s = nt * tm
    row = lambda bi, i: (bi, i, 0)
    tab = pl.BlockSpec((3, tm, LANES), lambda bi, i: (0, i, 0))
    vx_rows = SWA_KV_HEADS * SWA_VT_ROWS
    rows_out = lambda n, dt: (pl.BlockSpec((1, tm, n), row), jax.ShapeDtypeStruct((b, s, n), dt))
    outs = [rows_out(1024, BF16), rows_out(1024, BF16), rows_out(1024, BF16), rows_out(1024, BF16),
            rows_out(SWA_KV_HEADS * 2 * LANES, BF16),
            (pl.BlockSpec((1, vx_rows, tm), lambda bi, i: (bi, 0, i)),
             jax.ShapeDtypeStruct((b, vx_rows, s), BF16)),
            rows_out(LANES, F32), rows_out(3072, BF16),
            rows_out(MLA_HEADS * MLA_QK_PAD, BF16), rows_out(MLA_HEADS * MLA_NOPE, BF16),
            rows_out(LANES, BF16),
            (pl.BlockSpec((1, MLA_HEADS, MLA_VT_ROWS, tm), lambda bi, i: (bi, 0, 0, i)),
             jax.ShapeDtypeStruct((b, MLA_HEADS, MLA_VT_ROWS, s), BF16))]
    return pl.pallas_call(
        functools.partial(_inproj_kernel, n_lat=n_lat),
        grid=(b, nt),
        in_specs=x_specs + [
            pl.BlockSpec((1, 1, 8, d), lambda bi, i: (bi, i // n_lat, 0, 0)),
            _const_spec((1, d)),
            _const_spec(w.shape), _const_spec(wvt.shape),
            tab, tab, tab,
        ] + [_const_spec(a.shape) for a in mla_weights] + [tab],
        out_specs=[o[0] for o in outs],
        out_shape=[o[1] for o in outs],
        compiler_params=_cparams(2),
        name="inproj",
    )(xl, xc, modtab, gain, w, wvt, rq, rk, rm, *mla_weights, rmq)


def _gla_kernel(q_ref, k_ref, v_ref, g_ref, gk_ref, wf_ref, wb_ref, bf_ref, bb_ref, gn_ref,
                sum_ref, ones_ref, o_ref,
                of_ref, ob_ref, qdf_ref, qdb_ref, af_ref, ab_ref, stf_ref, stb_ref,
                decf_ref, decb_ref, keepf_ref, keepb_ref, pf0_ref, pb0_ref, pf1_ref, pb1_ref,
                *, n_lat):
    c = GLA_CHUNK
    tm = TOKEN_TILE
    per = tm // c
    q_scale = GLA_DK ** -0.5

    ri = lax.broadcasted_iota(jnp.int32, (tm, tm), 0)
    ci = lax.broadcasted_iota(jnp.int32, (tm, tm), 1)
    same = (ri // c) == (ci // c)
    keepf_ref[...] = jnp.where(same & (ci <= ri), 1.0, 0.0)
    keepb_ref[...] = jnp.where(same & (ci > ri), 1.0, 0.0)

    def tile_rows(t):
        return pl.ds(pl.multiple_of(t * tm, tm), tm)

    def decays(t, d, w_ref, b_ref, qd_ref, dec_ref, stage):
        rows = tile_rows(t)
        z = _dot(gk_ref[0, rows, :].astype(BF16), w_ref[0]) + b_ref[0]
        la = (jnp.minimum(z, 0.0) - jnp.log1p(jnp.exp(-jnp.abs(z)))) * (1.0 / GLA_GATE_NORM)
        hi = la.astype(BF16)
        lo = (la - hi.astype(F32)).astype(BF16)
        cat = jnp.concatenate([hi, lo], axis=1)
        cr = _dot(sum_ref[d], cat)
        cum = cr[:, 0:GLA_DK] + cr[:, GLA_DK:]
        tt = _dot_tn(cat, ones_ref[...])
        dec_ref[t] = jnp.exp(tt[0:GLA_DK] + tt[GLA_DK:])
        q = q_ref[0, rows, :].astype(F32) * q_scale
        k = k_ref[0, rows, :].astype(F32)
        q_dec = (q * jnp.exp(cum)).astype(BF16)
        qd_ref[rows, :] = q_dec
        stage[0] = q_dec
        stage[1] = (k * jnp.exp(-cum)).astype(BF16)

    def in_chunk(t, keep_ref, a_ref, out_ref, stage):
        v = v_ref[0, tile_rows(t), :]
        k_inv = stage[1]
        for j in range(per):
            a_ref[t * per + j] = _dot_tn(k_inv[j * c:(j + 1) * c], v[j * c:(j + 1) * c])
        scores = jnp.where(keep_ref[...] > 0.5, _dot_nt(stage[0], stage[1]), 0.0).astype(BF16)
        out_ref[tile_rows(t), :] = _dot(scores, v)

    def stage1(t, bufs):
        decays(t, 0, wf_ref, bf_ref, qdf_ref, decf_ref, bufs[0])
        decays(t, 1, wb_ref, bb_ref, qdb_ref, decb_ref, bufs[1])

    def stage2(t, bufs):
        in_chunk(t, keepf_ref, af_ref, of_ref, bufs[0])
        in_chunk(t, keepb_ref, ab_ref, ob_ref, bufs[1])

    even, odd = (pf0_ref, pb0_ref), (pf1_ref, pb1_ref)
    n_tiles = n_lat + 1
    stage1(0, even)

    def bulk_body(j, carry):
        stage1(2 * j + 1, odd)
        stage2(2 * j, even)
        stage1(2 * j + 2, even)
        stage2(2 * j + 1, odd)
        return carry

    lax.fori_loop(0, (n_tiles - 1) // 2, bulk_body, 0)
    stage2(n_tiles - 1, even)

    def scan_tile(t, order, state, a_ref, dec_ref, st_ref):
        for j in order:
            st_ref[t * per + j] = state.astype(BF16)
            dec = dec_ref[t, :, j * LANES:(j + 1) * LANES]
            state = jnp.concatenate([dec] * (GLA_DV // LANES), axis=1) * (state + a_ref[t * per + j])
        return state

    fwd_order = range(per)
    bwd_order = range(per - 1, -1, -1)
    zero = jnp.zeros((GLA_DK, GLA_DV), F32)
    s_f = scan_tile(n_lat, fwd_order, zero, af_ref, decf_ref, stf_ref)
    s_b = scan_tile(n_lat, bwd_order, zero, ab_ref, decb_ref, stb_ref)
    lax.fori_loop(0, n_lat, lambda i, s: scan_tile(
        i, fwd_order, s, af_ref, decf_ref, stf_ref), s_f)
    lax.fori_loop(0, n_lat, lambda i, s: scan_tile(
        n_lat - 1 - i, bwd_order, s, ab_ref, decb_ref, stb_ref), s_b)

    def post_body(t, carry):
        for j in range(per):
            rows = pl.ds(pl.multiple_of(t * tm + j * c, c), c)
            o = of_ref[rows, :] + ob_ref[rows, :]
            o += _dot(qdf_ref[rows, :], stf_ref[t * per + j])
            o += _dot(qdb_ref[rows, :], stb_ref[t * per + j])
            y = _rms(o) * gn_ref[...]
            o_ref[0, rows, :] = (y * g_ref[0, rows, :].astype(F32)).astype(BF16)
        return carry

    lax.fori_loop(0, n_lat + 1, post_body, 0, unroll=3)


def _gla_constants():
    tm, c = TOKEN_TILE, GLA_CHUNK
    r = np.arange(tm)[:, None]
    col = np.arange(tm)[None, :]
    same = (r // c) == (col // c)
    fwd = same & (col <= r)
    bwd = same & (col >= r)
    member = (np.arange(tm)[:, None] // c) == (np.arange(tm // c * LANES)[None, :] // LANES)
    return (jnp.asarray(np.stack([fwd, bwd]).astype(np.float32), BF16),
            jnp.asarray(member.astype(np.float32), BF16))


def _gla(qk, va, ga, gk, wf, wb, bf, bb, gn, t_lat):
    b, s, _ = qk.shape
    tm = TOKEN_TILE
    n_lat = t_lat // tm
    n_tiles = s // tm
    n_chunks = s // GLA_CHUNK
    h = GLA_HEADS
    sums, member = _gla_constants()
    return pl.pallas_call(
        functools.partial(_gla_kernel, n_lat=n_lat),
        grid=(b, h),
        in_specs=[
            pl.BlockSpec((1, s, GLA_DK), lambda bi, hi: (bi, 0, hi)),
            pl.BlockSpec((1, s, GLA_DK), lambda bi, hi: (bi, 0, h + hi)),
            pl.BlockSpec((1, s, GLA_DV), lambda bi, hi: (bi, 0, hi)),
            pl.BlockSpec((1, s, GLA_DV), lambda bi, hi: (bi, 0, hi)),
            pl.BlockSpec((1, s, LANES), lambda bi, hi: (bi, 0, 0)),
            pl.BlockSpec((1, LANES, GLA_DK), lambda bi, hi: (hi, 0, 0)),
            pl.BlockSpec((1, LANES, GLA_DK), lambda bi, hi: (hi, 0, 0)),
            pl.BlockSpec((1, 1, GLA_DK), lambda bi, hi: (hi, 0, 0)),
            pl.BlockSpec((1, 1, GLA_DK), lambda bi, hi: (hi, 0, 0)),
            pl.BlockSpec((1, GLA_DV), lambda bi, hi: (0, 0)),
            _const_spec(sums.shape), _const_spec(member.shape),
        ],
        out_specs=pl.BlockSpec((1, s, GLA_DV), lambda bi, hi: (bi, 0, hi)),
        out_shape=jax.ShapeDtypeStruct((b, s, h * GLA_DV), BF16),
        scratch_shapes=[
            pltpu.VMEM((s, GLA_DV), F32), pltpu.VMEM((s, GLA_DV), F32),
            pltpu.VMEM((s, GLA_DK), BF16), pltpu.VMEM((s, GLA_DK), BF16),
            pltpu.VMEM((n_chunks, GLA_DK, GLA_DV), F32), pltpu.VMEM((n_chunks, GLA_DK, GLA_DV), F32),
            pltpu.VMEM((n_chunks, GLA_DK, GLA_DV), BF16), pltpu.VMEM((n_chunks, GLA_DK, GLA_DV), BF16),
            pltpu.VMEM((n_tiles, GLA_DK, tm // GLA_CHUNK * LANES), F32),
            pltpu.VMEM((n_tiles, GLA_DK, tm // GLA_CHUNK * LANES), F32),
            pltpu.VMEM((tm, tm), F32), pltpu.VMEM((tm, tm), F32),
        ] + [pltpu.VMEM((2, tm, GLA_DK), BF16)] * 4,
        compiler_params=_cparams(2),
        name="gla",
    )(qk, qk, va, ga, gk, wf, wb, bf, bb, gn, sums, member)


def _swa_kernel(sink_ref, q_ref, kx_ref, vx_ref, bias_ref, o_ref, s0_ref, s1_ref, m0_ref, m1_ref,
                *, n_lat, t_lat, with_ctx):
    tq = TOKEN_TILE
    span = tq + 2 * WINDOW
    hd = SWA_HEAD_DIM
    g = pl.program_id(1)
    ka, kb = slice(0, LANES), slice(LANES, 2 * LANES)

    def q_rows(i):
        return pl.ds(pl.multiple_of(i * tq, tq), tq)

    def sink_of(head):
        return sink_ref[g * SWA_GROUP + head] * LOG2_E

    def keys(i):
        ws = pl.multiple_of(jnp.clip(i * tq - WINDOW, 0, t_lat - span), WINDOW)
        win = pl.ds(ws, span)
        kc = [jnp.concatenate([kx_ref[0, win, c], kx_ref[0, t_lat:, c]], axis=0) for c in (ka, kb)]
        vc = jnp.concatenate([vx_ref[0, :, win], vx_ref[0, :, t_lat:]], axis=1)
        return kc, vc, (i * tq - ws) // WINDOW

    def scores(i, kc, bias_idx, head, s_ref, m_ref):
        qp = q_ref[0, q_rows(i), (head // 2) * LANES:(head // 2 + 1) * LANES]
        s_t = _dot_nt(kc[head % 2], qp)
        top = s_t[0:span] + bias_ref[bias_idx]
        bot = s_t[span:]
        s_ref[0:span, :] = top
        s_ref[span:, :] = bot
        m = jnp.maximum(jnp.max(top, axis=0, keepdims=True), jnp.max(bot, axis=0, keepdims=True))
        m_ref[...] = jnp.broadcast_to(jnp.maximum(m, sink_of(head)), m_ref.shape)

    def normalise(acc, m, head):
        return acc[0:hd] / (acc[hd:hd + 1] + jnp.exp2(sink_of(head) - m))

    def values(vc, head, s_ref, m_ref):
        m = m_ref[0:1, :]
        p_t = jnp.exp2(s_ref[...] - m).astype(BF16)
        return normalise(_dot(vc, p_t), m, head)

    def store(i, pair, o_a, o_b):
        o_t = jnp.concatenate([o_a, o_b], axis=0)
        o_ref[0, q_rows(i), pair * LANES:(pair + 1) * LANES] = o_t.T.astype(BF16)

    def scores_all(i, s_ref, m_ref):
        kc, _, bias_idx = keys(i)
        for head in range(SWA_GROUP):
            scores(i, kc, bias_idx, head, s_ref.at[head], m_ref.at[head])

    def values_all(i, s_ref, m_ref):
        _, vc, _ = keys(i)
        for pair in range(SWA_GROUP // 2):
            a, b = 2 * pair, 2 * pair + 1
            store(i, pair, values(vc, a, s_ref.at[a], m_ref.at[a]),
                  values(vc, b, s_ref.at[b], m_ref.at[b]))

    scores_all(0, s0_ref, m0_ref)

    def body(j, carry):
        scores_all(2 * j + 1, s1_ref, m1_ref)
        values_all(2 * j, s0_ref, m0_ref)
        scores_all(2 * j + 2, s0_ref, m0_ref)
        values_all(2 * j + 1, s1_ref, m1_ref)
        return carry

    lax.fori_loop(0, n_lat // 2 - 1, body, 0)
    scores_all(n_lat - 1, s1_ref, m1_ref)
    values_all(n_lat - 2, s0_ref, m0_ref)
    values_all(n_lat - 1, s1_ref, m1_ref)

    if with_ctx:
        rows = slice(t_lat, t_lat + tq)
        for pair in range(SWA_GROUP // 2):
            qp = q_ref[0, rows, pair * LANES:(pair + 1) * LANES]
            outs = []
            for head, kcols in ((2 * pair, ka), (2 * pair + 1, kb)):
                s_t = _dot_nt(kx_ref[0, t_lat:, kcols], qp)
                m = jnp.maximum(jnp.max(s_t, axis=0, keepdims=True), sink_of(head))
                p_t = jnp.exp2(s_t - m).astype(BF16)
                outs.append(normalise(_dot(vx_ref[0, :, t_lat:], p_t), m, head))
            o_t = jnp.concatenate(outs, axis=0)
            o_ref[0, rows, pair * LANES:(pair + 1) * LANES] = o_t.T.astype(BF16)


def _swa_bias():
    tq, span = TOKEN_TILE, TOKEN_TILE + 2 * WINDOW
    r = np.arange(span)[:, None]
    c = np.arange(tq)[None, :]
    tabs = [np.where(np.abs(r - off - c) <= WINDOW, 0.0, -1e30) for off in (0, WINDOW, 2 * WINDOW)]
    return jnp.asarray(np.stack(tabs).astype(np.float32))


def _swa(sinks, qs, kx, vx, t_lat, n_tiles):
    b, s, _ = qs.shape
    tq = TOKEN_TILE
    gw = SWA_GROUP * SWA_HEAD_DIM
    n_keys = tq + 2 * WINDOW + (s - t_lat)
    bias = _swa_bias()
    n_lat = t_lat // tq
    return pl.pallas_call(
        functools.partial(_swa_kernel, n_lat=n_lat, t_lat=t_lat, with_ctx=n_tiles > n_lat),
        grid=(b, SWA_KV_HEADS),
        in_specs=[
            pl.BlockSpec(memory_space=pltpu.SMEM),
            pl.BlockSpec((1, s, gw), lambda bi, gi: (bi, 0, gi)),
            pl.BlockSpec((1, s, 2 * LANES), lambda bi, gi: (bi, 0, gi)),
            pl.BlockSpec((1, SWA_VT_ROWS, s), lambda bi, gi: (bi, gi, 0)),
            _const_spec(bias.shape),
        ],
        out_specs=pl.BlockSpec((1, n_tiles * tq, gw), lambda bi, gi: (bi, 0, gi)),
        out_shape=jax.ShapeDtypeStruct((b, n_tiles * tq, SWA_HEADS * SWA_HEAD_DIM), BF16),
        scratch_shapes=[pltpu.VMEM((SWA_GROUP, n_keys, tq), F32), pltpu.VMEM((SWA_GROUP, n_keys, tq), F32),
                        pltpu.VMEM((SWA_GROUP, 8, tq), F32), pltpu.VMEM((SWA_GROUP, 8, tq), F32)],
        compiler_params=_cparams(2),
        name="swa",
    )(sinks, qs, kx, vx, bias)


def _mla_attn_kernel(q_ref, kn_ref, kr_ref, vt_ref, o_ref, k_ref, s0_ref, s1_ref, m0_ref, m1_ref,
                     *, n_lat, t_lat, with_ctx):
    tq = TOKEN_TILE
    k_ref[:, 0:MLA_NOPE] = kn_ref[0]
    k_ref[:, MLA_NOPE:] = kr_ref[0]

    def q_rows(i):
        return pl.ds(pl.multiple_of(i * tq, tq), tq)

    def finish(acc, rows):
        o_t = acc[0:MLA_V] / acc[MLA_V:MLA_V + 1]
        o_ref[0, rows, :] = o_t.T.astype(BF16)

    def scores(i, s_ref, m_ref):
        q = q_ref[0, q_rows(i), :]
        half = k_ref.shape[0] // 2
        lo = _dot_nt(k_ref[0:half, :], q)
        hi = _dot_nt(k_ref[half:, :], q)
        s_ref[0:half, :] = lo
        s_ref[half:, :] = hi
        m = jnp.maximum(jnp.max(lo, axis=0, keepdims=True), jnp.max(hi, axis=0, keepdims=True))
        m_ref[...] = jnp.broadcast_to(m, m_ref.shape)

    def values(i, s_ref, m_ref):
        p_t = jnp.exp2(s_ref[...] - m_ref[0:1, :]).astype(BF16)
        finish(_dot(vt_ref[0, 0], p_t), q_rows(i))

    scores(0, s0_ref, m0_ref)

    def body(j, carry):
        scores(2 * j + 1, s1_ref, m1_ref)
        values(2 * j, s0_ref, m0_ref)
        scores(2 * j + 2, s0_ref, m0_ref)
        values(2 * j + 1, s1_ref, m1_ref)
        return carry

    lax.fori_loop(0, n_lat // 2 - 1, body, 0)
    scores(n_lat - 1, s1_ref, m1_ref)
    values(n_lat - 2, s0_ref, m0_ref)
    values(n_lat - 1, s1_ref, m1_ref)

    if with_ctx:
        s_t = _dot_nt(k_ref[t_lat:, :], q_ref[0, t_lat:, :])
        p_t = jnp.exp2(s_t - jnp.max(s_t, axis=0, keepdims=True)).astype(BF16)
        finish(_dot(vt_ref[0, 0, :, t_lat:], p_t), slice(t_lat, t_lat + tq))


def _mla_attn(q, kn, kr, vt, t_lat, n_tiles):
    b, s, _ = q.shape
    tq = TOKEN_TILE
    n_lat = t_lat // tq
    return pl.pallas_call(
        functools.partial(_mla_attn_kernel, n_lat=n_lat, t_lat=t_lat, with_ctx=n_tiles > n_lat),
        grid=(b, MLA_HEADS),
        in_specs=[
            pl.BlockSpec((1, s, MLA_QK_PAD), lambda bi, hi: (bi, 0, hi)),
            pl.BlockSpec((1, s, MLA_NOPE), lambda bi, hi: (bi, 0, hi)),
            pl.BlockSpec((1, s, LANES), lambda bi, hi: (bi, 0, 0)),
            pl.BlockSpec((1, 1, MLA_VT_ROWS, s), lambda bi, hi: (bi, hi, 0, 0)),
        ],
        out_specs=pl.BlockSpec((1, n_tiles * tq, MLA_V), lambda bi, hi: (bi, 0, hi)),
        out_shape=jax.ShapeDtypeStruct((b, n_tiles * tq, MLA_HEADS * MLA_V), BF16),
        scratch_shapes=[pltpu.VMEM((s, MLA_QK_PAD), BF16),
                        pltpu.VMEM((s, tq), F32), pltpu.VMEM((s, tq), F32),
                        pltpu.VMEM((8, tq), F32), pltpu.VMEM((8, tq), F32)],
        compiler_params=_cparams(2),
        name="mla_attn",
    )(q, kn, kr, vt)


def _residual_inputs(xs, n_lat):
    tm = TOKEN_TILE
    if isinstance(xs, tuple):
        lat, ctx, ctx_block = xs[0], xs[1], 0
    else:
        lat, ctx, ctx_block = xs, xs, n_lat
    d = lat.shape[-1]
    specs = [pl.BlockSpec((1, tm, d), lambda bi, i: (bi, jnp.minimum(i, n_lat - 1), 0)),
             pl.BlockSpec((1, tm, d), lambda bi, i: (bi, ctx_block, 0))]
    return (lat, ctx), specs


def _pick_tile(xl_ref, xc_ref, n_lat):
    return jnp.where(pl.program_id(1) < n_lat, xl_ref[0], xc_ref[0])


def _mix_ffn_kernel(xl_ref, xc_ref, ya_ref, yb_ref, yc_ref, mg_ref, mod_ref,
                    wa_ref, wb_ref, wc_ref, wo_ref, gain_ref, wi_ref, wd_ref, fin_ref, o_ref,
                    *, n_lat, final):
    d = D_MODEL
    u = mg_ref[0, :, 0:d].astype(F32) * _dot(ya_ref[0], wa_ref[...])
    u += mg_ref[0, :, d:2 * d].astype(F32) * _dot(yb_ref[0], wb_ref[...])
    u += mg_ref[0, :, 2 * d:3 * d].astype(F32) * _dot(yc_ref[0], wc_ref[...])
    x = _pick_tile(xl_ref, xc_ref, n_lat) + mod_ref[0, 0, 2:3, :] * _dot(u.astype(BF16), wo_ref[...])

    y = _rms(x) * gain_ref[...]
    h = (y * (1.0 + mod_ref[0, 0, 4:5, :]) + mod_ref[0, 0, 3:4, :]).astype(BF16)
    gate = _dot(h, wi_ref[:, 0:D_FF])
    up = _dot(h, wi_ref[:, D_FF:2 * D_FF])
    a = (gate * jax.nn.sigmoid(gate) * up).astype(BF16)
    x = x + mod_ref[0, 0, 5:6, :] * _dot(a, wd_ref[...])
    if final:
        x = _rms(x) * fin_ref[...]
    o_ref[0] = x


def _mix_ffn(xs, ya, yb, yc, mg, modtab, wa, wb, wc, wo, gain, wi, wd, fin, n_lat, n_tiles, final):
    (xl, xc), x_specs = _residual_inputs(xs, n_lat)
    b, _, d = xl.shape
    tm = TOKEN_TILE
    row = lambda bi, i: (bi, i, 0)
    weights = (wa, wb, wc, wo, gain, wi, wd, fin)
    return pl.pallas_call(
        functools.partial(_mix_ffn_kernel, n_lat=n_lat, final=final),
        grid=(b, n_tiles),
        in_specs=x_specs + [
            pl.BlockSpec((1, tm, d), row), pl.BlockSpec((1, tm, d), row),
            pl.BlockSpec((1, tm, d), row), pl.BlockSpec((1, tm, 3 * d), row),
            pl.BlockSpec((1, 1, 8, d), lambda bi, i: (bi, i // n_lat, 0, 0)),
        ] + [_const_spec(w.shape) for w in weights],
        out_specs=pl.BlockSpec((1, tm, d), row),
        out_shape=jax.ShapeDtypeStruct((b, n_tiles * tm, d), F32),
        compiler_params=_cparams(2),
        name="mix_ffn",
    )(xl, xc, ya, yb, yc, mg, modtab, *weights)


def _rope_tables(t_lat, s_tot):
    half = SWA_HEAD_DIM // 4
    inv = np.power(ROPE_BASE, -np.arange(half, dtype=np.float32) / half).astype(np.float32)
    pos = np.arange(t_lat)
    lane = np.arange(SWA_HEAD_DIM)
    p = np.where(lane[None, :] < 2 * half, (pos // GRID_W)[:, None], (pos % GRID_W)[:, None])
    ang = p.astype(np.float32) * inv[lane % half][None, :]
    cos, sin = np.cos(ang), np.sin(ang)
    upper = (lane % (2 * half)) >= half
    tabs = np.stack([cos, np.where(upper, sin, 0.0), np.where(upper, 0.0, -sin)]).astype(np.float32)
    ident = np.zeros((3, s_tot - t_lat, SWA_HEAD_DIM), np.float32)
    ident[0] = 1.0
    return np.concatenate([tabs, ident], axis=1)


def _prep_w_in(w_in):
    w_t = w_in.T.astype(BF16)
    parts, idx = [], 0
    for n in IN_SPLITS:
        parts.append(w_t[idx:idx + n])
        idx += n
    qa, ka, va, ga, gkf, gkb, qs, ks, vs, cq, ckv, kr, mg = parts
    pad = jnp.zeros((32, w_in.shape[0]), BF16)
    return jnp.concatenate([qa, ka, va, ga, qs, ks, cq, ckv, kr, gkf, gkb, pad, mg], axis=0), vs


def kernel(x, c, ctx, c_ctx, w_mod, b_mod, norm_mix, w_in, w_gk_fwd, b_gk_fwd, w_gk_bwd, b_gk_bwd,
           gla_norm, sinks, q_norm, w_q_up, kv_norm, w_kv_up, w_pa, w_pb, w_pc, w_o,
           norm_ffn, w_ffn_in, w_ffn_out, final_norm):
    b, t_lat, d = x.shape
    l_ctx = ctx.shape[1]
    s_tot = t_lat + l_ctx
    depth = w_mod.shape[0]
    tm = TOKEN_TILE
    assert t_lat % tm == 0 and l_ctx == tm and t_lat % GRID_W == 0 and d == D_MODEL
    n_lat = t_lat // tm
    n_all = s_tot // tm

    cc = jnp.concatenate([c, c_ctx[None, :], jnp.zeros((16 - b - 1, d), F32)], axis=0)
    mod = _modulation(cc, w_mod, b_mod)
    mod_lat = mod[:, :b].reshape(depth, b, 1, 6, d)
    mod_ctx = jnp.broadcast_to(mod[:, b].reshape(depth, 1, 1, 6, d), (depth, b, 1, 6, d))
    modtab = jnp.pad(jnp.concatenate([mod_lat, mod_ctx], axis=2),
                     ((0, 0), (0, 0), (0, 0), (0, 2), (0, 0)))

    t64 = _rope_tables(t_lat, s_tot)
    ident = np.zeros_like(t64)
    ident[0] = 1.0
    rk = np.concatenate([t64, t64], axis=2)
    rq = rk * np.float32(SWA_HEAD_DIM ** -0.5 * LOG2_E)
    rm = np.concatenate([t64, ident], axis=2)
    rmq = rm * np.float32((MLA_NOPE + MLA_ROPE) ** -0.5 * LOG2_E)
    rq, rk, rm, rmq = (jnp.asarray(a) for a in (rq, rk, rm, rmq))

    xs = (x, ctx)
    for l in range(depth):
        last = l == depth - 1
        n_out = n_lat if last else n_all
        w_in_l, w_vs_t = _prep_w_in(w_in[l])
        wf = jnp.zeros((LANES, GLA_HEADS * GLA_DK), F32).at[
            MISC_GK_LANE:MISC_GK_LANE + GLA_GATE_RANK].set(w_gk_fwd[l])
        wb = jnp.zeros((LANES, GLA_HEADS * GLA_DK), F32).at[
            MISC_GK_LANE + GLA_GATE_RANK:MISC_GK_LANE + 2 * GLA_GATE_RANK].set(w_gk_bwd[l])
        to_heads = lambda w: w.reshape(LANES, GLA_HEADS, GLA_DK).transpose(1, 0, 2).astype(BF16)
        wq = jnp.pad(w_q_up[l].reshape(MLA_Q_RANK, MLA_HEADS, MLA_NOPE + MLA_ROPE),
                     ((0, 0), (0, 0), (0, MLA_QK_PAD - MLA_NOPE - MLA_ROPE))
                     ).reshape(MLA_Q_RANK, MLA_HEADS * MLA_QK_PAD).astype(BF16)

        wkv = w_kv_up[l].reshape(MLA_KV_RANK, MLA_HEADS, MLA_NOPE + MLA_V)
        wk = wkv[:, :, :MLA_NOPE].reshape(MLA_KV_RANK, MLA_HEADS * MLA_NOPE).astype(BF16)
        wvt = wkv[:, :, MLA_NOPE:].reshape(MLA_KV_RANK, MLA_HEADS * MLA_V).T.astype(BF16)
        mla_weights = (wq, wk, wvt, q_norm[l].reshape(1, -1), kv_norm[l].reshape(1, -1))

        qk, va, ga, qs, kx, vx, gk, mg, qm, kn, kr, vm = _inproj(
            xs, modtab[l], norm_mix[l].reshape(1, d), w_in_l, w_vs_t, rq, rk, rm,
            mla_weights, rmq, n_lat)
        ya = _gla(qk, va, ga, gk, to_heads(wf), to_heads(wb),
                  b_gk_fwd[l].reshape(GLA_HEADS, 1, GLA_DK), b_gk_bwd[l].reshape(GLA_HEADS, 1, GLA_DK),
                  gla_norm[l].reshape(1, GLA_DV), t_lat)
        yb = _swa(sinks[l], qs, kx, vx, t_lat, n_out)
        yc = _mla_attn(qm, kn, kr, vm, t_lat, n_out)
        xs = _mix_ffn(xs, ya, yb, yc, mg, modtab[l], w_pa[l].astype(BF16), w_pb[l].astype(BF16),
                      w_pc[l].astype(BF16), w_o[l].astype(BF16), norm_ffn[l].reshape(1, d),
                      w_ffn_in[l].astype(BF16), w_ffn_out[l].astype(BF16), final_norm.reshape(1, d),
                      n_lat, n_out, last)
    return xs
```

```python
import functools

import jax
import jax.numpy as jnp
import numpy as np
from jax import lax
from jax.experimental import pallas as pl
from jax.experimental.pallas import tpu as pltpu

F32 = jnp.float32
BF16 = jnp.bfloat16

D_MODEL = 1024
GRID_W = 64
EPS = 1e-6
ROPE_BASE = 10000.0

GLA_HEADS = 4
GLA_DK = 128
GLA_DV = 256
GLA_GATE_RANK = 16
GLA_GATE_NORM = 16.0
GLA_CHUNK = 64

SWA_HEADS = 16
SWA_KV_HEADS = 2
SWA_GROUP = SWA_HEADS // SWA_KV_HEADS
SWA_HEAD_DIM = 64
WINDOW = 128
SWA_VT_ROWS = SWA_HEAD_DIM + 16

MLA_HEADS = 8
MLA_Q_RANK = 384
MLA_KV_RANK = 256
MLA_NOPE = 128
MLA_ROPE = 64
MLA_V = 128
MLA_QK_PAD = 256
MLA_VT_ROWS = MLA_V + 16
LOG2_E = 1.4426950408889634

D_FF = -(-(8 * D_MODEL) // (3 * 256)) * 256

IN_SPLITS = (
    GLA_HEADS * GLA_DK, GLA_HEADS * GLA_DK, GLA_HEADS * GLA_DV, GLA_HEADS * GLA_DV,
    GLA_GATE_RANK, GLA_GATE_RANK,
    SWA_HEADS * SWA_HEAD_DIM, SWA_KV_HEADS * SWA_HEAD_DIM, SWA_KV_HEADS * SWA_HEAD_DIM,
    MLA_Q_RANK, MLA_KV_RANK, MLA_ROPE,
    3 * D_MODEL,
)

LANES = 128
TOKEN_TILE = 256
MISC_W = 768
MISC_GK_LANE = 64
VMEM_LIMIT = 56 * 1024 * 1024


def _cparams(n_axes):
    return pltpu.CompilerParams(
        dimension_semantics=("arbitrary",) * n_axes, vmem_limit_bytes=VMEM_LIMIT)


def _const_spec(shape):
    nd = len(shape)
    return pl.BlockSpec(shape, lambda *_: (0,) * nd, pipeline_mode=pl.Buffered(1))


def _weight_spec(w):
    if isinstance(w, tuple):
        arr, layer = w
        nd = arr.ndim
        spec = pl.BlockSpec((None,) + arr.shape[1:], lambda *_: (layer,) + (0,) * (nd - 1),
                            pipeline_mode=pl.Buffered(1))
        return arr, spec
    return w, _const_spec(w.shape)


def _rope(x, tab_ref):
    return (x * tab_ref[0] + pltpu.roll(x, 16, 1) * tab_ref[1]
            + pltpu.roll(x, LANES - 16, 1) * tab_ref[2])


def _rms(x):
    return x * lax.rsqrt(jnp.mean(x * x, axis=-1, keepdims=True) + EPS)


def _dot(a, b):
    return jnp.dot(a, b, preferred_element_type=F32)


def _dot_nt(a, b):
    return lax.dot_general(a, b, (((1,), (1,)), ((), ())), preferred_element_type=F32)


def _dot_tn(a, b):
    return lax.dot_general(a, b, (((0,), (0,)), ((), ())), preferred_element_type=F32)


def _mod_kernel(c_ref, w_ref, b_ref, o_ref):
    c = c_ref[...]
    a = (c * jax.nn.sigmoid(c)).astype(BF16)
    o_ref[0] = _dot(a, w_ref[0].astype(BF16)) + b_ref[0]


def _modulation(cc, w_mod, b_mod):
    depth, d, n = w_mod.shape
    tn = 1536
    return pl.pallas_call(
        _mod_kernel,
        grid=(depth, n // tn),
        in_specs=[
            pl.BlockSpec(cc.shape, lambda l, j: (0, 0)),
            pl.BlockSpec((1, d, tn), lambda l, j: (l, 0, j)),
            pl.BlockSpec((1, 1, tn), lambda l, j: (l, 0, j)),
        ],
        out_specs=pl.BlockSpec((1, cc.shape[0], tn), lambda l, j: (l, 0, j)),
        out_shape=jax.ShapeDtypeStruct((depth, cc.shape[0], n), F32),
        compiler_params=_cparams(2),
        name="modulation",
    )(cc, w_mod, b_mod.reshape(depth, 1, n))


def _mla_project(cq, ckv, tail, wq_ref, wk_ref, wvt_ref, qn_ref, kvn_ref, rq_ref,
                 q_ref, k_ref, kr_ref, vt_ref):
    scale = (MLA_NOPE + MLA_ROPE) ** -0.5 * LOG2_E
    lo = lax.broadcasted_iota(jnp.int32, tail.shape, 1) < MLA_ROPE
    kr_ref[0] = jnp.where(lo, tail, 0.0).astype(BF16)

    qf = _dot((_rms(cq) * qn_ref[...]).astype(BF16), wq_ref[...])
    ckvn = (_rms(ckv) * kvn_ref[...]).astype(BF16)
    k_ref[0] = _dot(ckvn, wk_ref[...]).astype(BF16)
    vt = _dot_nt(wvt_ref[...], ckvn)
    ones = jnp.ones((MLA_VT_ROWS - MLA_V, vt.shape[1]), BF16)
    for h in range(MLA_HEADS):
        c0 = h * MLA_QK_PAD
        q_ref[0, :, c0:c0 + LANES] = (qf[:, c0:c0 + LANES] * scale).astype(BF16)
        q_ref[0, :, c0 + LANES:c0 + 2 * LANES] = _rope(
            qf[:, c0 + LANES:c0 + 2 * LANES], rq_ref).astype(BF16)
        vt_ref[0, h, 0:MLA_V, :] = vt[h * MLA_V:(h + 1) * MLA_V, :].astype(BF16)
        vt_ref[0, h, MLA_V:MLA_VT_ROWS, :] = ones


def _gla_decays(qk, tail, wg_ref, bg_ref, sum_ref, qd_ref, ki_ref, tot_ref):
    hk = GLA_HEADS * GLA_DK
    c = GLA_CHUNK
    q = qk[:, 0:hk] * (GLA_DK ** -0.5)
    k = qk[:, hk:]
    z = _dot(tail.astype(BF16), wg_ref[...]) + bg_ref[...]
    la = (jnp.minimum(z, 0.0) - jnp.log1p(jnp.exp(-jnp.abs(z)))) * (1.0 / GLA_GATE_NORM)
    hi = la.astype(BF16)
    lo = (la - hi.astype(F32)).astype(BF16)
    totals = []
    for d in range(2):
        sl = slice(d * hk, (d + 1) * hk)
        cr = _dot(sum_ref[d], jnp.concatenate([hi[:, sl], lo[:, sl]], axis=1))
        cum = cr[:, 0:hk] + cr[:, hk:]
        qd_ref[0, :, sl] = (q * jnp.exp(cum)).astype(BF16)
        ki_ref[0, :, sl] = (k * jnp.exp(-cum)).astype(BF16)
        last = (c - 1) if d == 0 else 0
        ends = [cum[j * c + last:j * c + last + 1] for j in range(TOKEN_TILE // c)]
        totals.append(jnp.concatenate(ends + [jnp.zeros((8 - len(ends), hk), F32)], axis=0))
    tot_ref[0, 0] = jnp.concatenate(totals, axis=1)


def _inproj_kernel(xl_ref, xc_ref, mod_ref, gain_ref, w_ref, wvt_ref, rq_ref, rk_ref, rm_ref,
                   wq_ref, wk_ref, wmvt_ref, qn_ref, kvn_ref, rmq_ref, wg_ref, bg_ref, sum_ref,
                   qd_ref, ki_ref, tot_ref, va_ref, ga_ref, qs_ref, kx_ref, vx_ref, mg_ref,
                   mq_ref, mk_ref, mkr_ref, mvt_ref, *, n_lat):
    x = _pick_tile(xl_ref, xc_ref, n_lat)
    y = _rms(x) * gain_ref[...]
    h = (y * (1.0 + mod_ref[0, 0, 1:2, :]) + mod_ref[0, 0, 0:1, :]).astype(BF16)

    def proj(c0, c1):
        return _dot_nt(h, w_ref[c0:c1, :])

    r = proj(4224, 4224 + MISC_W)
    tail = _rope(r[:, MISC_W - LANES:MISC_W], rm_ref)
    _mla_project(r[:, 0:MLA_Q_RANK], r[:, MLA_Q_RANK:MLA_Q_RANK + MLA_KV_RANK], tail,
                 wq_ref, wk_ref, wmvt_ref, qn_ref, kvn_ref, rmq_ref, mq_ref, mk_ref, mkr_ref, mvt_ref)

    _gla_decays(proj(0, 1024), tail, wg_ref, bg_ref, sum_ref, qd_ref, ki_ref, tot_ref)
    va_ref[0] = proj(1024, 2048).astype(BF16)
    r = proj(2048, 3072)
    ga_ref[0] = (r * jax.nn.sigmoid(r)).astype(BF16)

    r = proj(3072, 4096)
    for j in range(1024 // LANES):
        sl = slice(j * LANES, (j + 1) * LANES)
        qs_ref[0, :, sl] = _rope(r[:, sl], rq_ref).astype(BF16)

    kk = _rope(proj(4096, 4224), rk_ref)
    kx = pltpu.roll(kk, 64, 1)
    lo = lax.broadcasted_iota(jnp.int32, kk.shape, 1) < 64
    zero = jnp.zeros_like(kk)
    blocks = (jnp.where(lo, kk, zero), jnp.where(lo, zero, kx),
              jnp.where(lo, kx, zero), jnp.where(lo, zero, kk))
    for j, blk in enumerate(blocks):
        kx_ref[0, :, j * LANES:(j + 1) * LANES] = blk.astype(BF16)

    vt = _dot_nt(wvt_ref[...], h).astype(BF16)
    hd = SWA_HEAD_DIM
    ones = jnp.ones((SWA_VT_ROWS - hd, vt.shape[1]), BF16)
    for g in range(SWA_KV_HEADS):
        vx_ref[0, g * SWA_VT_ROWS:g * SWA_VT_ROWS + hd, :] = vt[g * hd:(g + 1) * hd, :]
        vx_ref[0, g * SWA_VT_ROWS + hd:(g + 1) * SWA_VT_ROWS, :] = ones

    c0 = 4224 + MISC_W
    for j in range(3):
        r = proj(c0 + j * 1024, c0 + (j + 1) * 1024)
        mg_ref[0, :, j * 1024:(j + 1) * 1024] = jax.nn.sigmoid(r).astype(BF16)


def _inproj(xs, modtab, gain, w, wvt, rq, rk, rm, mla_weights, rmq, gla_weights, n_lat):
    (xl, xc), x_specs = _residual_inputs(xs, n_lat)
    b, _, d = xl.shape
    tm = TOKEN_TILE
    nt = n_lat + 1
    s = nt * tm
    row = lambda bi, i: (bi, i, 0)
    tab = pl.BlockSpec((3, tm, LANES), lambda bi, i: (0, i, 0))
    vx_rows = SWA_KV_HEADS * SWA_VT_ROWS
    gla_w = 2 * GLA_HEADS * GLA_DK
    rows_out = lambda n, dt: (pl.BlockSpec((1, tm, n), row), jax.ShapeDtypeStruct((b, s, n), dt))
    outs = [rows_out(gla_w, BF16), rows_out(gla_w, BF16),
            (pl.BlockSpec((1, 1, 8, gla_w), lambda bi, i: (bi, i, 0, 0)),
             jax.ShapeDtypeStruct((b, nt, 8, gla_w), F32)),
            rows_out(1024, BF16), rows_out(1024, BF16), rows_out(1024, BF16),
            rows_out(SWA_KV_HEADS * 2 * LANES, BF16),
            (pl.BlockSpec((1, vx_rows, tm), lambda bi, i: (bi, 0, i)),
             jax.ShapeDtypeStruct((b, vx_rows, s), BF16)),
            rows_out(3072, BF16),
            rows_out(MLA_HEADS * MLA_QK_PAD, BF16), rows_out(MLA_HEADS * MLA_NOPE, BF16),
            rows_out(LANES, BF16),
            (pl.BlockSpec((1, MLA_HEADS, MLA_VT_ROWS, tm), lambda bi, i: (bi, 0, 0, i)),
             jax.ShapeDtypeStruct((b, MLA_HEADS, MLA_VT_ROWS, s), BF16))]
    (w, w_spec), (wvt, wvt_spec) = _weight_spec(w), _weight_spec(wvt)
    return pl.pallas_call(
        functools.partial(_inproj_kernel, n_lat=n_lat),
        grid=(b, nt),
        in_specs=x_specs + [
            pl.BlockSpec((1, 1, 8, d), lambda bi, i: (bi, i // n_lat, 0, 0)),
            _const_spec((1, d)),
            w_spec, wvt_spec,
            tab, tab, tab,
        ] + [_const_spec(a.shape) for a in mla_weights] + [tab]
        + [_const_spec(a.shape) for a in gla_weights],
        out_specs=[o[0] for o in outs],
        out_shape=[o[1] for o in outs],
        compiler_params=_cparams(2),
        name="inproj",
    )(xl, xc, modtab, gain, w, wvt, rq, rk, rm, *mla_weights, rmq, *gla_weights)


def _gla_kernel(qdf_ref, qdb_ref, kif_ref, kib_ref, v_ref, g_ref, totf_ref, totb_ref, gn_ref,
                sel_ref, o_ref,
                of_ref, ob_ref, af_ref, ab_ref, stf_ref, stb_ref, decf_ref, decb_ref,
                keepf_ref, keepb_ref, *, n_lat):
    c = GLA_CHUNK
    tm = TOKEN_TILE
    per = tm // c

    ri = lax.broadcasted_iota(jnp.int32, (tm, tm), 0)
    ci = lax.broadcasted_iota(jnp.int32, (tm, tm), 1)
    same = (ri // c) == (ci // c)
    keepf_ref[...] = jnp.where(same & (ci <= ri), 1.0, 0.0)
    keepb_ref[...] = jnp.where(same & (ci > ri), 1.0, 0.0)

    def tile_rows(t):
        return pl.ds(pl.multiple_of(t * tm, tm), tm)

    def in_chunk(t, qd_ref, ki_ref, tot_ref, keep_ref, a_ref, dec_ref, out_ref):
        rows = tile_rows(t)
        tot = tot_ref[0, t]
        hi = tot.astype(BF16)
        lo = (tot - hi.astype(F32)).astype(BF16)
        dec_ref[t] = jnp.exp(_dot_tn(jnp.concatenate([hi, lo], axis=0), sel_ref[...]))
        v = v_ref[0, rows, :]
        q_dec = qd_ref[0, rows, :]
        k_inv = ki_ref[0, rows, :]
        for j in range(per):
            a_ref[t * per + j] = _dot_tn(k_inv[j * c:(j + 1) * c], v[j * c:(j + 1) * c])
        scores = jnp.where(keep_ref[...] > 0.5, _dot_nt(q_dec, k_inv), 0.0).astype(BF16)
        out_ref[rows, :] = _dot(scores, v)

    def bulk_body(t, carry):
        in_chunk(t, qdf_ref, kif_ref, totf_ref, keepf_ref, af_ref, decf_ref, of_ref)
        in_chunk(t, qdb_ref, kib_ref, totb_ref, keepb_ref, ab_ref, decb_ref, ob_ref)
        return carry

    lax.fori_loop(0, n_lat + 1, bulk_body, 0, unroll=3)

    def scan_tile(t, order, state, a_ref, dec_ref, st_ref):
        for j in order:
            st_ref[t * per + j] = state.astype(BF16)
            dec = dec_ref[t, :, j * LANES:(j + 1) * LANES]
            state = jnp.concatenate([dec] * (GLA_DV // LANES), axis=1) * (state + a_ref[t * per + j])
        return state

    fwd_order = range(per)
    bwd_order = range(per - 1, -1, -1)
    zero = jnp.zeros((GLA_DK, GLA_DV), F32)
    s_f = scan_tile(n_lat, fwd_order, zero, af_ref, decf_ref, stf_ref)
    s_b = scan_tile(n_lat, bwd_order, zero, ab_ref, decb_ref, stb_ref)
    lax.fori_loop(0, n_lat, lambda i, s: scan_tile(
        i, fwd_order, s, af_ref, decf_ref, stf_ref), s_f)
    lax.fori_loop(0, n_lat, lambda i, s: scan_tile(
        n_lat - 1 - i, bwd_order, s, ab_ref, decb_ref, stb_ref), s_b)

    def post_body(t, carry):
        for j in range(per):
            rows = pl.ds(pl.multiple_of(t * tm + j * c, c), c)
            o = of_ref[rows, :] + ob_ref[rows, :]
            o += _dot(qdf_ref[0, rows, :], stf_ref[t * per + j])
            o += _dot(qdb_ref[0, rows, :], stb_ref[t * per + j])
            y = _rms(o) * gn_ref[...]
            o_ref[0, rows, :] = (y * g_ref[0, rows, :].astype(F32)).astype(BF16)
        return carry

    lax.fori_loop(0, n_lat + 1, post_body, 0, unroll=3)


def _gla_sums():
    tm, c = TOKEN_TILE, GLA_CHUNK
    r = np.arange(tm)[:, None]
    col = np.arange(tm)[None, :]
    same = (r // c) == (col // c)
    return jnp.asarray(np.stack([same & (col <= r), same & (col >= r)]).astype(np.float32), BF16)


def _gla_selector():
    per = TOKEN_TILE // GLA_CHUNK
    sel = np.zeros((16, per * LANES), np.float32)
    for j in range(per):
        sel[j, j * LANES:(j + 1) * LANES] = 1.0
        sel[8 + j, j * LANES:(j + 1) * LANES] = 1.0
    return jnp.asarray(sel, BF16)


def _gla(qd, ki, tot, va, ga, gn, t_lat):
    b, s, _ = va.shape
    tm = TOKEN_TILE
    n_lat = t_lat // tm
    n_tiles = s // tm
    n_chunks = s // GLA_CHUNK
    h = GLA_HEADS
    sel = _gla_selector()
    fwd = lambda bi, hi: (bi, 0, hi)
    bwd = lambda bi, hi: (bi, 0, h + hi)
    return pl.pallas_call(
        functools.partial(_gla_kernel, n_lat=n_lat),
        grid=(b, h),
        in_specs=[
            pl.BlockSpec((1, s, GLA_DK), fwd), pl.BlockSpec((1, s, GLA_DK), bwd),
            pl.BlockSpec((1, s, GLA_DK), fwd), pl.BlockSpec((1, s, GLA_DK), bwd),
            pl.BlockSpec((1, s, GLA_DV), fwd), pl.BlockSpec((1, s, GLA_DV), fwd),
            pl.BlockSpec((1, n_tiles, 8, GLA_DK), lambda bi, hi: (bi, 0, 0, hi)),
            pl.BlockSpec((1, n_tiles, 8, GLA_DK), lambda bi, hi: (bi, 0, 0, h + hi)),
            pl.BlockSpec((1, GLA_DV), lambda bi, hi: (0, 0)),
            _const_spec(sel.shape),
        ],
        out_specs=pl.BlockSpec((1, s, GLA_DV), fwd),
        out_shape=jax.ShapeDtypeStruct((b, s, h * GLA_DV), BF16),
        scratch_shapes=[
            pltpu.VMEM((s, GLA_DV), F32), pltpu.VMEM((s, GLA_DV), F32),
            pltpu.VMEM((n_chunks, GLA_DK, GLA_DV), F32), pltpu.VMEM((n_chunks, GLA_DK, GLA_DV), F32),
            pltpu.VMEM((n_chunks, GLA_DK, GLA_DV), BF16), pltpu.VMEM((n_chunks, GLA_DK, GLA_DV), BF16),
            pltpu.VMEM((n_tiles, GLA_DK, tm // GLA_CHUNK * LANES), F32),
            pltpu.VMEM((n_tiles, GLA_DK, tm // GLA_CHUNK * LANES), F32),
            pltpu.VMEM((tm, tm), F32), pltpu.VMEM((tm, tm), F32),
        ],
        compiler_params=_cparams(2),
        name="gla",
    )(qd, qd, ki, ki, va, ga, tot, tot, gn, sel)


def _swa_kernel(sink_ref, q_ref, kx_ref, vx_ref, bias_ref, o_ref, s0_ref, s1_ref, m0_ref, m1_ref,
                *, n_lat, t_lat, with_ctx):
    tq = TOKEN_TILE
    span = tq + 2 * WINDOW
    hd = SWA_HEAD_DIM
    g = pl.program_id(1)
    ka, kb = slice(0, LANES), slice(LANES, 2 * LANES)

    def q_rows(i):
        return pl.ds(pl.multiple_of(i * tq, tq), tq)

    def sink_of(head):
        return sink_ref[g * SWA_GROUP + head] * LOG2_E

    def keys(i):
        ws = pl.multiple_of(jnp.clip(i * tq - WINDOW, 0, t_lat - span), WINDOW)
        win = pl.ds(ws, span)
        kc = [jnp.concatenate([kx_ref[0, win, c], kx_ref[0, t_lat:, c]], axis=0) for c in (ka, kb)]
        vc = jnp.concatenate([vx_ref[0, :, win], vx_ref[0, :, t_lat:]], axis=1)
        return kc, vc, (i * tq - ws) // WINDOW

    def scores(i, kc, bias_idx, head, s_ref, m_ref):
        qp = q_ref[0, q_rows(i), (head // 2) * LANES:(head // 2 + 1) * LANES]
        s_t = _dot_nt(kc[head % 2], qp)
        top = s_t[0:span] + bias_ref[bias_idx]
        bot = s_t[span:]
        s_ref[0:span, :] = top
        s_ref[span:, :] = bot
        m = jnp.maximum(jnp.max(top, axis=0, keepdims=True), jnp.max(bot, axis=0, keepdims=True))
        m_ref[...] = jnp.broadcast_to(jnp.maximum(m, sink_of(head)), m_ref.shape)

    def normalise(acc, m, head):
        return acc[0:hd] / (acc[hd:hd + 1] + jnp.exp2(sink_of(head) - m))

    def values(vc, head, s_ref, m_ref):
        m = m_ref[0:1, :]
        p_t = jnp.exp2(s_ref[...] - m).astype(BF16)
        return normalise(_dot(vc, p_t), m, head)

    def store(i, pair, o_a, o_b):
        o_t = jnp.concatenate([o_a, o_b], axis=0)
        o_ref[0, q_rows(i), pair * LANES:(pair + 1) * LANES] = o_t.T.astype(BF16)

    def scores_all(i, s_ref, m_ref):
        kc, _, bias_idx = keys(i)
        for head in range(SWA_GROUP):
            scores(i, kc, bias_idx, head, s_ref.at[head], m_ref.at[head])

    def values_all(i, s_ref, m_ref):
        _, vc, _ = keys(i)
        for pair in range(SWA_GROUP // 2):
            a, b = 2 * pair, 2 * pair + 1
            store(i, pair, values(vc, a, s_ref.at[a], m_ref.at[a]),
                  values(vc, b, s_ref.at[b], m_ref.at[b]))

    scores_all(0, s0_ref, m0_ref)

    def body(j, carry):
        scores_all(2 * j + 1, s1_ref, m1_ref)
        values_all(2 * j, s0_ref, m0_ref)
        scores_all(2 * j + 2, s0_ref, m0_ref)
        values_all(2 * j + 1, s1_ref, m1_ref)
        return carry

    lax.fori_loop(0, n_lat // 2 - 1, body, 0)
    scores_all(n_lat - 1, s1_ref, m1_ref)
    values_all(n_lat - 2, s0_ref, m0_ref)
    values_all(n_lat - 1, s1_ref, m1_ref)

    if with_ctx:
        rows = slice(t_lat, t_lat + tq)
        for pair in range(SWA_GROUP // 2):
            qp = q_ref[0, rows, pair * LANES:(pair + 1) * LANES]
            outs = []
            for head, kcols in ((2 * pair, ka), (2 * pair + 1, kb)):
                s_t = _dot_nt(kx_ref[0, t_lat:, kcols], qp)
                m = jnp.maximum(jnp.max(s_t, axis=0, keepdims=True), sink_of(head))
                p_t = jnp.exp2(s_t - m).astype(BF16)
                outs.append(normalise(_dot(vx_ref[0, :, t_lat:], p_t), m, head))
            o_t = jnp.concatenate(outs, axis=0)
            o_ref[0, rows, pair * LANES:(pair + 1) * LANES] = o_t.T.astype(BF16)


def _swa_bias():
    tq, span = TOKEN_TILE, TOKEN_TILE + 2 * WINDOW
    r = np.arange(span)[:, None]
    c = np.arange(tq)[None, :]
    tabs = [np.where(np.abs(r - off - c) <= WINDOW, 0.0, -1e30) for off in (0, WINDOW, 2 * WINDOW)]
    return jnp.asarray(np.stack(tabs).astype(np.float32))


def _swa(sinks, qs, kx, vx, t_lat, n_tiles):
    b, s, _ = qs.shape
    tq = TOKEN_TILE
    gw = SWA_GROUP * SWA_HEAD_DIM
    n_keys = tq + 2 * WINDOW + (s - t_lat)
    bias = _swa_bias()
    n_lat = t_lat // tq
    return pl.pallas_call(
        functools.partial(_swa_kernel, n_lat=n_lat, t_lat=t_lat, with_ctx=n_tiles > n_lat),
        grid=(b, SWA_KV_HEADS),
        in_specs=[
            pl.BlockSpec(memory_space=pltpu.SMEM),
            pl.BlockSpec((1, s, gw), lambda bi, gi: (bi, 0, gi)),
            pl.BlockSpec((1, s, 2 * LANES), lambda bi, gi: (bi, 0, gi)),
            pl.BlockSpec((1, SWA_VT_ROWS, s), lambda bi, gi: (bi, gi, 0)),
            _const_spec(bias.shape),
        ],
        out_specs=pl.BlockSpec((1, n_tiles * tq, gw), lambda bi, gi: (bi, 0, gi)),
        out_shape=jax.ShapeDtypeStruct((b, n_tiles * tq, SWA_HEADS * SWA_HEAD_DIM), BF16),
        scratch_shapes=[pltpu.VMEM((SWA_GROUP, n_keys, tq), F32), pltpu.VMEM((SWA_GROUP, n_keys, tq), F32),
                        pltpu.VMEM((SWA_GROUP, 8, tq), F32), pltpu.VMEM((SWA_GROUP, 8, tq), F32)],
        compiler_params=_cparams(2),
        name="swa",
    )(sinks, qs, kx, vx, bias)


def _mla_attn_kernel(q_ref, kn_ref, kr_ref, vt_ref, o_ref, k_ref, s0_ref, s1_ref, m0_ref, m1_ref,
                     *, n_lat, t_lat, with_ctx):
    tq = TOKEN_TILE
    k_ref[:, 0:MLA_NOPE] = kn_ref[0]
    k_ref[:, MLA_NOPE:] = kr_ref[0]

    def q_rows(i):
        return pl.ds(pl.multiple_of(i * tq, tq), tq)

    def finish(acc, rows):
        o_t = acc[0:MLA_V] / acc[MLA_V:MLA_V + 1]
        o_ref[0, rows, :] = o_t.T.astype(BF16)

    def scores(i, s_ref, m_ref):
        q = q_ref[0, q_rows(i), :]
        half = k_ref.shape[0] // 2
        lo = _dot_nt(k_ref[0:half, :], q)
        hi = _dot_nt(k_ref[half:, :], q)
        s_ref[0:half, :] = lo
        s_ref[half:, :] = hi
        m = jnp.maximum(jnp.max(lo, axis=0, keepdims=True), jnp.max(hi, axis=0, keepdims=True))
        m_ref[...] = jnp.broadcast_to(m, m_ref.shape)

    def values(i, s_ref, m_ref):
        p_t = jnp.exp2(s_ref[...] - m_ref[0:1, :]).astype(BF16)
        finish(_dot(vt_ref[0, 0], p_t), q_rows(i))

    scores(0, s0_ref, m0_ref)

    def body(j, carry):
        scores(2 * j + 1, s1_ref, m1_ref)
        values(2 * j, s0_ref, m0_ref)
        scores(2 * j + 2, s0_ref, m0_ref)
        values(2 * j + 1, s1_ref, m1_ref)
        return carry

    lax.fori_loop(0, n_lat // 2 - 1, body, 0)
    scores(n_lat - 1, s1_ref, m1_ref)
    values(n_lat - 2, s0_ref, m0_ref)
    values(n_lat - 1, s1_ref, m1_ref)

    if with_ctx:
        s_t = _dot_nt(k_ref[t_lat:, :], q_ref[0, t_lat:, :])
        p_t = jnp.exp2(s_t - jnp.max(s_t, axis=0, keepdims=True)).astype(BF16)
        finish(_dot(vt_ref[0, 0, :, t_lat:], p_t), slice(t_lat, t_lat + tq))


def _mla_attn(q, kn, kr, vt, t_lat, n_tiles):
    b, s, _ = q.shape
    tq = TOKEN_TILE
    n_lat = t_lat // tq
    return pl.pallas_call(
        functools.partial(_mla_attn_kernel, n_lat=n_lat, t_lat=t_lat, with_ctx=n_tiles > n_lat),
        grid=(b, MLA_HEADS),
        in_specs=[
            pl.BlockSpec((1, s, MLA_QK_PAD), lambda bi, hi: (bi, 0, hi)),
            pl.BlockSpec((1, s, MLA_NOPE), lambda bi, hi: (bi, 0, hi)),
            pl.BlockSpec((1, s, LANES), lambda bi, hi: (bi, 0, 0)),
            pl.BlockSpec((1, 1, MLA_VT_ROWS, s), lambda bi, hi: (bi, hi, 0, 0)),
        ],
        out_specs=pl.BlockSpec((1, n_tiles * tq, MLA_V), lambda bi, hi: (bi, 0, hi)),
        out_shape=jax.ShapeDtypeStruct((b, n_tiles * tq, MLA_HEADS * MLA_V), BF16),
        scratch_shapes=[pltpu.VMEM((s, MLA_QK_PAD), BF16),
                        pltpu.VMEM((s, tq), F32), pltpu.VMEM((s, tq), F32),
                        pltpu.VMEM((8, tq), F32), pltpu.VMEM((8, tq), F32)],
        compiler_params=_cparams(2),
        name="mla_attn",
    )(q, kn, kr, vt)


def _residual_inputs(xs, n_lat):
    tm = TOKEN_TILE
    if isinstance(xs, tuple):
        lat, ctx, ctx_block = xs[0], xs[1], 0
    else:
        lat, ctx, ctx_block = xs, xs, n_lat
    d = lat.shape[-1]
    specs = [pl.BlockSpec((1, tm, d), lambda bi, i: (bi, jnp.minimum(i, n_lat - 1), 0)),
             pl.BlockSpec((1, tm, d), lambda bi, i: (bi, ctx_block, 0))]
    return (lat, ctx), specs


def _pick_tile(xl_ref, xc_ref, n_lat):
    return jnp.where(pl.program_id(1) < n_lat, xl_ref[0], xc_ref[0])


def _mix_ffn_kernel(xl_ref, xc_ref, ya_ref, yb_ref, yc_ref, mg_ref, mod_ref,
                    wa_ref, wb_ref, wc_ref, wo_ref, gain_ref, wi_ref, wd_ref, fin_ref, o_ref,
                    *, n_lat, final):
    d = D_MODEL
    u = mg_ref[0, :, 0:d].astype(F32) * _dot(ya_ref[0], wa_ref[...])
    u += mg_ref[0, :, d:2 * d].astype(F32) * _dot(yb_ref[0], wb_ref[...])
    u += mg_ref[0, :, 2 * d:3 * d].astype(F32) * _dot(yc_ref[0], wc_ref[...])
    x = _pick_tile(xl_ref, xc_ref, n_lat) + mod_ref[0, 0, 2:3, :] * _dot(u.astype(BF16), wo_ref[...])

    y = _rms(x) * gain_ref[...]
    h = (y * (1.0 + mod_ref[0, 0, 4:5, :]) + mod_ref[0, 0, 3:4, :]).astype(BF16)
    gate = _dot(h, wi_ref[:, 0:D_FF])
    up = _dot(h, wi_ref[:, D_FF:2 * D_FF])
    a = (gate * jax.nn.sigmoid(gate) * up).astype(BF16)
    x = x + mod_ref[0, 0, 5:6, :] * _dot(a, wd_ref[...])
    if final:
        x = _rms(x) * fin_ref[...]
    o_ref[0] = x


def _mix_ffn(xs, ya, yb, yc, mg, modtab, wa, wb, wc, wo, gain, wi, wd, fin, n_lat, n_tiles, final):
    (xl, xc), x_specs = _residual_inputs(xs, n_lat)
    b, _, d = xl.shape
    tm = TOKEN_TILE
    row = lambda bi, i: (bi, i, 0)
    weights, w_specs = zip(*[_weight_spec(w) for w in (wa, wb, wc, wo, gain, wi, wd, fin)])
    return pl.pallas_call(
        functools.partial(_mix_ffn_kernel, n_lat=n_lat, final=final),
        grid=(b, n_tiles),
        in_specs=x_specs + [
            pl.BlockSpec((1, tm, d), row), pl.BlockSpec((1, tm, d), row),
            pl.BlockSpec((1, tm, d), row), pl.BlockSpec((1, tm, 3 * d), row),
            pl.BlockSpec((1, 1, 8, d), lambda bi, i: (bi, i // n_lat, 0, 0)),
        ] + list(w_specs),
        out_specs=pl.BlockSpec((1, tm, d), row),
        out_shape=jax.ShapeDtypeStruct((b, n_tiles * tm, d), F32),
        compiler_params=_cparams(2),
        name="mix_ffn",
    )(xl, xc, ya, yb, yc, mg, modtab, *weights)


def _rope_tables(t_lat, s_tot):
    half = SWA_HEAD_DIM // 4
    inv = np.power(ROPE_BASE, -np.arange(half, dtype=np.float32) / half).astype(np.float32)
    pos = np.arange(t_lat)
    lane = np.arange(SWA_HEAD_DIM)
    p = np.where(lane[None, :] < 2 * half, (pos // GRID_W)[:, None], (pos % GRID_W)[:, None])
    ang = p.astype(np.float32) * inv[lane % half][None, :]
    cos, sin = np.cos(ang), np.sin(ang)
    upper = (lane % (2 * half)) >= half
    tabs = np.stack([cos, np.where(upper, sin, 0.0), np.where(upper, 0.0, -sin)]).astype(np.float32)
    ident = np.zeros((3, s_tot - t_lat, SWA_HEAD_DIM), np.float32)
    ident[0] = 1.0
    return np.concatenate([tabs, ident], axis=1)


def _prep_w_in(w_in):
    w_t = jnp.swapaxes(w_in, 1, 2).astype(BF16)
    parts, idx = [], 0
    for n in IN_SPLITS:
        parts.append(w_t[:, idx:idx + n])
        idx += n
    qa, ka, va, ga, gkf, gkb, qs, ks, vs, cq, ckv, kr, mg = parts
    pad = jnp.zeros((w_in.shape[0], 32, w_in.shape[1]), BF16)
    return jnp.concatenate([qa, ka, va, ga, qs, ks, cq, ckv, kr, gkf, gkb, pad, mg], axis=1), vs


def kernel(x, c, ctx, c_ctx, w_mod, b_mod, norm_mix, w_in, w_gk_fwd, b_gk_fwd, w_gk_bwd, b_gk_bwd,
           gla_norm, sinks, q_norm, w_q_up, kv_norm, w_kv_up, w_pa, w_pb, w_pc, w_o,
           norm_ffn, w_ffn_in, w_ffn_out, final_norm):
    b, t_lat, d = x.shape
    l_ctx = ctx.shape[1]
    s_tot = t_lat + l_ctx
    depth = w_mod.shape[0]
    tm = TOKEN_TILE
    assert t_lat % tm == 0 and l_ctx == tm and t_lat % GRID_W == 0 and d == D_MODEL
    n_lat = t_lat // tm
    n_all = s_tot // tm

    cc = jnp.concatenate([c, c_ctx[None, :], jnp.zeros((16 - b - 1, d), F32)], axis=0)
    mod = _modulation(cc, w_mod, b_mod)
    mod_lat = mod[:, :b].reshape(depth, b, 1, 6, d)
    mod_ctx = jnp.broadcast_to(mod[:, b].reshape(depth, 1, 1, 6, d), (depth, b, 1, 6, d))
    modtab = jnp.pad(jnp.concatenate([mod_lat, mod_ctx], axis=2),
                     ((0, 0), (0, 0), (0, 0), (0, 2), (0, 0)))

    t64 = _rope_tables(t_lat, s_tot)
    ident = np.zeros_like(t64)
    ident[0] = 1.0
    rk = np.concatenate([t64, t64], axis=2)
    rq = rk * np.float32(SWA_HEAD_DIM ** -0.5 * LOG2_E)
    rm = np.concatenate([t64, ident], axis=2)
    rmq = rm * np.float32((MLA_NOPE + MLA_ROPE) ** -0.5 * LOG2_E)
    rq, rk, rm, rmq = (jnp.asarray(a) for a in (rq, rk, rm, rmq))

    w_in_t, w_vs_t = _prep_w_in(w_in)
    w_pa, w_pb, w_pc, w_o, w_ffn_in, w_ffn_out = (
        w.astype(BF16) for w in (w_pa, w_pb, w_pc, w_o, w_ffn_in, w_ffn_out))

    gla_sums = _gla_sums()

    xs = (x, ctx)
    for l in range(depth):
        last = l == depth - 1
        n_out = n_lat if last else n_all
        hk = GLA_HEADS * GLA_DK
        wg = jnp.zeros((LANES, 2 * hk), F32)
        wg = wg.at[MISC_GK_LANE:MISC_GK_LANE + GLA_GATE_RANK, 0:hk].set(w_gk_fwd[l])
        wg = wg.at[MISC_GK_LANE + GLA_GATE_RANK:MISC_GK_LANE + 2 * GLA_GATE_RANK, hk:].set(w_gk_bwd[l])
        bg = jnp.concatenate([b_gk_fwd[l], b_gk_bwd[l]]).reshape(1, 2 * hk)
        gla_weights = (wg.astype(BF16), bg, gla_sums)
        wq = jnp.pad(w_q_up[l].reshape(MLA_Q_RANK, MLA_HEADS, MLA_NOPE + MLA_ROPE),
                     ((0, 0), (0, 0), (0, MLA_QK_PAD - MLA_NOPE - MLA_ROPE))
                     ).reshape(MLA_Q_RANK, MLA_HEADS * MLA_QK_PAD).astype(BF16)

        wkv = w_kv_up[l].reshape(MLA_KV_RANK, MLA_HEADS, MLA_NOPE + MLA_V)
        wk = wkv[:, :, :MLA_NOPE].reshape(MLA_KV_RANK, MLA_HEADS * MLA_NOPE).astype(BF16)
        wvt = wkv[:, :, MLA_NOPE:].reshape(MLA_KV_RANK, MLA_HEADS * MLA_V).T.astype(BF16)
        mla_weights = (wq, wk, wvt, q_norm[l].reshape(1, -1), kv_norm[l].reshape(1, -1))

        qd, ki, tot, va, ga, qs, kx, vx, mg, qm, kn, kr, vm = _inproj(
            xs, modtab[l], norm_mix[l].reshape(1, d), (w_in_t, l), (w_vs_t, l), rq, rk, rm,
            mla_weights, rmq, gla_weights, n_lat)
        ya = _gla(qd, ki, tot, va, ga, gla_norm[l].reshape(1, GLA_DV), t_lat)
        yb = _swa(sinks[l], qs, kx, vx, t_lat, n_out)
        yc = _mla_attn(qm, kn, kr, vm, t_lat, n_out)
        xs = _mix_ffn(xs, ya, yb, yc, mg, modtab[l], (w_pa, l), (w_pb, l), (w_pc, l), (w_o, l),
                      norm_ffn[l].reshape(1, d), (w_ffn_in, l), (w_ffn_out, l),
                      final_norm.reshape(1, d), n_lat, n_out, last)
    return xs
```

```python
import functools

import jax
import jax.numpy as jnp
import numpy as np
from jax import lax
from jax.experimental import pallas as pl
from jax.experimental.pallas import tpu as pltpu

F32 = jnp.float32
BF16 = jnp.bfloat16

D_MODEL = 1024
GRID_W = 64
EPS = 1e-6
ROPE_BASE = 10000.0

GLA_HEADS = 4
GLA_DK = 128
GLA_DV = 256
GLA_GATE_RANK = 16
GLA_GATE_NORM = 16.0
GLA_CHUNK = 64

SWA_HEADS = 16
SWA_KV_HEADS = 2
SWA_GROUP = SWA_HEADS // SWA_KV_HEADS
SWA_HEAD_DIM = 64
WINDOW = 128
SWA_VT_ROWS = SWA_HEAD_DIM + 16

MLA_HEADS = 8
MLA_Q_RANK = 384
MLA_KV_RANK = 256
MLA_NOPE = 128
MLA_ROPE = 64
MLA_V = 128
MLA_QK_PAD = 256
MLA_VT_ROWS = MLA_V + 16
LOG2_E = 1.4426950408889634

D_FF = -(-(8 * D_MODEL) // (3 * 256)) * 256

IN_SPLITS = (
    GLA_HEADS * GLA_DK, GLA_HEADS * GLA_DK, GLA_HEADS * GLA_DV, GLA_HEADS * GLA_DV,
    GLA_GATE_RANK, GLA_GATE_RANK,
    SWA_HEADS * SWA_HEAD_DIM, SWA_KV_HEADS * SWA_HEAD_DIM, SWA_KV_HEADS * SWA_HEAD_DIM,
    MLA_Q_RANK, MLA_KV_RANK, MLA_ROPE,
    3 * D_MODEL,
)

LANES = 128
TOKEN_TILE = 256
(ROW_QA, ROW_KA, ROW_VA, ROW_GA, ROW_GKF, ROW_GKB, ROW_QS, ROW_KS, ROW_VS,
 ROW_CQ, ROW_CKV, ROW_KR, ROW_MG, ROW_END) = (int(v) for v in np.cumsum((0,) + IN_SPLITS))
VMEM_LIMIT = 56 * 1024 * 1024


def _cparams(n_axes):
    return pltpu.CompilerParams(
        dimension_semantics=("arbitrary",) * n_axes, vmem_limit_bytes=VMEM_LIMIT)


def _const_spec(shape):
    nd = len(shape)
    return pl.BlockSpec(shape, lambda *_: (0,) * nd, pipeline_mode=pl.Buffered(1))


def _weight_spec(w):
    if isinstance(w, tuple):
        arr, layer = w
        nd = arr.ndim
        spec = pl.BlockSpec((None,) + arr.shape[1:], lambda *_: (layer,) + (0,) * (nd - 1),
                            pipeline_mode=pl.Buffered(1))
        return arr, spec
    return w, _const_spec(w.shape)


def _rope(x, tab_ref):
    return (x * tab_ref[0] + pltpu.roll(x, 16, 1) * tab_ref[1]
            + pltpu.roll(x, LANES - 16, 1) * tab_ref[2])


def _rms(x):
    return x * lax.rsqrt(jnp.mean(x * x, axis=-1, keepdims=True) + EPS)


def _dot(a, b):
    return jnp.dot(a, b, preferred_element_type=F32)


def _dot_nt(a, b):
    return lax.dot_general(a, b, (((1,), (1,)), ((), ())), preferred_element_type=F32)


def _dot_tn(a, b):
    return lax.dot_general(a, b, (((0,), (0,)), ((), ())), preferred_element_type=F32)


def _mod_kernel(c_ref, w_ref, b_ref, o_ref):
    c = c_ref[...]
    a = (c * jax.nn.sigmoid(c)).astype(BF16)
    o_ref[0] = _dot(a, w_ref[0].astype(BF16)) + b_ref[0]


def _modulation(cc, w_mod, b_mod):
    depth, d, n = w_mod.shape
    tn = 1536
    return pl.pallas_call(
        _mod_kernel,
        grid=(depth, n // tn),
        in_specs=[
            pl.BlockSpec(cc.shape, lambda l, j: (0, 0)),
            pl.BlockSpec((1, d, tn), lambda l, j: (l, 0, j)),
            pl.BlockSpec((1, 1, tn), lambda l, j: (l, 0, j)),
        ],
        out_specs=pl.BlockSpec((1, cc.shape[0], tn), lambda l, j: (l, 0, j)),
        out_shape=jax.ShapeDtypeStruct((depth, cc.shape[0], n), F32),
        compiler_params=_cparams(2),
        name="modulation",
    )(cc, w_mod, b_mod.reshape(depth, 1, n))


def _mla_project(cq, ckv, tail, wq_ref, wk_ref, wvt_ref, qn_ref, kvn_ref, rq_ref,
                 q_ref, k_ref, kr_ref, vt_ref):
    scale = (MLA_NOPE + MLA_ROPE) ** -0.5 * LOG2_E
    lo = lax.broadcasted_iota(jnp.int32, tail.shape, 1) < MLA_ROPE
    kr_ref[0] = jnp.where(lo, tail, 0.0).astype(BF16)

    qf = _dot((_rms(cq) * qn_ref[...]).astype(BF16), wq_ref[...])
    ckvn = (_rms(ckv) * kvn_ref[...]).astype(BF16)
    k_ref[0] = _dot(ckvn, wk_ref[...]).astype(BF16)
    vt = _dot_nt(wvt_ref[...], ckvn)
    ones = jnp.ones((MLA_VT_ROWS - MLA_V, vt.shape[1]), BF16)
    for h in range(MLA_HEADS):
        c0 = h * MLA_QK_PAD
        q_ref[0, :, c0:c0 + LANES] = (qf[:, c0:c0 + LANES] * scale).astype(BF16)
        q_ref[0, :, c0 + LANES:c0 + 2 * LANES] = _rope(
            qf[:, c0 + LANES:c0 + 2 * LANES], rq_ref).astype(BF16)
        vt_ref[0, h, 0:MLA_V, :] = vt[h * MLA_V:(h + 1) * MLA_V, :].astype(BF16)
        vt_ref[0, h, MLA_V:MLA_VT_ROWS, :] = ones


def _gla_decays(qk, gates, wg_ref, bg_ref, sum_ref, qd_ref, ki_ref, tot_ref):
    hk = GLA_HEADS * GLA_DK
    c = GLA_CHUNK
    q = qk[:, 0:hk] * (GLA_DK ** -0.5)
    k = qk[:, hk:]
    z = _dot(gates.astype(BF16), wg_ref[...]) + bg_ref[...]
    la = (jnp.minimum(z, 0.0) - jnp.log1p(jnp.exp(-jnp.abs(z)))) * (1.0 / GLA_GATE_NORM)
    hi = la.astype(BF16)
    lo = (la - hi.astype(F32)).astype(BF16)
    totals = []
    for d in range(2):
        sl = slice(d * hk, (d + 1) * hk)
        cr = _dot(sum_ref[d], jnp.concatenate([hi[:, sl], lo[:, sl]], axis=1))
        cum = cr[:, 0:hk] + cr[:, hk:]
        qd_ref[0, :, sl] = (q * jnp.exp(cum)).astype(BF16)
        ki_ref[0, :, sl] = (k * jnp.exp(-cum)).astype(BF16)
        last = (c - 1) if d == 0 else 0
        ends = [cum[j * c + last:j * c + last + 1] for j in range(TOKEN_TILE // c)]
        totals.append(jnp.concatenate(ends + [jnp.zeros((8 - len(ends), hk), F32)], axis=0))
    tot_ref[0, 0] = jnp.concatenate(totals, axis=1)


def _inproj_kernel(xl_ref, xc_ref, mod_ref, gain_ref, w_ref, rq_ref, rk_ref, rm_ref,
                   wq_ref, wk_ref, wmvt_ref, qn_ref, kvn_ref, rmq_ref, wg_ref, bg_ref, sum_ref,
                   qd_ref, ki_ref, tot_ref, va_ref, ga_ref, qs_ref, kx_ref, vx_ref, mg_ref,
                   mq_ref, mk_ref, mkr_ref, mvt_ref, *, n_lat):
    x = _pick_tile(xl_ref, xc_ref, n_lat)
    y = _rms(x) * gain_ref[...]
    h = (y * (1.0 + mod_ref[0, 0, 1:2, :]) + mod_ref[0, 0, 0:1, :]).astype(BF16)

    def proj(c0, c1):
        return _dot_nt(h, w_ref[c0:c1, :])

    r = proj(ROW_CQ, ROW_KR + LANES)
    tail = _rope(r[:, ROW_KR - ROW_CQ:], rm_ref)
    _mla_project(r[:, 0:MLA_Q_RANK], r[:, MLA_Q_RANK:MLA_Q_RANK + MLA_KV_RANK], tail,
                 wq_ref, wk_ref, wmvt_ref, qn_ref, kvn_ref, rmq_ref, mq_ref, mk_ref, mkr_ref, mvt_ref)

    r = proj(ROW_GA, ROW_QS)
    g = r[:, 0:ROW_GKF - ROW_GA]
    ga_ref[0] = (g * jax.nn.sigmoid(g)).astype(BF16)
    _gla_decays(proj(ROW_QA, ROW_VA), r[:, ROW_GKF - ROW_GA:], wg_ref, bg_ref, sum_ref,
                qd_ref, ki_ref, tot_ref)
    va_ref[0] = proj(ROW_VA, ROW_GA).astype(BF16)

    r = proj(ROW_QS, ROW_KS)
    for j in range(1024 // LANES):
        sl = slice(j * LANES, (j + 1) * LANES)
        qs_ref[0, :, sl] = _rope(r[:, sl], rq_ref).astype(BF16)

    kk = _rope(proj(ROW_KS, ROW_VS), rk_ref)
    kx = pltpu.roll(kk, 64, 1)
    lo = lax.broadcasted_iota(jnp.int32, kk.shape, 1) < 64
    zero = jnp.zeros_like(kk)
    blocks = (jnp.where(lo, kk, zero), jnp.where(lo, zero, kx),
              jnp.where(lo, kx, zero), jnp.where(lo, zero, kk))
    for j, blk in enumerate(blocks):
        kx_ref[0, :, j * LANES:(j + 1) * LANES] = blk.astype(BF16)

    vt = _dot_nt(w_ref[ROW_VS:ROW_CQ, :], h).astype(BF16)
    hd = SWA_HEAD_DIM
    ones = jnp.ones((SWA_VT_ROWS - hd, vt.shape[1]), BF16)
    for g in range(SWA_KV_HEADS):
        vx_ref[0, g * SWA_VT_ROWS:g * SWA_VT_ROWS + hd, :] = vt[g * hd:(g + 1) * hd, :]
        vx_ref[0, g * SWA_VT_ROWS + hd:(g + 1) * SWA_VT_ROWS, :] = ones

    for j in range(3):
        r = proj(ROW_MG + j * 1024, ROW_MG + (j + 1) * 1024)
        mg_ref[0, :, j * 1024:(j + 1) * 1024] = jax.nn.sigmoid(r).astype(BF16)


def _inproj(xs, modtab, gain, w, rq, rk, rm, mla_weights, rmq, gla_weights, n_lat):
    (xl, xc), x_specs = _residual_inputs(xs, n_lat)
    b, _, d = xl.shape
    tm = TOKEN_TILE
    nt = n_lat + 1
    s = nt * tm
    row = lambda bi, i: (bi, i, 0)
    tab = pl.BlockSpec((3, tm, LANES), lambda bi, i: (0, i, 0))
    vx_rows = SWA_KV_HEADS * SWA_VT_ROWS
    gla_w = 2 * GLA_HEADS * GLA_DK
    rows_out = lambda n, dt: (pl.BlockSpec((1, tm, n), row), jax.ShapeDtypeStruct((b, s, n), dt))
    outs = [rows_out(gla_w, BF16), rows_out(gla_w, BF16),
            (pl.BlockSpec((1, 1, 8, gla_w), lambda bi, i: (bi, i, 0, 0)),
             jax.ShapeDtypeStruct((b, nt, 8, gla_w), F32)),
            rows_out(1024, BF16), rows_out(1024, BF16), rows_out(1024, BF16),
            rows_out(SWA_KV_HEADS * 2 * LANES, BF16),
            (pl.BlockSpec((1, vx_rows, tm), lambda bi, i: (bi, 0, i)),
             jax.ShapeDtypeStruct((b, vx_rows, s), BF16)),
            rows_out(3072, BF16),
            rows_out(MLA_HEADS * MLA_QK_PAD, BF16), rows_out(MLA_HEADS * MLA_NOPE, BF16),
            rows_out(LANES, BF16),
            (pl.BlockSpec((1, MLA_HEADS, MLA_VT_ROWS, tm), lambda bi, i: (bi, 0, 0, i)),
             jax.ShapeDtypeStruct((b, MLA_HEADS, MLA_VT_ROWS, s), BF16))]
    w, w_spec = _weight_spec(w)
    return pl.pallas_call(
        functools.partial(_inproj_kernel, n_lat=n_lat),
        grid=(b, nt),
        in_specs=x_specs + [
            pl.BlockSpec((1, 1, 8, d), lambda bi, i: (bi, i // n_lat, 0, 0)),
            _const_spec((1, d)),
            w_spec,
            tab, tab, tab,
        ] + [_const_spec(a.shape) for a in mla_weights] + [tab]
        + [_const_spec(a.shape) for a in gla_weights],
        out_specs=[o[0] for o in outs],
        out_shape=[o[1] for o in outs],
        compiler_params=_cparams(2),
        name="inproj",
    )(xl, xc, modtab, gain, w, rq, rk, rm, *mla_weights, rmq, *gla_weights)


def _gla_kernel(qdf_ref, qdb_ref, kif_ref, kib_ref, v_ref, g_ref, totf_ref, totb_ref, gn_ref,
                sel_ref, o_ref,
                of_ref, ob_ref, af_ref, ab_ref, stf_ref, stb_ref, decf_ref, decb_ref,
                keepf_ref, keepb_ref, *, n_lat):
    c = GLA_CHUNK
    tm = TOKEN_TILE
    per = tm // c

    ri = lax.broadcasted_iota(jnp.int32, (tm, tm), 0)
    ci = lax.broadcasted_iota(jnp.int32, (tm, tm), 1)
    same = (ri // c) == (ci // c)
    keepf_ref[...] = jnp.where(same & (ci <= ri), 1.0, 0.0)
    keepb_ref[...] = jnp.where(same & (ci > ri), 1.0, 0.0)

    def tile_rows(t):
        return pl.ds(pl.multiple_of(t * tm, tm), tm)

    def in_chunk(t, qd_ref, ki_ref, tot_ref, keep_ref, a_ref, dec_ref, out_ref):
        rows = tile_rows(t)
        tot = tot_ref[0, t]
        hi = tot.astype(BF16)
        lo = (tot - hi.astype(F32)).astype(BF16)
        dec_ref[t] = jnp.exp(_dot_tn(jnp.concatenate([hi, lo], axis=0), sel_ref[...]))
        v = v_ref[0, rows, :]
        q_dec = qd_ref[0, rows, :]
        k_inv = ki_ref[0, rows, :]
        for j in range(per):
            a_ref[t * per + j] = _dot_tn(k_inv[j * c:(j + 1) * c], v[j * c:(j + 1) * c])
        scores = jnp.where(keep_ref[...] > 0.5, _dot_nt(q_dec, k_inv), 0.0).astype(BF16)
        out_ref[rows, :] = _dot(scores, v)

    def bulk_body(t, carry):
        in_chunk(t, qdf_ref, kif_ref, totf_ref, keepf_ref, af_ref, decf_ref, of_ref)
        in_chunk(t, qdb_ref, kib_ref, totb_ref, keepb_ref, ab_ref, decb_ref, ob_ref)
        return carry

    lax.fori_loop(0, n_lat + 1, bulk_body, 0, unroll=3)

    def scan_tile(t, order, state, a_ref, dec_ref, st_ref):
        for j in order:
            st_ref[t * per + j] = state.astype(BF16)
            dec = dec_ref[t, :, j * LANES:(j + 1) * LANES]
            state = jnp.concatenate([dec] * (GLA_DV // LANES), axis=1) * (state + a_ref[t * per + j])
        return state

    fwd_order = range(per)
    bwd_order = range(per - 1, -1, -1)
    zero = jnp.zeros((GLA_DK, GLA_DV), F32)
    s_f = scan_tile(n_lat, fwd_order, zero, af_ref, decf_ref, stf_ref)
    s_b = scan_tile(n_lat, bwd_order, zero, ab_ref, decb_ref, stb_ref)
    lax.fori_loop(0, n_lat, lambda i, s: scan_tile(
        i, fwd_order, s, af_ref, decf_ref, stf_ref), s_f)
    lax.fori_loop(0, n_lat, lambda i, s: scan_tile(
        n_lat - 1 - i, bwd_order, s, ab_ref, decb_ref, stb_ref), s_b)

    def post_body(t, carry):
        for j in range(per):
            rows = pl.ds(pl.multiple_of(t * tm + j * c, c), c)
            o = of_ref[rows, :] + ob_ref[rows, :]
            o += _dot(qdf_ref[0, rows, :], stf_ref[t * per + j])
            o += _dot(qdb_ref[0, rows, :], stb_ref[t * per + j])
            y = _rms(o) * gn_ref[...]
            o_ref[0, rows, :] = (y * g_ref[0, rows, :].astype(F32)).astype(BF16)
        return carry

    lax.fori_loop(0, n_lat + 1, post_body, 0, unroll=3)


def _gla_sums():
    tm, c = TOKEN_TILE, GLA_CHUNK
    r = np.arange(tm)[:, None]
    col = np.arange(tm)[None, :]
    same = (r // c) == (col // c)
    return jnp.asarray(np.stack([same & (col <= r), same & (col >= r)]).astype(np.float32), BF16)


def _gla_selector():
    per = TOKEN_TILE // GLA_CHUNK
    sel = np.zeros((16, per * LANES), np.float32)
    for j in range(per):
        sel[j, j * LANES:(j + 1) * LANES] = 1.0
        sel[8 + j, j * LANES:(j + 1) * LANES] = 1.0
    return jnp.asarray(sel, BF16)


def _gla(qd, ki, tot, va, ga, gn, t_lat):
    b, s, _ = va.shape
    tm = TOKEN_TILE
    n_lat = t_lat // tm
    n_tiles = s // tm
    n_chunks = s // GLA_CHUNK
    h = GLA_HEADS
    sel = _gla_selector()
    fwd = lambda bi, hi: (bi, 0, hi)
    bwd = lambda bi, hi: (bi, 0, h + hi)
    return pl.pallas_call(
        functools.partial(_gla_kernel, n_lat=n_lat),
        grid=(b, h),
        in_specs=[
            pl.BlockSpec((1, s, GLA_DK), fwd), pl.BlockSpec((1, s, GLA_DK), bwd),
            pl.BlockSpec((1, s, GLA_DK), fwd), pl.BlockSpec((1, s, GLA_DK), bwd),
            pl.BlockSpec((1, s, GLA_DV), fwd), pl.BlockSpec((1, s, GLA_DV), fwd),
            pl.BlockSpec((1, n_tiles, 8, GLA_DK), lambda bi, hi: (bi, 0, 0, hi)),
            pl.BlockSpec((1, n_tiles, 8, GLA_DK), lambda bi, hi: (bi, 0, 0, h + hi)),
            pl.BlockSpec((1, GLA_DV), lambda bi, hi: (0, 0)),
            _const_spec(sel.shape),
        ],
        out_specs=pl.BlockSpec((1, s, GLA_DV), fwd),
        out_shape=jax.ShapeDtypeStruct((b, s, h * GLA_DV), BF16),
        scratch_shapes=[
            pltpu.VMEM((s, GLA_DV), F32), pltpu.VMEM((s, GLA_DV), F32),
            pltpu.VMEM((n_chunks, GLA_DK, GLA_DV), F32), pltpu.VMEM((n_chunks, GLA_DK, GLA_DV), F32),
            pltpu.VMEM((n_chunks, GLA_DK, GLA_DV), BF16), pltpu.VMEM((n_chunks, GLA_DK, GLA_DV), BF16),
            pltpu.VMEM((n_tiles, GLA_DK, tm // GLA_CHUNK * LANES), F32),
            pltpu.VMEM((n_tiles, GLA_DK, tm // GLA_CHUNK * LANES), F32),
            pltpu.VMEM((tm, tm), F32), pltpu.VMEM((tm, tm), F32),
        ],
        compiler_params=_cparams(2),
        name="gla",
    )(qd, qd, ki, ki, va, ga, tot, tot, gn, sel)


def _swa_kernel(sink_ref, q_ref, kx_ref, vx_ref, bias_ref, o_ref, s0_ref, s1_ref, m0_ref, m1_ref,
                *, n_lat, t_lat, with_ctx):
    tq = TOKEN_TILE
    span = tq + 2 * WINDOW
    hd = SWA_HEAD_DIM
    g = pl.program_id(1)
    ka, kb = slice(0, LANES), slice(LANES, 2 * LANES)

    def q_rows(i):
        return pl.ds(pl.multiple_of(i * tq, tq), tq)

    def sink_of(head):
        return sink_ref[g * SWA_GROUP + head] * LOG2_E

    def keys(i):
        ws = pl.multiple_of(jnp.clip(i * tq - WINDOW, 0, t_lat - span), WINDOW)
        win = pl.ds(ws, span)
        kc = [jnp.concatenate([kx_ref[0, win, c], kx_ref[0, t_lat:, c]], axis=0) for c in (ka, kb)]
        vc = jnp.concatenate([vx_ref[0, :, win], vx_ref[0, :, t_lat:]], axis=1)
        return kc, vc, (i * tq - ws) // WINDOW

    def scores(i, kc, bias_idx, head, s_ref, m_ref):
        qp = q_ref[0, q_rows(i), (head // 2) * LANES:(head // 2 + 1) * LANES]
        s_t = _dot_nt(kc[head % 2], qp)
        top = s_t[0:span] + bias_ref[bias_idx]
        bot = s_t[span:]
        s_ref[0:span, :] = top
        s_ref[span:, :] = bot
        m = jnp.maximum(jnp.max(top, axis=0, keepdims=True), jnp.max(bot, axis=0, keepdims=True))
        m_ref[...] = jnp.broadcast_to(jnp.maximum(m, sink_of(head)), m_ref.shape)

    def normalise(acc, m, head):
        return acc[0:hd] / (acc[hd:hd + 1] + jnp.exp2(sink_of(head) - m))

    def values(vc, head, s_ref, m_ref):
        m = m_ref[0:1, :]
        p_t = jnp.exp2(s_ref[...] - m).astype(BF16)
        return normalise(_dot(vc, p_t), m, head)

    def store(i, pair, o_a, o_b):
        o_t = jnp.concatenate([o_a, o_b], axis=0)
        o_ref[0, q_rows(i), pair * LANES:(pair + 1) * LANES] = o_t.T.astype(BF16)

    def scores_all(i, s_ref, m_ref):
        kc, _, bias_idx = keys(i)
        for head in range(SWA_GROUP):
            scores(i, kc, bias_idx, head, s_ref.at[head], m_ref.at[head])

    def values_all(i, s_ref, m_ref):
        _, vc, _ = keys(i)
        for pair in range(SWA_GROUP // 2):
            a, b = 2 * pair, 2 * pair + 1
            store(i, pair, values(vc, a, s_ref.at[a], m_ref.at[a]),
                  values(vc, b, s_ref.at[b], m_ref.at[b]))

    scores_all(0, s0_ref, m0_ref)

    def body(j, carry):
        scores_all(2 * j + 1, s1_ref, m1_ref)
        values_all(2 * j, s0_ref, m0_ref)
        scores_all(2 * j + 2, s0_ref, m0_ref)
        values_all(2 * j + 1, s1_ref, m1_ref)
        return carry

    lax.fori_loop(0, n_lat // 2 - 1, body, 0)
    scores_all(n_lat - 1, s1_ref, m1_ref)
    values_all(n_lat - 2, s0_ref, m0_ref)
    values_all(n_lat - 1, s1_ref, m1_ref)

    if with_ctx:
        rows = slice(t_lat, t_lat + tq)
        for pair in range(SWA_GROUP // 2):
            qp = q_ref[0, rows, pair * LANES:(pair + 1) * LANES]
            outs = []
            for head, kcols in ((2 * pair, ka), (2 * pair + 1, kb)):
                s_t = _dot_nt(kx_ref[0, t_lat:, kcols], qp)
                m = jnp.maximum(jnp.max(s_t, axis=0, keepdims=True), sink_of(head))
                p_t = jnp.exp2(s_t - m).astype(BF16)
                outs.append(normalise(_dot(vx_ref[0, :, t_lat:], p_t), m, head))
            o_t = jnp.concatenate(outs, axis=0)
            o_ref[0, rows, pair * LANES:(pair + 1) * LANES] = o_t.T.astype(BF16)


def _swa_bias():
    tq, span = TOKEN_TILE, TOKEN_TILE + 2 * WINDOW
    r = np.arange(span)[:, None]
    c = np.arange(tq)[None, :]
    tabs = [np.where(np.abs(r - off - c) <= WINDOW, 0.0, -1e30) for off in (0, WINDOW, 2 * WINDOW)]
    return jnp.asarray(np.stack(tabs).astype(np.float32))


def _swa(sinks, qs, kx, vx, t_lat, n_tiles):
    b, s, _ = qs.shape
    tq = TOKEN_TILE
    gw = SWA_GROUP * SWA_HEAD_DIM
    n_keys = tq + 2 * WINDOW + (s - t_lat)
    bias = _swa_bias()
    n_lat = t_lat // tq
    return pl.pallas_call(
        functools.partial(_swa_kernel, n_lat=n_lat, t_lat=t_lat, with_ctx=n_tiles > n_lat),
        grid=(b, SWA_KV_HEADS),
        in_specs=[
            pl.BlockSpec(memory_space=pltpu.SMEM),
            pl.BlockSpec((1, s, gw), lambda bi, gi: (bi, 0, gi)),
            pl.BlockSpec((1, s, 2 * LANES), lambda bi, gi: (bi, 0, gi)),
            pl.BlockSpec((1, SWA_VT_ROWS, s), lambda bi, gi: (bi, gi, 0)),
            _const_spec(bias.shape),
        ],
        out_specs=pl.BlockSpec((1, n_tiles * tq, gw), lambda bi, gi: (bi, 0, gi)),
        out_shape=jax.ShapeDtypeStruct((b, n_tiles * tq, SWA_HEADS * SWA_HEAD_DIM), BF16),
        scratch_shapes=[pltpu.VMEM((SWA_GROUP, n_keys, tq), F32), pltpu.VMEM((SWA_GROUP, n_keys, tq), F32),
                        pltpu.VMEM((SWA_GROUP, 8, tq), F32), pltpu.VMEM((SWA_GROUP, 8, tq), F32)],
        compiler_params=_cparams(2),
        name="swa",
    )(sinks, qs, kx, vx, bias)


def _mla_attn_kernel(q_ref, kn_ref, kr_ref, vt_ref, o_ref, k_ref, s0_ref, s1_ref, m0_ref, m1_ref,
                     *, n_lat, t_lat, with_ctx):
    tq = TOKEN_TILE
    k_ref[:, 0:MLA_NOPE] = kn_ref[0]
    k_ref[:, MLA_NOPE:] = kr_ref[0]

    def q_rows(i):
        return pl.ds(pl.multiple_of(i * tq, tq), tq)

    def finish(acc, rows):
        o_t = acc[0:MLA_V] / acc[MLA_V:MLA_V + 1]
        o_ref[0, rows, :] = o_t.T.astype(BF16)

    def scores(i, s_ref, m_ref):
        q = q_ref[0, q_rows(i), :]
        half = k_ref.shape[0] // 2
        lo = _dot_nt(k_ref[0:half, :], q)
        hi = _dot_nt(k_ref[half:, :], q)
        s_ref[0:half, :] = lo
        s_ref[half:, :] = hi
        m = jnp.maximum(jnp.max(lo, axis=0, keepdims=True), jnp.max(hi, axis=0, keepdims=True))
        m_ref[...] = jnp.broadcast_to(m, m_ref.shape)

    def values(i, s_ref, m_ref):
        p_t = jnp.exp2(s_ref[...] - m_ref[0:1, :]).astype(BF16)
        finish(_dot(vt_ref[0, 0], p_t), q_rows(i))

    scores(0, s0_ref, m0_ref)

    def body(j, carry):
        scores(2 * j + 1, s1_ref, m1_ref)
        values(2 * j, s0_ref, m0_ref)
        scores(2 * j + 2, s0_ref, m0_ref)
        values(2 * j + 1, s1_ref, m1_ref)
        return carry

    lax.fori_loop(0, n_lat // 2 - 1, body, 0)
    scores(n_lat - 1, s1_ref, m1_ref)
    values(n_lat - 2, s0_ref, m0_ref)
    values(n_lat - 1, s1_ref, m1_ref)

    if with_ctx:
        s_t = _dot_nt(k_ref[t_lat:, :], q_ref[0, t_lat:, :])
        p_t = jnp.exp2(s_t - jnp.max(s_t, axis=0, keepdims=True)).astype(BF16)
        finish(_dot(vt_ref[0, 0, :, t_lat:], p_t), slice(t_lat, t_lat + tq))


def _mla_attn(q, kn, kr, vt, t_lat, n_tiles):
    b, s, _ = q.shape
    tq = TOKEN_TILE
    n_lat = t_lat // tq
    return pl.pallas_call(
        functools.partial(_mla_attn_kernel, n_lat=n_lat, t_lat=t_lat, with_ctx=n_tiles > n_lat),
        grid=(b, MLA_HEADS),
        in_specs=[
            pl.BlockSpec((1, s, MLA_QK_PAD), lambda bi, hi: (bi, 0, hi)),
            pl.BlockSpec((1, s, MLA_NOPE), lambda bi, hi: (bi, 0, hi)),
            pl.BlockSpec((1, s, LANES), lambda bi, hi: (bi, 0, 0)),
            pl.BlockSpec((1, 1, MLA_VT_ROWS, s), lambda bi, hi: (bi, hi, 0, 0)),
        ],
        out_specs=pl.BlockSpec((1, n_tiles * tq, MLA_V), lambda bi, hi: (bi, 0, hi)),
        out_shape=jax.ShapeDtypeStruct((b, n_tiles * tq, MLA_HEADS * MLA_V), BF16),
        scratch_shapes=[pltpu.VMEM((s, MLA_QK_PAD), BF16),
                        pltpu.VMEM((s, tq), F32), pltpu.VMEM((s, tq), F32),
                        pltpu.VMEM((8, tq), F32), pltpu.VMEM((8, tq), F32)],
        compiler_params=_cparams(2),
        name="mla_attn",
    )(q, kn, kr, vt)


def _residual_inputs(xs, n_lat):
    tm = TOKEN_TILE
    if isinstance(xs, tuple):
        lat, ctx, ctx_block = xs[0], xs[1], 0
    else:
        lat, ctx, ctx_block = xs, xs, n_lat
    d = lat.shape[-1]
    specs = [pl.BlockSpec((1, tm, d), lambda bi, i: (bi, jnp.minimum(i, n_lat - 1), 0)),
             pl.BlockSpec((1, tm, d), lambda bi, i: (bi, ctx_block, 0))]
    return (lat, ctx), specs


def _pick_tile(xl_ref, xc_ref, n_lat):
    return jnp.where(pl.program_id(1) < n_lat, xl_ref[0], xc_ref[0])


def _mix_ffn_kernel(xl_ref, xc_ref, ya_ref, yb_ref, yc_ref, mg_ref, mod_ref,
                    wa_ref, wb_ref, wc_ref, wo_ref, gain_ref, wi_ref, wd_ref, fin_ref, o_ref,
                    *, n_lat, final):
    d = D_MODEL
    u = mg_ref[0, :, 0:d].astype(F32) * _dot(ya_ref[0], wa_ref[...])
    u += mg_ref[0, :, d:2 * d].astype(F32) * _dot(yb_ref[0], wb_ref[...])
    u += mg_ref[0, :, 2 * d:3 * d].astype(F32) * _dot(yc_ref[0], wc_ref[...])
    x = _pick_tile(xl_ref, xc_ref, n_lat) + mod_ref[0, 0, 2:3, :] * _dot(u.astype(BF16), wo_ref[...])

    y = _rms(x) * gain_ref[...]
    h = (y * (1.0 + mod_ref[0, 0, 4:5, :]) + mod_ref[0, 0, 3:4, :]).astype(BF16)
    gate = _dot(h, wi_ref[:, 0:D_FF])
    up = _dot(h, wi_ref[:, D_FF:2 * D_FF])
    a = (gate * jax.nn.sigmoid(gate) * up).astype(BF16)
    x = x + mod_ref[0, 0, 5:6, :] * _dot(a, wd_ref[...])
    if final:
        x = _rms(x) * fin_ref[...]
    o_ref[0] = x


def _mix_ffn(xs, ya, yb, yc, mg, modtab, wa, wb, wc, wo, gain, wi, wd, fin, n_lat, n_tiles, final):
    (xl, xc), x_specs = _residual_inputs(xs, n_lat)
    b, _, d = xl.shape
    tm = TOKEN_TILE
    row = lambda bi, i: (bi, i, 0)
    weights, w_specs = zip(*[_weight_spec(w) for w in (wa, wb, wc, wo, gain, wi, wd, fin)])
    return pl.pallas_call(
        functools.partial(_mix_ffn_kernel, n_lat=n_lat, final=final),
        grid=(b, n_tiles),
        in_specs=x_specs + [
            pl.BlockSpec((1, tm, d), row), pl.BlockSpec((1, tm, d), row),
            pl.BlockSpec((1, tm, d), row), pl.BlockSpec((1, tm, 3 * d), row),
            pl.BlockSpec((1, 1, 8, d), lambda bi, i: (bi, i // n_lat, 0, 0)),
        ] + list(w_specs),
        out_specs=pl.BlockSpec((1, tm, d), row),
        out_shape=jax.ShapeDtypeStruct((b, n_tiles * tm, d), F32),
        compiler_params=_cparams(2),
        name="mix_ffn",
    )(xl, xc, ya, yb, yc, mg, modtab, *weights)


def _rope_tables(t_lat, s_tot):
    half = SWA_HEAD_DIM // 4
    inv = np.power(ROPE_BASE, -np.arange(half, dtype=np.float32) / half).astype(np.float32)
    pos = np.arange(t_lat)
    lane = np.arange(SWA_HEAD_DIM)
    p = np.where(lane[None, :] < 2 * half, (pos // GRID_W)[:, None], (pos % GRID_W)[:, None])
    ang = p.astype(np.float32) * inv[lane % half][None, :]
    cos, sin = np.cos(ang), np.sin(ang)
    upper = (lane % (2 * half)) >= half
    tabs = np.stack([cos, np.where(upper, sin, 0.0), np.where(upper, 0.0, -sin)]).astype(np.float32)
    ident = np.zeros((3, s_tot - t_lat, SWA_HEAD_DIM), np.float32)
    ident[0] = 1.0
    return np.concatenate([tabs, ident], axis=1)


def kernel(x, c, ctx, c_ctx, w_mod, b_mod, norm_mix, w_in, w_gk_fwd, b_gk_fwd, w_gk_bwd, b_gk_bwd,
           gla_norm, sinks, q_norm, w_q_up, kv_norm, w_kv_up, w_pa, w_pb, w_pc, w_o,
           norm_ffn, w_ffn_in, w_ffn_out, final_norm):
    b, t_lat, d = x.shape
    l_ctx = ctx.shape[1]
    s_tot = t_lat + l_ctx
    depth = w_mod.shape[0]
    tm = TOKEN_TILE
    assert t_lat % tm == 0 and l_ctx == tm and t_lat % GRID_W == 0 and d == D_MODEL
    n_lat = t_lat // tm
    n_all = s_tot // tm

    cc = jnp.concatenate([c, c_ctx[None, :], jnp.zeros((16 - b - 1, d), F32)], axis=0)
    mod = _modulation(cc, w_mod, b_mod)
    mod_lat = mod[:, :b].reshape(depth, b, 1, 6, d)
    mod_ctx = jnp.broadcast_to(mod[:, b].reshape(depth, 1, 1, 6, d), (depth, b, 1, 6, d))
    modtab = jnp.pad(jnp.concatenate([mod_lat, mod_ctx], axis=2),
                     ((0, 0), (0, 0), (0, 0), (0, 2), (0, 0)))

    t64 = _rope_tables(t_lat, s_tot)
    ident = np.zeros_like(t64)
    ident[0] = 1.0
    rk = np.concatenate([t64, t64], axis=2)
    rq = rk * np.float32(SWA_HEAD_DIM ** -0.5 * LOG2_E)
    rm = np.concatenate([t64, ident], axis=2)
    rmq = rm * np.float32((MLA_NOPE + MLA_ROPE) ** -0.5 * LOG2_E)
    rq, rk, rm, rmq = (jnp.asarray(a) for a in (rq, rk, rm, rmq))

    w_in_t = jnp.swapaxes(w_in, 1, 2).astype(BF16)
    w_pa, w_pb, w_pc, w_o, w_ffn_in, w_ffn_out = (
        w.astype(BF16) for w in (w_pa, w_pb, w_pc, w_o, w_ffn_in, w_ffn_out))

    gla_sums = _gla_sums()

    xs = (x, ctx)
    for l in range(depth):
        last = l == depth - 1
        n_out = n_lat if last else n_all
        hk = GLA_HEADS * GLA_DK
        wg = jnp.zeros((2 * GLA_GATE_RANK, 2 * hk), F32)
        wg = wg.at[0:GLA_GATE_RANK, 0:hk].set(w_gk_fwd[l])
        wg = wg.at[GLA_GATE_RANK:, hk:].set(w_gk_bwd[l])
        bg = jnp.concatenate([b_gk_fwd[l], b_gk_bwd[l]]).reshape(1, 2 * hk)
        gla_weights = (wg.astype(BF16), bg, gla_sums)
        wq = jnp.pad(w_q_up[l].reshape(MLA_Q_RANK, MLA_HEADS, MLA_NOPE + MLA_ROPE),
                     ((0, 0), (0, 0), (0, MLA_QK_PAD - MLA_NOPE - MLA_ROPE))
                     ).reshape(MLA_Q_RANK, MLA_HEADS * MLA_QK_PAD).astype(BF16)

        wkv = w_kv_up[l].reshape(MLA_KV_RANK, MLA_HEADS, MLA_NOPE + MLA_V)
        wk = wkv[:, :, :MLA_NOPE].reshape(MLA_KV_RANK, MLA_HEADS * MLA_NOPE).astype(BF16)
        wvt = wkv[:, :, MLA_NOPE:].reshape(MLA_KV_RANK, MLA_HEADS * MLA_V).T.astype(BF16)
        mla_weights = (wq, wk, wvt, q_norm[l].reshape(1, -1), kv_norm[l].reshape(1, -1))

        qd, ki, tot, va, ga, qs, kx, vx, mg, qm, kn, kr, vm = _inproj(
            xs, modtab[l], norm_mix[l].reshape(1, d), (w_in_t, l), rq, rk, rm,
            mla_weights, rmq, gla_weights, n_lat)
        ya = _gla(qd, ki, tot, va, ga, gla_norm[l].reshape(1, GLA_DV), t_lat)
        yb = _swa(sinks[l], qs, kx, vx, t_lat, n_out)
        yc = _mla_attn(qm, kn, kr, vm, t_lat, n_out)
        xs = _mix_ffn(xs, ya, yb, yc, mg, modtab[l], (w_pa, l), (w_pb, l), (w_pc, l), (w_o, l),
                      norm_ffn[l].reshape(1, d), (w_ffn_in, l), (w_ffn_out, l),
                      final_norm.reshape(1, d), n_lat, n_out, last)
    return xs
```

```python
import functools

import jax
import jax.numpy as jnp
import numpy as np
from jax import lax
from jax.experimental import pallas as pl
from jax.experimental.pallas import tpu as pltpu

F32 = jnp.float32
BF16 = jnp.bfloat16

D_MODEL = 1024
GRID_W = 64
EPS = 1e-6
ROPE_BASE = 10000.0

GLA_HEADS = 4
GLA_DK = 128
GLA_DV = 256
GLA_GATE_RANK = 16
GLA_GATE_NORM = 16.0
GLA_CHUNK = 64

SWA_HEADS = 16
SWA_KV_HEADS = 2
SWA_GROUP = SWA_HEADS // SWA_KV_HEADS
SWA_HEAD_DIM = 64
WINDOW = 128
SWA_VT_ROWS = SWA_HEAD_DIM + 16

MLA_HEADS = 8
MLA_Q_RANK = 384
MLA_KV_RANK = 256
MLA_NOPE = 128
MLA_ROPE = 64
MLA_V = 128
MLA_QK_PAD = 256
MLA_VT_ROWS = MLA_V + 16
LOG2_E = 1.4426950408889634

D_FF = -(-(8 * D_MODEL) // (3 * 256)) * 256

IN_SPLITS = (
    GLA_HEADS * GLA_DK, GLA_HEADS * GLA_DK, GLA_HEADS * GLA_DV, GLA_HEADS * GLA_DV,
    GLA_GATE_RANK, GLA_GATE_RANK,
    SWA_HEADS * SWA_HEAD_DIM, SWA_KV_HEADS * SWA_HEAD_DIM, SWA_KV_HEADS * SWA_HEAD_DIM,
    MLA_Q_RANK, MLA_KV_RANK, MLA_ROPE,
    3 * D_MODEL,
)

LANES = 128
TOKEN_TILE = 256
(ROW_QA, ROW_KA, ROW_VA, ROW_GA, ROW_GKF, ROW_GKB, ROW_QS, ROW_KS, ROW_VS,
 ROW_CQ, ROW_CKV, ROW_KR, ROW_MG, ROW_END) = (int(v) for v in np.cumsum((0,) + IN_SPLITS))
VMEM_LIMIT = 56 * 1024 * 1024


def _cparams(n_axes):
    return pltpu.CompilerParams(
        dimension_semantics=("arbitrary",) * n_axes, vmem_limit_bytes=VMEM_LIMIT)


def _const_spec(shape):
    nd = len(shape)
    return pl.BlockSpec(shape, lambda *_: (0,) * nd, pipeline_mode=pl.Buffered(1))


def _weight_spec(w):
    if isinstance(w, tuple):
        arr, layer = w
        nd = arr.ndim
        spec = pl.BlockSpec((None,) + arr.shape[1:], lambda *_: (layer,) + (0,) * (nd - 1),
                            pipeline_mode=pl.Buffered(1))
        return arr, spec
    return w, _const_spec(w.shape)


def _rope(x, tab_ref):
    return (x * tab_ref[0] + pltpu.roll(x, 16, 1) * tab_ref[1]
            + pltpu.roll(x, LANES - 16, 1) * tab_ref[2])


def _rms(x):
    return x * lax.rsqrt(jnp.mean(x * x, axis=-1, keepdims=True) + EPS)


def _dot(a, b):
    return jnp.dot(a, b, preferred_element_type=F32)


def _dot_nt(a, b):
    return lax.dot_general(a, b, (((1,), (1,)), ((), ())), preferred_element_type=F32)


def _dot_tn(a, b):
    return lax.dot_general(a, b, (((0,), (0,)), ((), ())), preferred_element_type=F32)


def _mod_kernel(c_ref, w_ref, b_ref, o_ref):
    c = c_ref[...]
    a = (c * jax.nn.sigmoid(c)).astype(BF16)
    o_ref[0] = _dot(a, w_ref[0].astype(BF16)) + b_ref[0]


def _modulation(cc, w_mod, b_mod):
    depth, d, n = w_mod.shape
    tn = 1536
    return pl.pallas_call(
        _mod_kernel,
        grid=(depth, n // tn),
        in_specs=[
            pl.BlockSpec(cc.shape, lambda l, j: (0, 0)),
            pl.BlockSpec((1, d, tn), lambda l, j: (l, 0, j)),
            pl.BlockSpec((1, 1, tn), lambda l, j: (l, 0, j)),
        ],
        out_specs=pl.BlockSpec((1, cc.shape[0], tn), lambda l, j: (l, 0, j)),
        out_shape=jax.ShapeDtypeStruct((depth, cc.shape[0], n), F32),
        compiler_params=_cparams(2),
        name="modulation",
    )(cc, w_mod, b_mod.reshape(depth, 1, n))


def _mla_project(cq, ckv, tail, wq_ref, wk_ref, wvt_ref, qn_ref, kvn_ref, rq_ref,
                 q_ref, k_ref, kr_ref, vt_ref):
    scale = (MLA_NOPE + MLA_ROPE) ** -0.5 * LOG2_E
    lo = lax.broadcasted_iota(jnp.int32, tail.shape, 1) < MLA_ROPE
    kr_ref[0] = jnp.where(lo, tail, 0.0).astype(BF16)

    qf = _dot((_rms(cq) * qn_ref[...]).astype(BF16), wq_ref[...])
    ckvn = (_rms(ckv) * kvn_ref[...]).astype(BF16)
    k_ref[0] = _dot(ckvn, wk_ref[...]).astype(BF16)
    vt = _dot_nt(wvt_ref[...], ckvn)
    ones = jnp.ones((MLA_VT_ROWS - MLA_V, vt.shape[1]), BF16)
    for h in range(MLA_HEADS):
        c0 = h * MLA_QK_PAD
        q_ref[0, :, c0:c0 + LANES] = (qf[:, c0:c0 + LANES] * scale).astype(BF16)
        q_ref[0, :, c0 + LANES:c0 + 2 * LANES] = _rope(
            qf[:, c0 + LANES:c0 + 2 * LANES], rq_ref).astype(BF16)
        vt_ref[0, h, 0:MLA_V, :] = vt[h * MLA_V:(h + 1) * MLA_V, :].astype(BF16)
        vt_ref[0, h, MLA_V:MLA_VT_ROWS, :] = ones


def _gla_decays(qk, gates, wg_ref, bg_ref, sum_ref, qd_ref, ki_ref, tot_ref):
    hk = GLA_HEADS * GLA_DK
    c = GLA_CHUNK
    q = qk[:, 0:hk] * (GLA_DK ** -0.5)
    k = qk[:, hk:]
    z = _dot(gates.astype(BF16), wg_ref[...]) + bg_ref[...]
    la = (jnp.minimum(z, 0.0) - jnp.log1p(jnp.exp(-jnp.abs(z)))) * (1.0 / GLA_GATE_NORM)
    hi = la.astype(BF16)
    lo = (la - hi.astype(F32)).astype(BF16)
    totals = []
    for d in range(2):
        sl = slice(d * hk, (d + 1) * hk)
        cr = _dot(sum_ref[d], jnp.concatenate([hi[:, sl], lo[:, sl]], axis=1))
        cum = cr[:, 0:hk] + cr[:, hk:]
        qd_ref[0, :, sl] = (q * jnp.exp(cum)).astype(BF16)
        ki_ref[0, :, sl] = (k * jnp.exp(-cum)).astype(BF16)
        last = (c - 1) if d == 0 else 0
        ends = [cum[j * c + last:j * c + last + 1] for j in range(TOKEN_TILE // c)]
        totals.append(jnp.concatenate(ends + [jnp.zeros((8 - len(ends), hk), F32)], axis=0))
    tot_ref[0, 0] = jnp.concatenate(totals, axis=1)


def _inproj_kernel(xl_ref, xc_ref, mod_ref, gain_ref, w_ref, rq_ref, rk_ref, rm_ref,
                   wq_ref, wk_ref, wmvt_ref, qn_ref, kvn_ref, rmq_ref, wg_ref, bg_ref, sum_ref,
                   qd_ref, ki_ref, tot_ref, va_ref, ga_ref, qs_ref, kx_ref, vx_ref, mg_ref,
                   mq_ref, mk_ref, mkr_ref, mvt_ref, *, n_lat):
    x = _pick_tile(xl_ref, xc_ref, n_lat)
    y = _rms(x) * gain_ref[...]
    h = (y * (1.0 + mod_ref[0, 0, 1:2, :]) + mod_ref[0, 0, 0:1, :]).astype(BF16)

    def proj(c0, c1):
        return _dot_nt(h, w_ref[c0:c1, :])

    r = proj(ROW_CQ, ROW_KR + LANES)
    tail = _rope(r[:, ROW_KR - ROW_CQ:], rm_ref)
    _mla_project(r[:, 0:MLA_Q_RANK], r[:, MLA_Q_RANK:MLA_Q_RANK + MLA_KV_RANK], tail,
                 wq_ref, wk_ref, wmvt_ref, qn_ref, kvn_ref, rmq_ref, mq_ref, mk_ref, mkr_ref, mvt_ref)

    r = proj(ROW_GA, ROW_QS)
    g = r[:, 0:ROW_GKF - ROW_GA]
    ga_ref[0] = (g * jax.nn.sigmoid(g)).astype(BF16)
    _gla_decays(proj(ROW_QA, ROW_VA), r[:, ROW_GKF - ROW_GA:], wg_ref, bg_ref, sum_ref,
                qd_ref, ki_ref, tot_ref)
    va_ref[0] = proj(ROW_VA, ROW_GA).astype(BF16)

    r = proj(ROW_QS, ROW_KS)
    for j in range(1024 // LANES):
        sl = slice(j * LANES, (j + 1) * LANES)
        qs_ref[0, :, sl] = _rope(r[:, sl], rq_ref).astype(BF16)

    kk = _rope(proj(ROW_KS, ROW_VS), rk_ref)
    kx = pltpu.roll(kk, 64, 1)
    lo = lax.broadcasted_iota(jnp.int32, kk.shape, 1) < 64
    zero = jnp.zeros_like(kk)
    blocks = (jnp.where(lo, kk, zero), jnp.where(lo, zero, kx),
              jnp.where(lo, kx, zero), jnp.where(lo, zero, kk))
    for j, blk in enumerate(blocks):
        kx_ref[0, :, j * LANES:(j + 1) * LANES] = blk.astype(BF16)

    vt = _dot_nt(w_ref[ROW_VS:ROW_CQ, :], h).astype(BF16)
    hd = SWA_HEAD_DIM
    ones = jnp.ones((SWA_VT_ROWS - hd, vt.shape[1]), BF16)
    for g in range(SWA_KV_HEADS):
        vx_ref[0, g * SWA_VT_ROWS:g * SWA_VT_ROWS + hd, :] = vt[g * hd:(g + 1) * hd, :]
        vx_ref[0, g * SWA_VT_ROWS + hd:(g + 1) * SWA_VT_ROWS, :] = ones

    for j in range(3):
        r = proj(ROW_MG + j * 1024, ROW_MG + (j + 1) * 1024)
        mg_ref[0, :, j * 1024:(j + 1) * 1024] = jax.nn.sigmoid(r).astype(BF16)


def _inproj(xs, modtab, gain, w, rq, rk, rm, mla_weights, rmq, gla_weights, n_lat):
    (xl, xc), x_specs = _residual_inputs(xs, n_lat)
    b, _, d = xl.shape
    tm = TOKEN_TILE
    nt = n_lat + 1
    s = nt * tm
    row = lambda bi, i: (bi, i, 0)
    tab = pl.BlockSpec((3, tm, LANES), lambda bi, i: (0, i, 0))
    vx_rows = SWA_KV_HEADS * SWA_VT_ROWS
    gla_w = 2 * GLA_HEADS * GLA_DK
    rows_out = lambda n, dt: (pl.BlockSpec((1, tm, n), row), jax.ShapeDtypeStruct((b, s, n), dt))
    outs = [rows_out(gla_w, BF16), rows_out(gla_w, BF16),
            (pl.BlockSpec((1, 1, 8, gla_w), lambda bi, i: (bi, i, 0, 0)),
             jax.ShapeDtypeStruct((b, nt, 8, gla_w), F32)),
            rows_out(1024, BF16), rows_out(1024, BF16), rows_out(1024, BF16),
            rows_out(SWA_KV_HEADS * 2 * LANES, BF16),
            (pl.BlockSpec((1, vx_rows, tm), lambda bi, i: (bi, 0, i)),
             jax.ShapeDtypeStruct((b, vx_rows, s), BF16)),
            rows_out(3072, BF16),
            rows_out(MLA_HEADS * MLA_QK_PAD, BF16), rows_out(MLA_HEADS * MLA_NOPE, BF16),
            rows_out(LANES, BF16),
            (pl.BlockSpec((1, MLA_HEADS, MLA_VT_ROWS, tm), lambda bi, i: (bi, 0, 0, i)),
             jax.ShapeDtypeStruct((b, MLA_HEADS, MLA_VT_ROWS, s), BF16))]
    w, w_spec = _weight_spec(w)
    return pl.pallas_call(
        functools.partial(_inproj_kernel, n_lat=n_lat),
        grid=(b, nt),
        in_specs=x_specs + [
            pl.BlockSpec((1, 1, 8, d), lambda bi, i: (bi, i // n_lat, 0, 0)),
            _const_spec((1, d)),
            w_spec,
            tab, tab, tab,
        ] + [_const_spec(a.shape) for a in mla_weights] + [tab]
        + [_const_spec(a.shape) for a in gla_weights],
        out_specs=[o[0] for o in outs],
        out_shape=[o[1] for o in outs],
        compiler_params=_cparams(2),
        name="inproj",
    )(xl, xc, modtab, gain, w, rq, rk, rm, *mla_weights, rmq, *gla_weights)


def _gla_kernel(qdf_ref, qdb_ref, kif_ref, kib_ref, v_ref, g_ref, totf_ref, totb_ref, gn_ref,
                sel_ref, o_ref,
                oi_ref, af_ref, ab_ref, st_ref, decf_ref, decb_ref, keepf_ref, keepb_ref, *, n_lat):
    c = GLA_CHUNK
    tm = TOKEN_TILE
    per = tm // c

    ri = lax.broadcasted_iota(jnp.int32, (tm, tm), 0)
    ci = lax.broadcasted_iota(jnp.int32, (tm, tm), 1)
    same = (ri // c) == (ci // c)
    keepf_ref[...] = jnp.where(same & (ci <= ri), 1.0, 0.0)
    keepb_ref[...] = jnp.where(same & (ci > ri), 1.0, 0.0)

    def tile_rows(t):
        return pl.ds(pl.multiple_of(t * tm, tm), tm)

    def in_chunk(t, qd_ref, ki_ref, tot_ref, a_ref, dec_ref):
        rows = tile_rows(t)
        tot = tot_ref[0, t]
        hi = tot.astype(BF16)
        lo = (tot - hi.astype(F32)).astype(BF16)
        dec_ref[t] = jnp.exp(_dot_tn(jnp.concatenate([hi, lo], axis=0), sel_ref[...]))
        v = v_ref[0, rows, :]
        q_dec = qd_ref[0, rows, :]
        k_inv = ki_ref[0, rows, :]
        for j in range(per):
            a_ref[t * per + j] = _dot_tn(k_inv[j * c:(j + 1) * c], v[j * c:(j + 1) * c])
        return _dot_nt(q_dec, k_inv)

    def bulk_body(t, carry):
        s_f = in_chunk(t, qdf_ref, kif_ref, totf_ref, af_ref, decf_ref)
        s_b = in_chunk(t, qdb_ref, kib_ref, totb_ref, ab_ref, decb_ref)
        scores = jnp.where(keepf_ref[...] > 0.5, s_f, jnp.where(keepb_ref[...] > 0.5, s_b, 0.0))
        oi_ref[tile_rows(t), :] = _dot(scores.astype(BF16), v_ref[0, tile_rows(t), :])
        return carry

    lax.fori_loop(0, n_lat + 1, bulk_body, 0, unroll=3)

    def scan_tile(t, order, state, a_ref, dec_ref, st_rows):
        for j in order:
            st_ref[t * per + j, st_rows, :] = state.astype(BF16)
            dec = dec_ref[t, :, j * LANES:(j + 1) * LANES]
            state = jnp.concatenate([dec] * (GLA_DV // LANES), axis=1) * (state + a_ref[t * per + j])
        return state

    fwd_order = range(per)
    bwd_order = range(per - 1, -1, -1)
    zero = jnp.zeros((GLA_DK, GLA_DV), F32)
    f_rows, b_rows = slice(0, GLA_DK), slice(GLA_DK, 2 * GLA_DK)
    s_f = scan_tile(n_lat, fwd_order, zero, af_ref, decf_ref, f_rows)
    s_b = scan_tile(n_lat, bwd_order, zero, ab_ref, decb_ref, b_rows)
    lax.fori_loop(0, n_lat, lambda i, s: scan_tile(
        i, fwd_order, s, af_ref, decf_ref, f_rows), s_f)
    lax.fori_loop(0, n_lat, lambda i, s: scan_tile(
        n_lat - 1 - i, bwd_order, s, ab_ref, decb_ref, b_rows), s_b)

    def post_body(t, carry):
        for j in range(per):
            rows = pl.ds(pl.multiple_of(t * tm + j * c, c), c)
            q_both = jnp.concatenate([qdf_ref[0, rows, :], qdb_ref[0, rows, :]], axis=1)
            o = oi_ref[rows, :] + _dot(q_both, st_ref[t * per + j])
            y = _rms(o) * gn_ref[...]
            o_ref[0, rows, :] = (y * g_ref[0, rows, :].astype(F32)).astype(BF16)
        return carry

    lax.fori_loop(0, n_lat + 1, post_body, 0, unroll=3)


def _gla_sums():
    tm, c = TOKEN_TILE, GLA_CHUNK
    r = np.arange(tm)[:, None]
    col = np.arange(tm)[None, :]
    same = (r // c) == (col // c)
    return jnp.asarray(np.stack([same & (col <= r), same & (col >= r)]).astype(np.float32), BF16)


def _gla_selector():
    per = TOKEN_TILE // GLA_CHUNK
    sel = np.zeros((16, per * LANES), np.float32)
    for j in range(per):
        sel[j, j * LANES:(j + 1) * LANES] = 1.0
        sel[8 + j, j * LANES:(j + 1) * LANES] = 1.0
    return jnp.asarray(sel, BF16)


def _gla(qd, ki, tot, va, ga, gn, t_lat):
    b, s, _ = va.shape
    tm = TOKEN_TILE
    n_lat = t_lat // tm
    n_tiles = s // tm
    n_chunks = s // GLA_CHUNK
    h = GLA_HEADS
    sel = _gla_selector()
    fwd = lambda bi, hi: (bi, 0, hi)
    bwd = lambda bi, hi: (bi, 0, h + hi)
    return pl.pallas_call(
        functools.partial(_gla_kernel, n_lat=n_lat),
        grid=(b, h),
        in_specs=[
            pl.BlockSpec((1, s, GLA_DK), fwd), pl.BlockSpec((1, s, GLA_DK), bwd),
            pl.BlockSpec((1, s, GLA_DK), fwd), pl.BlockSpec((1, s, GLA_DK), bwd),
            pl.BlockSpec((1, s, GLA_DV), fwd), pl.BlockSpec((1, s, GLA_DV), fwd),
            pl.BlockSpec((1, n_tiles, 8, GLA_DK), lambda bi, hi: (bi, 0, 0, hi)),
            pl.BlockSpec((1, n_tiles, 8, GLA_DK), lambda bi, hi: (bi, 0, 0, h + hi)),
            pl.BlockSpec((1, GLA_DV), lambda bi, hi: (0, 0)),
            _const_spec(sel.shape),
        ],
        out_specs=pl.BlockSpec((1, s, GLA_DV), fwd),
        out_shape=jax.ShapeDtypeStruct((b, s, h * GLA_DV), BF16),
        scratch_shapes=[
            pltpu.VMEM((s, GLA_DV), F32),
            pltpu.VMEM((n_chunks, GLA_DK, GLA_DV), F32), pltpu.VMEM((n_chunks, GLA_DK, GLA_DV), F32),
            pltpu.VMEM((n_chunks, 2 * GLA_DK, GLA_DV), BF16),
            pltpu.VMEM((n_tiles, GLA_DK, tm // GLA_CHUNK * LANES), F32),
            pltpu.VMEM((n_tiles, GLA_DK, tm // GLA_CHUNK * LANES), F32),
            pltpu.VMEM((tm, tm), F32), pltpu.VMEM((tm, tm), F32),
        ],
        compiler_params=_cparams(2),
        name="gla",
    )(qd, qd, ki, ki, va, ga, tot, tot, gn, sel)


def _swa_kernel(sink_ref, q_ref, kx_ref, vx_ref, bias_ref, o_ref, s0_ref, s1_ref, m0_ref, m1_ref,
                *, n_lat, t_lat, with_ctx):
    tq = TOKEN_TILE
    span = tq + 2 * WINDOW
    hd = SWA_HEAD_DIM
    g = pl.program_id(1)
    ka, kb = slice(0, LANES), slice(LANES, 2 * LANES)

    def q_rows(i):
        return pl.ds(pl.multiple_of(i * tq, tq), tq)

    def sink_of(head):
        return sink_ref[g * SWA_GROUP + head] * LOG2_E

    def keys(i):
        ws = pl.multiple_of(jnp.clip(i * tq - WINDOW, 0, t_lat - span), WINDOW)
        win = pl.ds(ws, span)
        kc = [jnp.concatenate([kx_ref[0, win, c], kx_ref[0, t_lat:, c]], axis=0) for c in (ka, kb)]
        vc = jnp.concatenate([vx_ref[0, :, win], vx_ref[0, :, t_lat:]], axis=1)
        return kc, vc, (i * tq - ws) // WINDOW

    def scores(i, kc, bias_idx, head, s_ref, m_ref):
        qp = q_ref[0, q_rows(i), (head // 2) * LANES:(head // 2 + 1) * LANES]
        s_t = _dot_nt(kc[head % 2], qp)
        top = s_t[0:span] + bias_ref[bias_idx]
        bot = s_t[span:]
        s_ref[0:span, :] = top
        s_ref[span:, :] = bot
        m = jnp.maximum(jnp.max(top, axis=0, keepdims=True), jnp.max(bot, axis=0, keepdims=True))
        m_ref[...] = jnp.broadcast_to(jnp.maximum(m, sink_of(head)), m_ref.shape)

    def normalise(acc, m, head):
        return acc[0:hd] / (acc[hd:hd + 1] + jnp.exp2(sink_of(head) - m))

    def values(vc, head, s_ref, m_ref):
        m = m_ref[0:1, :]
        p_t = jnp.exp2(s_ref[...] - m).astype(BF16)
        return normalise(_dot(vc, p_t), m, head)

    def store(i, pair, o_a, o_b):
        o_t = jnp.concatenate([o_a, o_b], axis=0)
        o_ref[0, q_rows(i), pair * LANES:(pair + 1) * LANES] = o_t.T.astype(BF16)

    def scores_all(i, s_ref, m_ref):
        kc, _, bias_idx = keys(i)
        for head in range(SWA_GROUP):
            scores(i, kc, bias_idx, head, s_ref.at[head], m_ref.at[head])

    def values_all(i, s_ref, m_ref):
        _, vc, _ = keys(i)
        for pair in range(SWA_GROUP // 2):
            a, b = 2 * pair, 2 * pair + 1
            store(i, pair, values(vc, a, s_ref.at[a], m_ref.at[a]),
                  values(vc, b, s_ref.at[b], m_ref.at[b]))

    scores_all(0, s0_ref, m0_ref)

    def body(j, carry):
        scores_all(2 * j + 1, s1_ref, m1_ref)
        values_all(2 * j, s0_ref, m0_ref)
        scores_all(2 * j + 2, s0_ref, m0_ref)
        values_all(2 * j + 1, s1_ref, m1_ref)
        return carry

    lax.fori_loop(0, n_lat // 2 - 1, body, 0)
    scores_all(n_lat - 1, s1_ref, m1_ref)
    values_all(n_lat - 2, s0_ref, m0_ref)
    values_all(n_lat - 1, s1_ref, m1_ref)

    if with_ctx:
        rows = slice(t_lat, t_lat + tq)
        for pair in range(SWA_GROUP // 2):
            qp = q_ref[0, rows, pair * LANES:(pair + 1) * LANES]
            outs = []
            for head, kcols in ((2 * pair, ka), (2 * pair + 1, kb)):
                s_t = _dot_nt(kx_ref[0, t_lat:, kcols], qp)
                m = jnp.maximum(jnp.max(s_t, axis=0, keepdims=True), sink_of(head))
                p_t = jnp.exp2(s_t - m).astype(BF16)
                outs.append(normalise(_dot(vx_ref[0, :, t_lat:], p_t), m, head))
            o_t = jnp.concatenate(outs, axis=0)
            o_ref[0, rows, pair * LANES:(pair + 1) * LANES] = o_t.T.astype(BF16)


def _swa_bias():
    tq, span = TOKEN_TILE, TOKEN_TILE + 2 * WINDOW
    r = np.arange(span)[:, None]
    c = np.arange(tq)[None, :]
    tabs = [np.where(np.abs(r - off - c) <= WINDOW, 0.0, -1e30) for off in (0, WINDOW, 2 * WINDOW)]
    return jnp.asarray(np.stack(tabs).astype(np.float32))


def _swa(sinks, qs, kx, vx, t_lat, n_tiles):
    b, s, _ = qs.shape
    tq = TOKEN_TILE
    gw = SWA_GROUP * SWA_HEAD_DIM
    n_keys = tq + 2 * WINDOW + (s - t_lat)
    bias = _swa_bias()
    n_lat = t_lat // tq
    return pl.pallas_call(
        functools.partial(_swa_kernel, n_lat=n_lat, t_lat=t_lat, with_ctx=n_tiles > n_lat),
        grid=(b, SWA_KV_HEADS),
        in_specs=[
            pl.BlockSpec(memory_space=pltpu.SMEM),
            pl.BlockSpec((1, s, gw), lambda bi, gi: (bi, 0, gi)),
            pl.BlockSpec((1, s, 2 * LANES), lambda bi, gi: (bi, 0, gi)),
            pl.BlockSpec((1, SWA_VT_ROWS, s), lambda bi, gi: (bi, gi, 0)),
            _const_spec(bias.shape),
        ],
        out_specs=pl.BlockSpec((1, n_tiles * tq, gw), lambda bi, gi: (bi, 0, gi)),
        out_shape=jax.ShapeDtypeStruct((b, n_tiles * tq, SWA_HEADS * SWA_HEAD_DIM), BF16),
        scratch_shapes=[pltpu.VMEM((SWA_GROUP, n_keys, tq), F32), pltpu.VMEM((SWA_GROUP, n_keys, tq), F32),
                        pltpu.VMEM((SWA_GROUP, 8, tq), F32), pltpu.VMEM((SWA_GROUP, 8, tq), F32)],
        compiler_params=_cparams(2),
        name="swa",
    )(sinks, qs, kx, vx, bias)


def _mla_attn_kernel(q_ref, kn_ref, kr_ref, vt_ref, o_ref, k_ref, s0_ref, s1_ref, m0_ref, m1_ref,
                     *, n_lat, t_lat, with_ctx):
    tq = TOKEN_TILE
    k_ref[:, 0:MLA_NOPE] = kn_ref[0]
    k_ref[:, MLA_NOPE:] = kr_ref[0]

    def q_rows(i):
        return pl.ds(pl.multiple_of(i * tq, tq), tq)

    def finish(acc, rows):
        o_t = acc[0:MLA_V] / acc[MLA_V:MLA_V + 1]
        o_ref[0, rows, :] = o_t.T.astype(BF16)

    def scores(i, s_ref, m_ref):
        q = q_ref[0, q_rows(i), :]
        half = k_ref.shape[0] // 2
        lo = _dot_nt(k_ref[0:half, :], q)
        hi = _dot_nt(k_ref[half:, :], q)
        s_ref[0:half, :] = lo
        s_ref[half:, :] = hi
        m = jnp.maximum(jnp.max(lo, axis=0, keepdims=True), jnp.max(hi, axis=0, keepdims=True))
        m_ref[...] = jnp.broadcast_to(m, m_ref.shape)

    def values(i, s_ref, m_ref):
        p_t = jnp.exp2(s_ref[...] - m_ref[0:1, :]).astype(BF16)
        finish(_dot(vt_ref[0, 0], p_t), q_rows(i))

    scores(0, s0_ref, m0_ref)

    def body(j, carry):
        scores(2 * j + 1, s1_ref, m1_ref)
        values(2 * j, s0_ref, m0_ref)
        scores(2 * j + 2, s0_ref, m0_ref)
        values(2 * j + 1, s1_ref, m1_ref)
        return carry

    lax.fori_loop(0, n_lat // 2 - 1, body, 0)
    scores(n_lat - 1, s1_ref, m1_ref)
    values(n_lat - 2, s0_ref, m0_ref)
    values(n_lat - 1, s1_ref, m1_ref)

    if with_ctx:
        s_t = _dot_nt(k_ref[t_lat:, :], q_ref[0, t_lat:, :])
        p_t = jnp.exp2(s_t - jnp.max(s_t, axis=0, keepdims=True)).astype(BF16)
        finish(_dot(vt_ref[0, 0, :, t_lat:], p_t), slice(t_lat, t_lat + tq))


def _mla_attn(q, kn, kr, vt, t_lat, n_tiles):
    b, s, _ = q.shape
    tq = TOKEN_TILE
    n_lat = t_lat // tq
    return pl.pallas_call(
        functools.partial(_mla_attn_kernel, n_lat=n_lat, t_lat=t_lat, with_ctx=n_tiles > n_lat),
        grid=(b, MLA_HEADS),
        in_specs=[
            pl.BlockSpec((1, s, MLA_QK_PAD), lambda bi, hi: (bi, 0, hi)),
            pl.BlockSpec((1, s, MLA_NOPE), lambda bi, hi: (bi, 0, hi)),
            pl.BlockSpec((1, s, LANES), lambda bi, hi: (bi, 0, 0)),
            pl.BlockSpec((1, 1, MLA_VT_ROWS, s), lambda bi, hi: (bi, hi, 0, 0)),
        ],
        out_specs=pl.BlockSpec((1, n_tiles * tq, MLA_V), lambda bi, hi: (bi, 0, hi)),
        out_shape=jax.ShapeDtypeStruct((b, n_tiles * tq, MLA_HEADS * MLA_V), BF16),
        scratch_shapes=[pltpu.VMEM((s, MLA_QK_PAD), BF16),
                        pltpu.VMEM((s, tq), F32), pltpu.VMEM((s, tq), F32),
                        pltpu.VMEM((8, tq), F32), pltpu.VMEM((8, tq), F32)],
        compiler_params=_cparams(2),
        name="mla_attn",
    )(q, kn, kr, vt)


def _residual_inputs(xs, n_lat):
    tm = TOKEN_TILE
    if isinstance(xs, tuple):
        lat, ctx, ctx_block = xs[0], xs[1], 0
    else:
        lat, ctx, ctx_block = xs, xs, n_lat
    d = lat.shape[-1]
    specs = [pl.BlockSpec((1, tm, d), lambda bi, i: (bi, jnp.minimum(i, n_lat - 1), 0)),
             pl.BlockSpec((1, tm, d), lambda bi, i: (bi, ctx_block, 0))]
    return (lat, ctx), specs


def _pick_tile(xl_ref, xc_ref, n_lat):
    return jnp.where(pl.program_id(1) < n_lat, xl_ref[0], xc_ref[0])


def _mix_ffn_kernel(xl_ref, xc_ref, ya_ref, yb_ref, yc_ref, mg_ref, mod_ref,
                    wa_ref, wb_ref, wc_ref, wo_ref, gain_ref, wi_ref, wd_ref, fin_ref, o_ref,
                    *, n_lat, final):
    d = D_MODEL
    u = mg_ref[0, :, 0:d].astype(F32) * _dot(ya_ref[0], wa_ref[...])
    u += mg_ref[0, :, d:2 * d].astype(F32) * _dot(yb_ref[0], wb_ref[...])
    u += mg_ref[0, :, 2 * d:3 * d].astype(F32) * _dot(yc_ref[0], wc_ref[...])
    x = _pick_tile(xl_ref, xc_ref, n_lat) + mod_ref[0, 0, 2:3, :] * _dot(u.astype(BF16), wo_ref[...])

    y = _rms(x) * gain_ref[...]
    h = (y * (1.0 + mod_ref[0, 0, 4:5, :]) + mod_ref[0, 0, 3:4, :]).astype(BF16)
    gate = _dot(h, wi_ref[:, 0:D_FF])
    up = _dot(h, wi_ref[:, D_FF:2 * D_FF])
    a = (gate * jax.nn.sigmoid(gate) * up).astype(BF16)
    x = x + mod_ref[0, 0, 5:6, :] * _dot(a, wd_ref[...])
    if final:
        x = _rms(x) * fin_ref[...]
    o_ref[0] = x


def _mix_ffn(xs, ya, yb, yc, mg, modtab, wa, wb, wc, wo, gain, wi, wd, fin, n_lat, n_tiles, final):
    (xl, xc), x_specs = _residual_inputs(xs, n_lat)
    b, _, d = xl.shape
    tm = TOKEN_TILE
    row = lambda bi, i: (bi, i, 0)
    weights, w_specs = zip(*[_weight_spec(w) for w in (wa, wb, wc, wo, gain, wi, wd, fin)])
    return pl.pallas_call(
        functools.partial(_mix_ffn_kernel, n_lat=n_lat, final=final),
        grid=(b, n_tiles),
        in_specs=x_specs + [
            pl.BlockSpec((1, tm, d), row), pl.BlockSpec((1, tm, d), row),
            pl.BlockSpec((1, tm, d), row), pl.BlockSpec((1, tm, 3 * d), row),
            pl.BlockSpec((1, 1, 8, d), lambda bi, i: (bi, i // n_lat, 0, 0)),
        ] + list(w_specs),
        out_specs=pl.BlockSpec((1, tm, d), row),
        out_shape=jax.ShapeDtypeStruct((b, n_tiles * tm, d), F32),
        compiler_params=_cparams(2),
        name="mix_ffn",
    )(xl, xc, ya, yb, yc, mg, modtab, *weights)


def _rope_tables(t_lat, s_tot):
    half = SWA_HEAD_DIM // 4
    inv = np.power(ROPE_BASE, -np.arange(half, dtype=np.float32) / half).astype(np.float32)
    pos = np.arange(t_lat)
    lane = np.arange(SWA_HEAD_DIM)
    p = np.where(lane[None, :] < 2 * half, (pos // GRID_W)[:, None], (pos % GRID_W)[:, None])
    ang = p.astype(np.float32) * inv[lane % half][None, :]
    cos, sin = np.cos(ang), np.sin(ang)
    upper = (lane % (2 * half)) >= half
    tabs = np.stack([cos, np.where(upper, sin, 0.0), np.where(upper, 0.0, -sin)]).astype(np.float32)
    ident = np.zeros((3, s_tot - t_lat, SWA_HEAD_DIM), np.float32)
    ident[0] = 1.0
    return np.concatenate([tabs, ident], axis=1)


def kernel(x, c, ctx, c_ctx, w_mod, b_mod, norm_mix, w_in, w_gk_fwd, b_gk_fwd, w_gk_bwd, b_gk_bwd,
           gla_norm, sinks, q_norm, w_q_up, kv_norm, w_kv_up, w_pa, w_pb, w_pc, w_o,
           norm_ffn, w_ffn_in, w_ffn_out, final_norm):
    b, t_lat, d = x.shape
    l_ctx = ctx.shape[1]
    s_tot = t_lat + l_ctx
    depth = w_mod.shape[0]
    tm = TOKEN_TILE
    assert t_lat % tm == 0 and l_ctx == tm and t_lat % GRID_W == 0 and d == D_MODEL
    n_lat = t_lat // tm
    n_all = s_tot // tm

    cc = jnp.concatenate([c, c_ctx[None, :], jnp.zeros((16 - b - 1, d), F32)], axis=0)
    mod = _modulation(cc, w_mod, b_mod)
    mod_lat = mod[:, :b].reshape(depth, b, 1, 6, d)
    mod_ctx = jnp.broadcast_to(mod[:, b].reshape(depth, 1, 1, 6, d), (depth, b, 1, 6, d))
    modtab = jnp.pad(jnp.concatenate([mod_lat, mod_ctx], axis=2),
                     ((0, 0), (0, 0), (0, 0), (0, 2), (0, 0)))

    t64 = _rope_tables(t_lat, s_tot)
    ident = np.zeros_like(t64)
    ident[0] = 1.0
    rk = np.concatenate([t64, t64], axis=2)
    rq = rk * np.float32(SWA_HEAD_DIM ** -0.5 * LOG2_E)
    rm = np.concatenate([t64, ident], axis=2)
    rmq = rm * np.float32((MLA_NOPE + MLA_ROPE) ** -0.5 * LOG2_E)
    rq, rk, rm, rmq = (jnp.asarray(a) for a in (rq, rk, rm, rmq))

    w_in_t = jnp.swapaxes(w_in, 1, 2).astype(BF16)
    w_pa, w_pb, w_pc, w_o, w_ffn_in, w_ffn_out = (
        w.astype(BF16) for w in (w_pa, w_pb, w_pc, w_o, w_ffn_in, w_ffn_out))

    gla_sums = _gla_sums()

    xs = (x, ctx)
    for l in range(depth):
        last = l == depth - 1
        n_out = n_lat if last else n_all
        hk = GLA_HEADS * GLA_DK
        wg = jnp.zeros((2 * GLA_GATE_RANK, 2 * hk), F32)
        wg = wg.at[0:GLA_GATE_RANK, 0:hk].set(w_gk_fwd[l])
        wg = wg.at[GLA_GATE_RANK:, hk:].set(w_gk_bwd[l])
        bg = jnp.concatenate([b_gk_fwd[l], b_gk_bwd[l]]).reshape(1, 2 * hk)
        gla_weights = (wg.astype(BF16), bg, gla_sums)
        wq = jnp.pad(w_q_up[l].reshape(MLA_Q_RANK, MLA_HEADS, MLA_NOPE + MLA_ROPE),
                     ((0, 0), (0, 0), (0, MLA_QK_PAD - MLA_NOPE - MLA_ROPE))
                     ).reshape(MLA_Q_RANK, MLA_HEADS * MLA_QK_PAD).astype(BF16)

        wkv = w_kv_up[l].reshape(MLA_KV_RANK, MLA_HEADS, MLA_NOPE + MLA_V)
        wk = wkv[:, :, :MLA_NOPE].reshape(MLA_KV_RANK, MLA_HEADS * MLA_NOPE).astype(BF16)
        wvt = wkv[:, :, MLA_NOPE:].reshape(MLA_KV_RANK, MLA_HEADS * MLA_V).T.astype(BF16)
        mla_weights = (wq, wk, wvt, q_norm[l].reshape(1, -1), kv_norm[l].reshape(1, -1))

        qd, ki, tot, va, ga, qs, kx, vx, mg, qm, kn, kr, vm = _inproj(
            xs, modtab[l], norm_mix[l].reshape(1, d), (w_in_t, l), rq, rk, rm,
            mla_weights, rmq, gla_weights, n_lat)
        ya = _gla(qd, ki, tot, va, ga, gla_norm[l].reshape(1, GLA_DV), t_lat)
        yb = _swa(sinks[l], qs, kx, vx, t_lat, n_out)
        yc = _mla_attn(qm, kn, kr, vm, t_lat, n_out)
        xs = _mix_ffn(xs, ya, yb, yc, mg, modtab[l], (w_pa, l), (w_pb, l), (w_pc, l), (w_o, l),
                      norm_ffn[l].reshape(1, d), (w_ffn_in, l), (w_ffn_out, l),
                      final_norm.reshape(1, d), n_lat, n_out, last)
    return xs
```

```python
import functools

import jax
import jax.numpy as jnp
import numpy as np
from jax import lax
from jax.experimental import pallas as pl
from jax.experimental.pallas import tpu as pltpu

F32 = jnp.float32
BF16 = jnp.bfloat16

D_MODEL = 1024
GRID_W = 64
EPS = 1e-6
ROPE_BASE = 10000.0

GLA_HEADS = 4
GLA_DK = 128
GLA_DV = 256
GLA_GATE_RANK = 16
GLA_GATE_NORM = 16.0
GLA_CHUNK = 64

SWA_HEADS = 16
SWA_KV_HEADS = 2
SWA_GROUP = SWA_HEADS // SWA_KV_HEADS
SWA_HEAD_DIM = 64
WINDOW = 128
SWA_VT_ROWS = SWA_HEAD_DIM + 16

MLA_HEADS = 8
MLA_Q_RANK = 384
MLA_KV_RANK = 256
MLA_NOPE = 128
MLA_ROPE = 64
MLA_V = 128
MLA_QK_PAD = 256
MLA_VT_ROWS = MLA_V + 16
LOG2_E = 1.4426950408889634

D_FF = -(-(8 * D_MODEL) // (3 * 256)) * 256

IN_SPLITS = (
    GLA_HEADS * GLA_DK, GLA_HEADS * GLA_DK, GLA_HEADS * GLA_DV, GLA_HEADS * GLA_DV,
    GLA_GATE_RANK, GLA_GATE_RANK,
    SWA_HEADS * SWA_HEAD_DIM, SWA_KV_HEADS * SWA_HEAD_DIM, SWA_KV_HEADS * SWA_HEAD_DIM,
    MLA_Q_RANK, MLA_KV_RANK, MLA_ROPE,
    3 * D_MODEL,
)

LANES = 128
TOKEN_TILE = 256
(ROW_QA, ROW_KA, ROW_VA, ROW_GA, ROW_GKF, ROW_GKB, ROW_QS, ROW_KS, ROW_VS,
 ROW_CQ, ROW_CKV, ROW_KR, ROW_MG, ROW_END) = (int(v) for v in np.cumsum((0,) + IN_SPLITS))
VMEM_LIMIT = 56 * 1024 * 1024


def _cparams(n_axes):
    return pltpu.CompilerParams(
        dimension_semantics=("arbitrary",) * n_axes, vmem_limit_bytes=VMEM_LIMIT)


def _const_spec(shape):
    nd = len(shape)
    return pl.BlockSpec(shape, lambda *_: (0,) * nd, pipeline_mode=pl.Buffered(1))


def _weight_spec(w):
    if isinstance(w, tuple):
        arr, layer = w
        nd = arr.ndim
        spec = pl.BlockSpec((None,) + arr.shape[1:], lambda *_: (layer,) + (0,) * (nd - 1),
                            pipeline_mode=pl.Buffered(1))
        return arr, spec
    return w, _const_spec(w.shape)


def _rope(x, tab_ref):
    return (x * tab_ref[0] + pltpu.roll(x, 16, 1) * tab_ref[1]
            + pltpu.roll(x, LANES - 16, 1) * tab_ref[2])


def _rms(x):
    return x * lax.rsqrt(jnp.mean(x * x, axis=-1, keepdims=True) + EPS)


def _dot(a, b):
    return jnp.dot(a, b, preferred_element_type=F32)


def _dot_nt(a, b):
    return lax.dot_general(a, b, (((1,), (1,)), ((), ())), preferred_element_type=F32)


def _dot_tn(a, b):
    return lax.dot_general(a, b, (((0,), (0,)), ((), ())), preferred_element_type=F32)


def _mod_kernel(c_ref, w_ref, b_ref, o_ref):
    c = c_ref[...]
    a = (c * jax.nn.sigmoid(c)).astype(BF16)
    o_ref[0] = _dot(a, w_ref[0].astype(BF16)) + b_ref[0]


def _modulation(cc, w_mod, b_mod):
    depth, d, n = w_mod.shape
    tn = 1536
    return pl.pallas_call(
        _mod_kernel,
        grid=(depth, n // tn),
        in_specs=[
            pl.BlockSpec(cc.shape, lambda l, j: (0, 0)),
            pl.BlockSpec((1, d, tn), lambda l, j: (l, 0, j)),
            pl.BlockSpec((1, 1, tn), lambda l, j: (l, 0, j)),
        ],
        out_specs=pl.BlockSpec((1, cc.shape[0], tn), lambda l, j: (l, 0, j)),
        out_shape=jax.ShapeDtypeStruct((depth, cc.shape[0], n), F32),
        compiler_params=_cparams(2),
        name="modulation",
    )(cc, w_mod, b_mod.reshape(depth, 1, n))


def _mla_project(cq, ckv, tail, wq_ref, wk_ref, wvt_ref, qn_ref, kvn_ref, rq_ref,
                 q_ref, k_ref, kr_ref, vt_ref):
    scale = (MLA_NOPE + MLA_ROPE) ** -0.5 * LOG2_E
    lo = lax.broadcasted_iota(jnp.int32, tail.shape, 1) < MLA_ROPE
    kr_ref[0] = jnp.where(lo, tail, 0.0).astype(BF16)

    qf = _dot((_rms(cq) * qn_ref[...]).astype(BF16), wq_ref[...])
    ckvn = (_rms(ckv) * kvn_ref[...]).astype(BF16)
    k_ref[0] = _dot(ckvn, wk_ref[...]).astype(BF16)
    vt = _dot_nt(wvt_ref[...], ckvn)
    ones = jnp.ones((MLA_VT_ROWS - MLA_V, vt.shape[1]), BF16)
    for h in range(MLA_HEADS):
        c0 = h * MLA_QK_PAD
        q_ref[0, :, c0:c0 + LANES] = (qf[:, c0:c0 + LANES] * scale).astype(BF16)
        q_ref[0, :, c0 + LANES:c0 + 2 * LANES] = _rope(
            qf[:, c0 + LANES:c0 + 2 * LANES], rq_ref).astype(BF16)
        vt_ref[0, h, 0:MLA_V, :] = vt[h * MLA_V:(h + 1) * MLA_V, :].astype(BF16)
        vt_ref[0, h, MLA_V:MLA_VT_ROWS, :] = ones


def _gla_decays(qk, gates, wg_ref, bg_ref, sum_ref, qd_ref, ki_ref, tot_ref):
    hk = GLA_HEADS * GLA_DK
    c = GLA_CHUNK
    q = qk[:, 0:hk] * (GLA_DK ** -0.5)
    k = qk[:, hk:]
    z = _dot(gates.astype(BF16), wg_ref[...]) + bg_ref[...]
    la = (jnp.minimum(z, 0.0) - jnp.log1p(jnp.exp(-jnp.abs(z)))) * (1.0 / GLA_GATE_NORM)
    hi = la.astype(BF16)
    lo = (la - hi.astype(F32)).astype(BF16)
    totals = []
    for d in range(2):
        sl = slice(d * hk, (d + 1) * hk)
        cr = _dot(sum_ref[d], jnp.concatenate([hi[:, sl], lo[:, sl]], axis=1))
        cum = cr[:, 0:hk] + cr[:, hk:]
        qd_ref[0, :, sl] = (q * jnp.exp(cum)).astype(BF16)
        ki_ref[0, :, sl] = (k * jnp.exp(-cum)).astype(BF16)
        last = (c - 1) if d == 0 else 0
        ends = [cum[j * c + last:j * c + last + 1] for j in range(TOKEN_TILE // c)]
        totals.append(jnp.concatenate(ends + [jnp.zeros((8 - len(ends), hk), F32)], axis=0))
    tot_ref[0, 0] = jnp.concatenate(totals, axis=1)


INPROJ_INPUTS = 17
INPROJ_OUTPUTS = 13


def _inproj_kernel(*refs, n_lat, n_cast):
    n_in = INPROJ_INPUTS + n_cast
    _inproj_body(*refs[:INPROJ_INPUTS], *refs[n_in:n_in + INPROJ_OUTPUTS], n_lat=n_lat)
    for src, dst in zip(refs[INPROJ_INPUTS:n_in], refs[n_in + INPROJ_OUTPUTS:]):
        dst[...] = src[...].astype(BF16)


def _inproj_body(xl_ref, xc_ref, mod_ref, gain_ref, w_ref, rq_ref, rk_ref, rm_ref,
                 wq_ref, wk_ref, wmvt_ref, qn_ref, kvn_ref, rmq_ref, wg_ref, bg_ref, sum_ref,
                 qd_ref, ki_ref, tot_ref, va_ref, ga_ref, qs_ref, kx_ref, vx_ref, mg_ref,
                 mq_ref, mk_ref, mkr_ref, mvt_ref, *, n_lat):
    x = _pick_tile(xl_ref, xc_ref, n_lat)
    y = _rms(x) * gain_ref[...]
    h = (y * (1.0 + mod_ref[0, 0, 1:2, :]) + mod_ref[0, 0, 0:1, :]).astype(BF16)

    def proj(c0, c1):
        return _dot_nt(h, w_ref[c0:c1, :])

    r = proj(ROW_CQ, ROW_KR + LANES)
    tail = _rope(r[:, ROW_KR - ROW_CQ:], rm_ref)
    _mla_project(r[:, 0:MLA_Q_RANK], r[:, MLA_Q_RANK:MLA_Q_RANK + MLA_KV_RANK], tail,
                 wq_ref, wk_ref, wmvt_ref, qn_ref, kvn_ref, rmq_ref, mq_ref, mk_ref, mkr_ref, mvt_ref)

    r = proj(ROW_GA, ROW_QS)
    g = r[:, 0:ROW_GKF - ROW_GA]
    ga_ref[0] = (g * jax.nn.sigmoid(g)).astype(BF16)
    _gla_decays(proj(ROW_QA, ROW_VA), r[:, ROW_GKF - ROW_GA:], wg_ref, bg_ref, sum_ref,
                qd_ref, ki_ref, tot_ref)
    va_ref[0] = proj(ROW_VA, ROW_GA).astype(BF16)

    r = proj(ROW_QS, ROW_KS)
    for j in range(1024 // LANES):
        sl = slice(j * LANES, (j + 1) * LANES)
        qs_ref[0, :, sl] = _rope(r[:, sl], rq_ref).astype(BF16)

    kk = _rope(proj(ROW_KS, ROW_VS), rk_ref)
    kx = pltpu.roll(kk, 64, 1)
    lo = lax.broadcasted_iota(jnp.int32, kk.shape, 1) < 64
    zero = jnp.zeros_like(kk)
    blocks = (jnp.where(lo, kk, zero), jnp.where(lo, zero, kx),
              jnp.where(lo, kx, zero), jnp.where(lo, zero, kk))
    for j, blk in enumerate(blocks):
        kx_ref[0, :, j * LANES:(j + 1) * LANES] = blk.astype(BF16)

    vt = _dot_nt(w_ref[ROW_VS:ROW_CQ, :], h).astype(BF16)
    hd = SWA_HEAD_DIM
    ones = jnp.ones((SWA_VT_ROWS - hd, vt.shape[1]), BF16)
    for g in range(SWA_KV_HEADS):
        vx_ref[0, g * SWA_VT_ROWS:g * SWA_VT_ROWS + hd, :] = vt[g * hd:(g + 1) * hd, :]
        vx_ref[0, g * SWA_VT_ROWS + hd:(g + 1) * SWA_VT_ROWS, :] = ones

    for j in range(3):
        r = proj(ROW_MG + j * 1024, ROW_MG + (j + 1) * 1024)
        mg_ref[0, :, j * 1024:(j + 1) * 1024] = jax.nn.sigmoid(r).astype(BF16)


def _cast_rows(rows, steps):
    return next(r for r in range(16, rows + 1, 16) if rows % r == 0 and rows // r <= steps)


def _inproj(xs, modtab, gain, w, rq, rk, rm, mla_weights, rmq, gla_weights, casts, n_lat):
    (xl, xc), x_specs = _residual_inputs(xs, n_lat)
    b, _, d = xl.shape
    tm = TOKEN_TILE
    nt = n_lat + 1
    s = nt * tm
    row = lambda bi, i: (bi, i, 0)
    tab = pl.BlockSpec((3, tm, LANES), lambda bi, i: (0, i, 0))
    vx_rows = SWA_KV_HEADS * SWA_VT_ROWS
    gla_w = 2 * GLA_HEADS * GLA_DK
    rows_out = lambda n, dt: (pl.BlockSpec((1, tm, n), row), jax.ShapeDtypeStruct((b, s, n), dt))
    outs = [rows_out(gla_w, BF16), rows_out(gla_w, BF16),
            (pl.BlockSpec((1, 1, 8, gla_w), lambda bi, i: (bi, i, 0, 0)),
             jax.ShapeDtypeStruct((b, nt, 8, gla_w), F32)),
            rows_out(1024, BF16), rows_out(1024, BF16), rows_out(1024, BF16),
            rows_out(SWA_KV_HEADS * 2 * LANES, BF16),
            (pl.BlockSpec((1, vx_rows, tm), lambda bi, i: (bi, 0, i)),
             jax.ShapeDtypeStruct((b, vx_rows, s), BF16)),
            rows_out(3072, BF16),
            rows_out(MLA_HEADS * MLA_QK_PAD, BF16), rows_out(MLA_HEADS * MLA_NOPE, BF16),
            rows_out(LANES, BF16),
            (pl.BlockSpec((1, MLA_HEADS, MLA_VT_ROWS, tm), lambda bi, i: (bi, 0, 0, i)),
             jax.ShapeDtypeStruct((b, MLA_HEADS, MLA_VT_ROWS, s), BF16))]
    w, w_spec = _weight_spec(w)
    cast_in, cast_out = [], []
    for arr, layer in casts:
        rows, cols = arr.shape[1:]
        blk = _cast_rows(rows, b * nt)
        last = rows // blk - 1
        cast_in.append(pl.BlockSpec(
            (None, blk, cols),
            lambda bi, i, layer=layer, last=last: (layer, jnp.minimum(bi * nt + i, last), 0)))
        cast_out.append((pl.BlockSpec(
            (blk, cols), lambda bi, i, last=last: (jnp.minimum(bi * nt + i, last), 0)),
            jax.ShapeDtypeStruct((rows, cols), BF16)))
    outs = outs + cast_out
    return pl.pallas_call(
        functools.partial(_inproj_kernel, n_lat=n_lat, n_cast=len(casts)),
        grid=(b, nt),
        in_specs=x_specs + [
            pl.BlockSpec((1, 1, 8, d), lambda bi, i: (bi, i // n_lat, 0, 0)),
            _const_spec((1, d)),
            w_spec,
            tab, tab, tab,
        ] + [_const_spec(a.shape) for a in mla_weights] + [tab]
        + [_const_spec(a.shape) for a in gla_weights] + cast_in,
        out_specs=[o[0] for o in outs],
        out_shape=[o[1] for o in outs],
        compiler_params=_cparams(2),
        name="inproj",
    )(xl, xc, modtab, gain, w, rq, rk, rm, *mla_weights, rmq, *gla_weights, *[a for a, _ in casts])


def _gla_kernel(qdf_ref, qdb_ref, kif_ref, kib_ref, v_ref, g_ref, totf_ref, totb_ref, gn_ref,
                sel_ref, o_ref,
                oi_ref, af_ref, ab_ref, st_ref, decf_ref, decb_ref, keepf_ref, keepb_ref, *, n_lat):
    c = GLA_CHUNK
    tm = TOKEN_TILE
    per = tm // c

    ri = lax.broadcasted_iota(jnp.int32, (tm, tm), 0)
    ci = lax.broadcasted_iota(jnp.int32, (tm, tm), 1)
    same = (ri // c) == (ci // c)
    keepf_ref[...] = jnp.where(same & (ci <= ri), 1.0, 0.0)
    keepb_ref[...] = jnp.where(same & (ci > ri), 1.0, 0.0)

    def tile_rows(t):
        return pl.ds(pl.multiple_of(t * tm, tm), tm)

    def in_chunk(t, qd_ref, ki_ref, tot_ref, a_ref, dec_ref):
        rows = tile_rows(t)
        tot = tot_ref[0, t]
        hi = tot.astype(BF16)
        lo = (tot - hi.astype(F32)).astype(BF16)
        dec_ref[t] = jnp.exp(_dot_tn(jnp.concatenate([hi, lo], axis=0), sel_ref[...]))
        v = v_ref[0, rows, :]
        q_dec = qd_ref[0, rows, :]
        k_inv = ki_ref[0, rows, :]
        for j in range(per):
            a_ref[t * per + j] = _dot_tn(k_inv[j * c:(j + 1) * c], v[j * c:(j + 1) * c])
        return _dot_nt(q_dec, k_inv)

    def bulk_body(t, carry):
        s_f = in_chunk(t, qdf_ref, kif_ref, totf_ref, af_ref, decf_ref)
        s_b = in_chunk(t, qdb_ref, kib_ref, totb_ref, ab_ref, decb_ref)
        scores = jnp.where(keepf_ref[...] > 0.5, s_f, jnp.where(keepb_ref[...] > 0.5, s_b, 0.0))
        oi_ref[tile_rows(t), :] = _dot(scores.astype(BF16), v_ref[0, tile_rows(t), :])
        return carry

    lax.fori_loop(0, n_lat + 1, bulk_body, 0, unroll=3)

    def scan_tile(t, order, state, a_ref, dec_ref, st_rows):
        for j in order:
            st_ref[t * per + j, st_rows, :] = state.astype(BF16)
            dec = dec_ref[t, :, j * LANES:(j + 1) * LANES]
            state = jnp.concatenate([dec] * (GLA_DV // LANES), axis=1) * (state + a_ref[t * per + j])
        return state

    fwd_order = range(per)
    bwd_order = range(per - 1, -1, -1)
    zero = jnp.zeros((GLA_DK, GLA_DV), F32)
    f_rows, b_rows = slice(0, GLA_DK), slice(GLA_DK, 2 * GLA_DK)
    s_f = scan_tile(n_lat, fwd_order, zero, af_ref, decf_ref, f_rows)
    s_b = scan_tile(n_lat, bwd_order, zero, ab_ref, decb_ref, b_rows)
    lax.fori_loop(0, n_lat, lambda i, s: scan_tile(
        i, fwd_order, s, af_ref, decf_ref, f_rows), s_f)
    lax.fori_loop(0, n_lat, lambda i, s: scan_tile(
        n_lat - 1 - i, bwd_order, s, ab_ref, decb_ref, b_rows), s_b)

    def post_body(t, carry):
        for j in range(per):
            rows = pl.ds(pl.multiple_of(t * tm + j * c, c), c)
            q_both = jnp.concatenate([qdf_ref[0, rows, :], qdb_ref[0, rows, :]], axis=1)
            o = oi_ref[rows, :] + _dot(q_both, st_ref[t * per + j])
            y = _rms(o) * gn_ref[...]
            o_ref[0, rows, :] = (y * g_ref[0, rows, :].astype(F32)).astype(BF16)
        return carry

    lax.fori_loop(0, n_lat + 1, post_body, 0, unroll=3)


def _gla_sums():
    tm, c = TOKEN_TILE, GLA_CHUNK
    r = np.arange(tm)[:, None]
    col = np.arange(tm)[None, :]
    same = (r // c) == (col // c)
    return jnp.asarray(np.stack([same & (col <= r), same & (col >= r)]).astype(np.float32), BF16)


def _gla_selector():
    per = TOKEN_TILE // GLA_CHUNK
    sel = np.zeros((16, per * LANES), np.float32)
    for j in range(per):
        sel[j, j * LANES:(j + 1) * LANES] = 1.0
        sel[8 + j, j * LANES:(j + 1) * LANES] = 1.0
    return jnp.asarray(sel, BF16)


def _gla(qd, ki, tot, va, ga, gn, t_lat):
    b, s, _ = va.shape
    tm = TOKEN_TILE
    n_lat = t_lat // tm
    n_tiles = s // tm
    n_chunks = s // GLA_CHUNK
    h = GLA_HEADS
    sel = _gla_selector()
    fwd = lambda bi, hi: (bi, 0, hi)
    bwd = lambda bi, hi: (bi, 0, h + hi)
    return pl.pallas_call(
        functools.partial(_gla_kernel, n_lat=n_lat),
        grid=(b, h),
        in_specs=[
            pl.BlockSpec((1, s, GLA_DK), fwd), pl.BlockSpec((1, s, GLA_DK), bwd),
            pl.BlockSpec((1, s, GLA_DK), fwd), pl.BlockSpec((1, s, GLA_DK), bwd),
            pl.BlockSpec((1, s, GLA_DV), fwd), pl.BlockSpec((1, s, GLA_DV), fwd),
            pl.BlockSpec((1, n_tiles, 8, GLA_DK), lambda bi, hi: (bi, 0, 0, hi)),
            pl.BlockSpec((1, n_tiles, 8, GLA_DK), lambda bi, hi: (bi, 0, 0, h + hi)),
            pl.BlockSpec((1, GLA_DV), lambda bi, hi: (0, 0)),
            _const_spec(sel.shape),
        ],
        out_specs=pl.BlockSpec((1, s, GLA_DV), fwd),
        out_shape=jax.ShapeDtypeStruct((b, s, h * GLA_DV), BF16),
        scratch_shapes=[
            pltpu.VMEM((s, GLA_DV), F32),
            pltpu.VMEM((n_chunks, GLA_DK, GLA_DV), F32), pltpu.VMEM((n_chunks, GLA_DK, GLA_DV), F32),
            pltpu.VMEM((n_chunks, 2 * GLA_DK, GLA_DV), BF16),
            pltpu.VMEM((n_tiles, GLA_DK, tm // GLA_CHUNK * LANES), F32),
            pltpu.VMEM((n_tiles, GLA_DK, tm // GLA_CHUNK * LANES), F32),
            pltpu.VMEM((tm, tm), F32), pltpu.VMEM((tm, tm), F32),
        ],
        compiler_params=_cparams(2),
        name="gla",
    )(qd, qd, ki, ki, va, ga, tot, tot, gn, sel)


def _swa_kernel(sink_ref, q_ref, kx_ref, vx_ref, bias_ref, o_ref, s0_ref, s1_ref, m0_ref, m1_ref,
                *, n_lat, t_lat, with_ctx):
    tq = TOKEN_TILE
    span = tq + 2 * WINDOW
    hd = SWA_HEAD_DIM
    g = pl.program_id(1)
    ka, kb = slice(0, LANES), slice(LANES, 2 * LANES)

    def q_rows(i):
        return pl.ds(pl.multiple_of(i * tq, tq), tq)

    def sink_of(head):
        return sink_ref[g * SWA_GROUP + head] * LOG2_E

    def keys(i):
        ws = pl.multiple_of(jnp.clip(i * tq - WINDOW, 0, t_lat - span), WINDOW)
        win = pl.ds(ws, span)
        kc = [jnp.concatenate([kx_ref[0, win, c], kx_ref[0, t_lat:, c]], axis=0) for c in (ka, kb)]
        vc = jnp.concatenate([vx_ref[0, :, win], vx_ref[0, :, t_lat:]], axis=1)
        return kc, vc, (i * tq - ws) // WINDOW

    def scores(i, kc, bias_idx, head, s_ref, m_ref):
        qp = q_ref[0, q_rows(i), (head // 2) * LANES:(head // 2 + 1) * LANES]
        s_t = _dot_nt(kc[head % 2], qp)
        top = s_t[0:span] + bias_ref[bias_idx]
        bot = s_t[span:]
        s_ref[0:span, :] = top
        s_ref[span:, :] = bot
        m = jnp.maximum(jnp.max(top, axis=0, keepdims=True), jnp.max(bot, axis=0, keepdims=True))
        m_ref[...] = jnp.broadcast_to(jnp.maximum(m, sink_of(head)), m_ref.shape)

    def normalise(acc, m, head):
        return acc[0:hd] / (acc[hd:hd + 1] + jnp.exp2(sink_of(head) - m))

    def values(vc, head, s_ref, m_ref):
        m = m_ref[0:1, :]
        p_t = jnp.exp2(s_ref[...] - m).astype(BF16)
        return normalise(_dot(vc, p_t), m, head)

    def store(i, pair, o_a, o_b):
        o_t = jnp.concatenate([o_a, o_b], axis=0)
        o_ref[0, q_rows(i), pair * LANES:(pair + 1) * LANES] = o_t.T.astype(BF16)

    def scores_all(i, s_ref, m_ref):
        kc, _, bias_idx = keys(i)
        for head in range(SWA_GROUP):
            scores(i, kc, bias_idx, head, s_ref.at[head], m_ref.at[head])

    def values_all(i, s_ref, m_ref):
        _, vc, _ = keys(i)
        for pair in range(SWA_GROUP // 2):
            a, b = 2 * pair, 2 * pair + 1
            store(i, pair, values(vc, a, s_ref.at[a], m_ref.at[a]),
                  values(vc, b, s_ref.at[b], m_ref.at[b]))

    scores_all(0, s0_ref, m0_ref)

    def body(j, carry):
        scores_all(2 * j + 1, s1_ref, m1_ref)
        values_all(2 * j, s0_ref, m0_ref)
        scores_all(2 * j + 2, s0_ref, m0_ref)
        values_all(2 * j + 1, s1_ref, m1_ref)
        return carry

    lax.fori_loop(0, n_lat // 2 - 1, body, 0)
    scores_all(n_lat - 1, s1_ref, m1_ref)
    values_all(n_lat - 2, s0_ref, m0_ref)
    values_all(n_lat - 1, s1_ref, m1_ref)

    if with_ctx:
        rows = slice(t_lat, t_lat + tq)
        for pair in range(SWA_GROUP // 2):
            qp = q_ref[0, rows, pair * LANES:(pair + 1) * LANES]
            outs = []
            for head, kcols in ((2 * pair, ka), (2 * pair + 1, kb)):
                s_t = _dot_nt(kx_ref[0, t_lat:, kcols], qp)
                m = jnp.maximum(jnp.max(s_t, axis=0, keepdims=True), sink_of(head))
                p_t = jnp.exp2(s_t - m).astype(BF16)
                outs.append(normalise(_dot(vx_ref[0, :, t_lat:], p_t), m, head))
            o_t = jnp.concatenate(outs, axis=0)
            o_ref[0, rows, pair * LANES:(pair + 1) * LANES] = o_t.T.astype(BF16)


def _swa_bias():
    tq, span = TOKEN_TILE, TOKEN_TILE + 2 * WINDOW
    r = np.arange(span)[:, None]
    c = np.arange(tq)[None, :]
    tabs = [np.where(np.abs(r - off - c) <= WINDOW, 0.0, -1e30) for off in (0, WINDOW, 2 * WINDOW)]
    return jnp.asarray(np.stack(tabs).astype(np.float32))


def _swa(sinks, qs, kx, vx, t_lat, n_tiles):
    b, s, _ = qs.shape
    tq = TOKEN_TILE
    gw = SWA_GROUP * SWA_HEAD_DIM
    n_keys = tq + 2 * WINDOW + (s - t_lat)
    bias = _swa_bias()
    n_lat = t_lat // tq
    return pl.pallas_call(
        functools.partial(_swa_kernel, n_lat=n_lat, t_lat=t_lat, with_ctx=n_tiles > n_lat),
        grid=(b, SWA_KV_HEADS),
        in_specs=[
            pl.BlockSpec(memory_space=pltpu.SMEM),
            pl.BlockSpec((1, s, gw), lambda bi, gi: (bi, 0, gi)),
            pl.BlockSpec((1, s, 2 * LANES), lambda bi, gi: (bi, 0, gi)),
            pl.BlockSpec((1, SWA_VT_ROWS, s), lambda bi, gi: (bi, gi, 0)),
            _const_spec(bias.shape),
        ],
        out_specs=pl.BlockSpec((1, n_tiles * tq, gw), lambda bi, gi: (bi, 0, gi)),
        out_shape=jax.ShapeDtypeStruct((b, n_tiles * tq, SWA_HEADS * SWA_HEAD_DIM), BF16),
        scratch_shapes=[pltpu.VMEM((SWA_GROUP, n_keys, tq), F32), pltpu.VMEM((SWA_GROUP, n_keys, tq), F32),
                        pltpu.VMEM((SWA_GROUP, 8, tq), F32), pltpu.VMEM((SWA_GROUP, 8, tq), F32)],
        compiler_params=_cparams(2),
        name="swa",
    )(sinks, qs, kx, vx, bias)


def _mla_attn_kernel(q_ref, kn_ref, kr_ref, vt_ref, o_ref, k_ref, s0_ref, s1_ref, m0_ref, m1_ref,
                     *, n_lat, t_lat, with_ctx):
    tq = TOKEN_TILE
    k_ref[:, 0:MLA_NOPE] = kn_ref[0]
    k_ref[:, MLA_NOPE:] = kr_ref[0]

    def q_rows(i):
        return pl.ds(pl.multiple_of(i * tq, tq), tq)

    def finish(acc, rows):
        o_t = acc[0:MLA_V] / acc[MLA_V:MLA_V + 1]
        o_ref[0, rows, :] = o_t.T.astype(BF16)

    def scores(i, s_ref, m_ref):
        q = q_ref[0, q_rows(i), :]
        half = k_ref.shape[0] // 2
        lo = _dot_nt(k_ref[0:half, :], q)
        hi = _dot_nt(k_ref[half:, :], q)
        s_ref[0:half, :] = lo
        s_ref[half:, :] = hi
        m = jnp.maximum(jnp.max(lo, axis=0, keepdims=True), jnp.max(hi, axis=0, keepdims=True))
        m_ref[...] = jnp.broadcast_to(m, m_ref.shape)

    def values(i, s_ref, m_ref):
        p_t = jnp.exp2(s_ref[...] - m_ref[0:1, :]).astype(BF16)
        finish(_dot(vt_ref[0, 0], p_t), q_rows(i))

    scores(0, s0_ref, m0_ref)

    def body(j, carry):
        scores(2 * j + 1, s1_ref, m1_ref)
        values(2 * j, s0_ref, m0_ref)
        scores(2 * j + 2, s0_ref, m0_ref)
        values(2 * j + 1, s1_ref, m1_ref)
        return carry

    lax.fori_loop(0, n_lat // 2 - 1, body, 0)
    scores(n_lat - 1, s1_ref, m1_ref)
    values(n_lat - 2, s0_ref, m0_ref)
    values(n_lat - 1, s1_ref, m1_ref)

    if with_ctx:
        s_t = _dot_nt(k_ref[t_lat:, :], q_ref[0, t_lat:, :])
        p_t = jnp.exp2(s_t - jnp.max(s_t, axis=0, keepdims=True)).astype(BF16)
        finish(_dot(vt_ref[0, 0, :, t_lat:], p_t), slice(t_lat, t_lat + tq))


def _mla_attn(q, kn, kr, vt, t_lat, n_tiles):
    b, s, _ = q.shape
    tq = TOKEN_TILE
    n_lat = t_lat // tq
    return pl.pallas_call(
        functools.partial(_mla_attn_kernel, n_lat=n_lat, t_lat=t_lat, with_ctx=n_tiles > n_lat),
        grid=(b, MLA_HEADS),
        in_specs=[
            pl.BlockSpec((1, s, MLA_QK_PAD), lambda bi, hi: (bi, 0, hi)),
            pl.BlockSpec((1, s, MLA_NOPE), lambda bi, hi: (bi, 0, hi)),
            pl.BlockSpec((1, s, LANES), lambda bi, hi: (bi, 0, 0)),
            pl.BlockSpec((1, 1, MLA_VT_ROWS, s), lambda bi, hi: (bi, hi, 0, 0)),
        ],
        out_specs=pl.BlockSpec((1, n_tiles * tq, MLA_V), lambda bi, hi: (bi, 0, hi)),
        out_shape=jax.ShapeDtypeStruct((b, n_tiles * tq, MLA_HEADS * MLA_V), BF16),
        scratch_shapes=[pltpu.VMEM((s, MLA_QK_PAD), BF16),
                        pltpu.VMEM((s, tq), F32), pltpu.VMEM((s, tq), F32),
                        pltpu.VMEM((8, tq), F32), pltpu.VMEM((8, tq), F32)],
        compiler_params=_cparams(2),
        name="mla_attn",
    )(q, kn, kr, vt)


def _residual_inputs(xs, n_lat):
    tm = TOKEN_TILE
    if isinstance(xs, tuple):
        lat, ctx, ctx_block = xs[0], xs[1], 0
    else:
        lat, ctx, ctx_block = xs, xs, n_lat
    d = lat.shape[-1]
    specs = [pl.BlockSpec((1, tm, d), lambda bi, i: (bi, jnp.minimum(i, n_lat - 1), 0)),
             pl.BlockSpec((1, tm, d), lambda bi, i: (bi, ctx_block, 0))]
    return (lat, ctx), specs


def _pick_tile(xl_ref, xc_ref, n_lat):
    return jnp.where(pl.program_id(1) < n_lat, xl_ref[0], xc_ref[0])


def _mix_ffn_kernel(xl_ref, xc_ref, ya_ref, yb_ref, yc_ref, mg_ref, mod_ref,
                    wa_ref, wb_ref, wc_ref, wo_ref, gain_ref, wi_ref, wd_ref, fin_ref, o_ref,
                    *, n_lat, final):
    d = D_MODEL
    u = mg_ref[0, :, 0:d].astype(F32) * _dot(ya_ref[0], wa_ref[...])
    u += mg_ref[0, :, d:2 * d].astype(F32) * _dot(yb_ref[0], wb_ref[...])
    u += mg_ref[0, :, 2 * d:3 * d].astype(F32) * _dot(yc_ref[0], wc_ref[...])
    x = _pick_tile(xl_ref, xc_ref, n_lat) + mod_ref[0, 0, 2:3, :] * _dot(u.astype(BF16), wo_ref[...])

    y = _rms(x) * gain_ref[...]
    h = (y * (1.0 + mod_ref[0, 0, 4:5, :]) + mod_ref[0, 0, 3:4, :]).astype(BF16)
    gate = _dot(h, wi_ref[:, 0:D_FF])
    up = _dot(h, wi_ref[:, D_FF:2 * D_FF])
    a = (gate * jax.nn.sigmoid(gate) * up).astype(BF16)
    x = x + mod_ref[0, 0, 5:6, :] * _dot(a, wd_ref[...])
    if final:
        x = _rms(x) * fin_ref[...]
    o_ref[0] = x


def _mix_ffn(xs, ya, yb, yc, mg, modtab, wa, wb, wc, wo, gain, wi, wd, fin, n_lat, n_tiles, final):
    (xl, xc), x_specs = _residual_inputs(xs, n_lat)
    b, _, d = xl.shape
    tm = TOKEN_TILE
    row = lambda bi, i: (bi, i, 0)
    weights, w_specs = zip(*[_weight_spec(w) for w in (wa, wb, wc, wo, gain, wi, wd, fin)])
    return pl.pallas_call(
        functools.partial(_mix_ffn_kernel, n_lat=n_lat, final=final),
        grid=(b, n_tiles),
        in_specs=x_specs + [
            pl.BlockSpec((1, tm, d), row), pl.BlockSpec((1, tm, d), row),
            pl.BlockSpec((1, tm, d), row), pl.BlockSpec((1, tm, 3 * d), row),
            pl.BlockSpec((1, 1, 8, d), lambda bi, i: (bi, i // n_lat, 0, 0)),
        ] + list(w_specs),
        out_specs=pl.BlockSpec((1, tm, d), row),
        out_shape=jax.ShapeDtypeStruct((b, n_tiles * tm, d), F32),
        compiler_params=_cparams(2),
        name="mix_ffn",
    )(xl, xc, ya, yb, yc, mg, modtab, *weights)


def _rope_tables(t_lat, s_tot):
    half = SWA_HEAD_DIM // 4
    inv = np.power(ROPE_BASE, -np.arange(half, dtype=np.float32) / half).astype(np.float32)
    pos = np.arange(t_lat)
    lane = np.arange(SWA_HEAD_DIM)
    p = np.where(lane[None, :] < 2 * half, (pos // GRID_W)[:, None], (pos % GRID_W)[:, None])
    ang = p.astype(np.float32) * inv[lane % half][None, :]
    cos, sin = np.cos(ang), np.sin(ang)
    upper = (lane % (2 * half)) >= half
    tabs = np.stack([cos, np.where(upper, sin, 0.0), np.where(upper, 0.0, -sin)]).astype(np.float32)
    ident = np.zeros((3, s_tot - t_lat, SWA_HEAD_DIM), np.float32)
    ident[0] = 1.0
    return np.concatenate([tabs, ident], axis=1)


def kernel(x, c, ctx, c_ctx, w_mod, b_mod, norm_mix, w_in, w_gk_fwd, b_gk_fwd, w_gk_bwd, b_gk_bwd,
           gla_norm, sinks, q_norm, w_q_up, kv_norm, w_kv_up, w_pa, w_pb, w_pc, w_o,
           norm_ffn, w_ffn_in, w_ffn_out, final_norm):
    b, t_lat, d = x.shape
    l_ctx = ctx.shape[1]
    s_tot = t_lat + l_ctx
    depth = w_mod.shape[0]
    tm = TOKEN_TILE
    assert t_lat % tm == 0 and l_ctx == tm and t_lat % GRID_W == 0 and d == D_MODEL
    n_lat = t_lat // tm
    n_all = s_tot // tm

    cc = jnp.concatenate([c, c_ctx[None, :], jnp.zeros((16 - b - 1, d), F32)], axis=0)
    mod = _modulation(cc, w_mod, b_mod)
    mod_lat = mod[:, :b].reshape(depth, b, 1, 6, d)
    mod_ctx = jnp.broadcast_to(mod[:, b].reshape(depth, 1, 1, 6, d), (depth, b, 1, 6, d))
    modtab = jnp.pad(jnp.concatenate([mod_lat, mod_ctx], axis=2),
                     ((0, 0), (0, 0), (0, 0), (0, 2), (0, 0)))

    t64 = _rope_tables(t_lat, s_tot)
    ident = np.zeros_like(t64)
    ident[0] = 1.0
    rk = np.concatenate([t64, t64], axis=2)
    rq = rk * np.float32(SWA_HEAD_DIM ** -0.5 * LOG2_E)
    rm = np.concatenate([t64, ident], axis=2)
    rmq = rm * np.float32((MLA_NOPE + MLA_ROPE) ** -0.5 * LOG2_E)
    rq, rk, rm, rmq = (jnp.asarray(a) for a in (rq, rk, rm, rmq))

    w_in_t = jnp.swapaxes(w_in, 1, 2).astype(BF16)

    gla_sums = _gla_sums()

    xs = (x, ctx)
    for l in range(depth):
        last = l == depth - 1
        n_out = n_lat if last else n_all
        hk = GLA_HEADS * GLA_DK
        wg = jnp.zeros((2 * GLA_GATE_RANK, 2 * hk), F32)
        wg = wg.at[0:GLA_GATE_RANK, 0:hk].set(w_gk_fwd[l])
        wg = wg.at[GLA_GATE_RANK:, hk:].set(w_gk_bwd[l])
        bg = jnp.concatenate([b_gk_fwd[l], b_gk_bwd[l]]).reshape(1, 2 * hk)
        gla_weights = (wg.astype(BF16), bg, gla_sums)
        wq = jnp.pad(w_q_up[l].reshape(MLA_Q_RANK, MLA_HEADS, MLA_NOPE + MLA_ROPE),
                     ((0, 0), (0, 0), (0, MLA_QK_PAD - MLA_NOPE - MLA_ROPE))
                     ).reshape(MLA_Q_RANK, MLA_HEADS * MLA_QK_PAD).astype(BF16)

        wkv = w_kv_up[l].reshape(MLA_KV_RANK, MLA_HEADS, MLA_NOPE + MLA_V)
        wk = wkv[:, :, :MLA_NOPE].reshape(MLA_KV_RANK, MLA_HEADS * MLA_NOPE).astype(BF16)
        wvt = wkv[:, :, MLA_NOPE:].reshape(MLA_KV_RANK, MLA_HEADS * MLA_V).T.astype(BF16)
        mla_weights = (wq, wk, wvt, q_norm[l].reshape(1, -1), kv_norm[l].reshape(1, -1))

        later_weights = [(w, l) for w in (w_pa, w_pb, w_pc, w_o, w_ffn_in, w_ffn_out)]
        (qd, ki, tot, va, ga, qs, kx, vx, mg, qm, kn, kr, vm,
         wa_b, wb_b, wc_b, wo_b, wi_b, wd_b) = _inproj(
            xs, modtab[l], norm_mix[l].reshape(1, d), (w_in_t, l), rq, rk, rm,
            mla_weights, rmq, gla_weights, later_weights, n_lat)
        ya = _gla(qd, ki, tot, va, ga, gla_norm[l].reshape(1, GLA_DV), t_lat)
        yb = _swa(sinks[l], qs, kx, vx, t_lat, n_out)
        yc = _mla_attn(qm, kn, kr, vm, t_lat, n_out)
        xs = _mix_ffn(xs, ya, yb, yc, mg, modtab[l], wa_b, wb_b, wc_b, wo_b,
                      norm_ffn[l].reshape(1, d), wi_b, wd_b,
                      final_norm.reshape(1, d), n_lat, n_out, last)
    return xs
```

```python
import functools

import jax
import jax.numpy as jnp
import numpy as np
from jax import lax
from jax.experimental import pallas as pl
from jax.experimental.pallas import tpu as pltpu

F32 = jnp.float32
BF16 = jnp.bfloat16

D_MODEL = 1024
GRID_W = 64
EPS = 1e-6
ROPE_BASE = 10000.0

GLA_HEADS = 4
GLA_DK = 128
GLA_DV = 256
GLA_GATE_RANK = 16
GLA_GATE_NORM = 16.0
GLA_CHUNK = 64

SWA_HEADS = 16
SWA_KV_HEADS = 2
SWA_GROUP = SWA_HEADS // SWA_KV_HEADS
SWA_HEAD_DIM = 64
WINDOW = 128
SWA_VT_ROWS = SWA_HEAD_DIM + 16

MLA_HEADS = 8
MLA_Q_RANK = 384
MLA_KV_RANK = 256
MLA_NOPE = 128
MLA_ROPE = 64
MLA_V = 128
MLA_QK_PAD = 256
MLA_VT_ROWS = MLA_V + 16
LOG2_E = 1.4426950408889634

D_FF = -(-(8 * D_MODEL) // (3 * 256)) * 256

IN_SPLITS = (
    GLA_HEADS * GLA_DK, GLA_HEADS * GLA_DK, GLA_HEADS * GLA_DV, GLA_HEADS * GLA_DV,
    GLA_GATE_RANK, GLA_GATE_RANK,
    SWA_HEADS * SWA_HEAD_DIM, SWA_KV_HEADS * SWA_HEAD_DIM, SWA_KV_HEADS * SWA_HEAD_DIM,
    MLA_Q_RANK, MLA_KV_RANK, MLA_ROPE,
    3 * D_MODEL,
)

LANES = 128
TOKEN_TILE = 256
(ROW_QA, ROW_KA, ROW_VA, ROW_GA, ROW_GKF, ROW_GKB, ROW_QS, ROW_KS, ROW_VS,
 ROW_CQ, ROW_CKV, ROW_KR, ROW_MG, ROW_END) = (int(v) for v in np.cumsum((0,) + IN_SPLITS))
VMEM_LIMIT = 56 * 1024 * 1024


def _cparams(n_axes):
    return pltpu.CompilerParams(
        dimension_semantics=("arbitrary",) * n_axes, vmem_limit_bytes=VMEM_LIMIT)


def _const_spec(shape):
    nd = len(shape)
    return pl.BlockSpec(shape, lambda *_: (0,) * nd, pipeline_mode=pl.Buffered(1))


def _weight_spec(w):
    if isinstance(w, tuple):
        arr, layer = w
        nd = arr.ndim
        spec = pl.BlockSpec((None,) + arr.shape[1:], lambda *_: (layer,) + (0,) * (nd - 1),
                            pipeline_mode=pl.Buffered(1))
        return arr, spec
    return w, _const_spec(w.shape)


def _rope(x, tab_ref):
    return (x * tab_ref[0] + pltpu.roll(x, 16, 1) * tab_ref[1]
            + pltpu.roll(x, LANES - 16, 1) * tab_ref[2])


def _rms(x):
    return x * lax.rsqrt(jnp.mean(x * x, axis=-1, keepdims=True) + EPS)


def _dot(a, b):
    return jnp.dot(a, b, preferred_element_type=F32)


def _dot_nt(a, b):
    return lax.dot_general(a, b, (((1,), (1,)), ((), ())), preferred_element_type=F32)


def _dot_tn(a, b):
    return lax.dot_general(a, b, (((0,), (0,)), ((), ())), preferred_element_type=F32)


def _mod_kernel(c_ref, w_ref, b_ref, o_ref):
    c = c_ref[...]
    a = (c * jax.nn.sigmoid(c)).astype(BF16)
    o_ref[0] = _dot(a, w_ref[0].astype(BF16)) + b_ref[0]


def _modulation(cc, w_mod, b_mod):
    depth, d, n = w_mod.shape
    tn = 1536
    return pl.pallas_call(
        _mod_kernel,
        grid=(depth, n // tn),
        in_specs=[
            pl.BlockSpec(cc.shape, lambda l, j: (0, 0)),
            pl.BlockSpec((1, d, tn), lambda l, j: (l, 0, j)),
            pl.BlockSpec((1, 1, tn), lambda l, j: (l, 0, j)),
        ],
        out_specs=pl.BlockSpec((1, cc.shape[0], tn), lambda l, j: (l, 0, j)),
        out_shape=jax.ShapeDtypeStruct((depth, cc.shape[0], n), F32),
        compiler_params=_cparams(2),
        name="modulation",
    )(cc, w_mod, b_mod.reshape(depth, 1, n))


def _mla_keys_values(ckv, tail, wk_ref, wvt_ref, kvn_ref, k_ref, kr_ref, vt_ref):
    lo = lax.broadcasted_iota(jnp.int32, tail.shape, 1) < MLA_ROPE
    kr_ref[0] = jnp.where(lo, tail, 0.0).astype(BF16)
    ckvn = (_rms(ckv) * kvn_ref[...]).astype(BF16)
    k_ref[0] = _dot(ckvn, wk_ref[...]).astype(BF16)
    vt = _dot_nt(wvt_ref[...], ckvn)
    ones = jnp.ones((MLA_VT_ROWS - MLA_V, vt.shape[1]), BF16)
    for h in range(MLA_HEADS):
        vt_ref[0, h, 0:MLA_V, :] = vt[h * MLA_V:(h + 1) * MLA_V, :].astype(BF16)
        vt_ref[0, h, MLA_V:MLA_VT_ROWS, :] = ones


def _mla_queries(cq, wq_ref, qn_ref, rq_ref, q_ref):
    scale = (MLA_NOPE + MLA_ROPE) ** -0.5 * LOG2_E
    qf = _dot((_rms(cq) * qn_ref[...]).astype(BF16), wq_ref[...])
    for h in range(MLA_HEADS):
        c0 = h * MLA_QK_PAD
        q_ref[0, :, c0:c0 + LANES] = (qf[:, c0:c0 + LANES] * scale).astype(BF16)
        q_ref[0, :, c0 + LANES:c0 + 2 * LANES] = _rope(
            qf[:, c0 + LANES:c0 + 2 * LANES], rq_ref).astype(BF16)


def _gla_decays(q, k, gates, wg_ref, bg_ref, sum_ref, qd_ref, ki_ref, tot_ref):
    hk = GLA_HEADS * GLA_DK
    c = GLA_CHUNK
    z = _dot(gates.astype(BF16), wg_ref[...]) + bg_ref[...]
    la = (jnp.minimum(z, 0.0) - jnp.log1p(jnp.exp(-jnp.abs(z)))) * (1.0 / GLA_GATE_NORM)
    hi = la.astype(BF16)
    lo = (la - hi.astype(F32)).astype(BF16)
    totals = []
    for d in range(2):
        sl = slice(d * hk, (d + 1) * hk)
        cr = _dot(sum_ref[d], jnp.concatenate([hi[:, sl], lo[:, sl]], axis=1))
        cum = cr[:, 0:hk] + cr[:, hk:]
        if q is not None:
            qd_ref[0, :, sl] = (q * (GLA_DK ** -0.5) * jnp.exp(cum)).astype(BF16)
        ki_ref[0, :, sl] = (k * jnp.exp(-cum)).astype(BF16)
        last = (c - 1) if d == 0 else 0
        ends = [cum[j * c + last:j * c + last + 1] for j in range(TOKEN_TILE // c)]
        totals.append(jnp.concatenate(ends + [jnp.zeros((8 - len(ends), hk), F32)], axis=0))
    tot_ref[0, 0] = jnp.concatenate(totals, axis=1)


INPROJ_INPUTS = 17
INPROJ_OUTPUTS = 13


def _inproj_kernel(*refs, n_lat, n_cast, ctx_queries):
    n_in = INPROJ_INPUTS + n_cast
    body = functools.partial(_inproj_body, *refs[:INPROJ_INPUTS],
                             *refs[n_in:n_in + INPROJ_OUTPUTS], n_lat=n_lat)
    if ctx_queries:
        body(queries=True)
    else:
        pl.when(pl.program_id(1) < n_lat)(functools.partial(body, queries=True))
        pl.when(pl.program_id(1) >= n_lat)(functools.partial(body, queries=False))
    for src, dst in zip(refs[INPROJ_INPUTS:n_in], refs[n_in + INPROJ_OUTPUTS:]):
        dst[...] = src[...].astype(BF16)


def _inproj_body(xl_ref, xc_ref, mod_ref, gain_ref, w_ref, rq_ref, rk_ref, rm_ref,
                 wq_ref, wk_ref, wmvt_ref, qn_ref, kvn_ref, rmq_ref, wg_ref, bg_ref, sum_ref,
                 qd_ref, ki_ref, tot_ref, va_ref, ga_ref, qs_ref, kx_ref, vx_ref, mg_ref,
                 mq_ref, mk_ref, mkr_ref, mvt_ref, *, n_lat, queries):
    x = _pick_tile(xl_ref, xc_ref, n_lat)
    y = _rms(x) * gain_ref[...]
    h = (y * (1.0 + mod_ref[0, 0, 1:2, :]) + mod_ref[0, 0, 0:1, :]).astype(BF16)
    hk = GLA_HEADS * GLA_DK

    def proj(c0, c1):
        return _dot_nt(h, w_ref[c0:c1, :])

    r = proj(ROW_CQ if queries else ROW_CKV, ROW_KR + LANES)
    kv0 = r.shape[1] - MLA_KV_RANK - LANES
    _mla_keys_values(r[:, kv0:kv0 + MLA_KV_RANK], _rope(r[:, kv0 + MLA_KV_RANK:], rm_ref),
                     wk_ref, wmvt_ref, kvn_ref, mk_ref, mkr_ref, mvt_ref)

    if queries:
        _mla_queries(r[:, 0:MLA_Q_RANK], wq_ref, qn_ref, rmq_ref, mq_ref)
        r = proj(ROW_GA, ROW_QS)
        g = r[:, 0:ROW_GKF - ROW_GA]
        ga_ref[0] = (g * jax.nn.sigmoid(g)).astype(BF16)
        qk = proj(ROW_QA, ROW_VA)
        _gla_decays(qk[:, 0:hk], qk[:, hk:], r[:, ROW_GKF - ROW_GA:], wg_ref, bg_ref, sum_ref,
                    qd_ref, ki_ref, tot_ref)
        r = proj(ROW_QS, ROW_KS)
        for j in range(1024 // LANES):
            sl = slice(j * LANES, (j + 1) * LANES)
            qs_ref[0, :, sl] = _rope(r[:, sl], rq_ref).astype(BF16)
        for j in range(3):
            r = proj(ROW_MG + j * 1024, ROW_MG + (j + 1) * 1024)
            mg_ref[0, :, j * 1024:(j + 1) * 1024] = jax.nn.sigmoid(r).astype(BF16)
    else:
        _gla_decays(None, proj(ROW_KA, ROW_VA), proj(ROW_GKF, ROW_QS), wg_ref, bg_ref, sum_ref,
                    qd_ref, ki_ref, tot_ref)
    va_ref[0] = proj(ROW_VA, ROW_GA).astype(BF16)

    kk = _rope(proj(ROW_KS, ROW_VS), rk_ref)
    kx = pltpu.roll(kk, 64, 1)
    lo = lax.broadcasted_iota(jnp.int32, kk.shape, 1) < 64
    zero = jnp.zeros_like(kk)
    blocks = (jnp.where(lo, kk, zero), jnp.where(lo, zero, kx),
              jnp.where(lo, kx, zero), jnp.where(lo, zero, kk))
    for j, blk in enumerate(blocks):
        kx_ref[0, :, j * LANES:(j + 1) * LANES] = blk.astype(BF16)

    vt = _dot_nt(w_ref[ROW_VS:ROW_CQ, :], h).astype(BF16)
    hd = SWA_HEAD_DIM
    ones = jnp.ones((SWA_VT_ROWS - hd, vt.shape[1]), BF16)
    for g in range(SWA_KV_HEADS):
        vx_ref[0, g * SWA_VT_ROWS:g * SWA_VT_ROWS + hd, :] = vt[g * hd:(g + 1) * hd, :]
        vx_ref[0, g * SWA_VT_ROWS + hd:(g + 1) * SWA_VT_ROWS, :] = ones


def _cast_rows(rows, steps):
    return next(r for r in range(16, rows + 1, 16) if rows % r == 0 and rows // r <= steps)


def _inproj(xs, modtab, gain, w, rq, rk, rm, mla_weights, rmq, gla_weights, casts, n_lat,
            ctx_queries):
    (xl, xc), x_specs = _residual_inputs(xs, n_lat)
    b, _, d = xl.shape
    tm = TOKEN_TILE
    nt = n_lat + 1
    s = nt * tm
    row = lambda bi, i: (bi, i, 0)
    tab = pl.BlockSpec((3, tm, LANES), lambda bi, i: (0, i, 0))
    vx_rows = SWA_KV_HEADS * SWA_VT_ROWS
    gla_w = 2 * GLA_HEADS * GLA_DK
    rows_out = lambda n, dt: (pl.BlockSpec((1, tm, n), row), jax.ShapeDtypeStruct((b, s, n), dt))
    if ctx_queries:
        q_out = rows_out
    else:
        q_out = lambda n, dt: (
            pl.BlockSpec((1, tm, n), lambda bi, i: (bi, jnp.minimum(i, n_lat - 1), 0)),
            jax.ShapeDtypeStruct((b, n_lat * tm, n), dt))
    outs = [q_out(gla_w, BF16), rows_out(gla_w, BF16),
            (pl.BlockSpec((1, 1, 8, gla_w), lambda bi, i: (bi, i, 0, 0)),
             jax.ShapeDtypeStruct((b, nt, 8, gla_w), F32)),
            rows_out(1024, BF16), q_out(1024, BF16), q_out(1024, BF16),
            rows_out(SWA_KV_HEADS * 2 * LANES, BF16),
            (pl.BlockSpec((1, vx_rows, tm), lambda bi, i: (bi, 0, i)),
             jax.ShapeDtypeStruct((b, vx_rows, s), BF16)),
            q_out(3072, BF16),
            q_out(MLA_HEADS * MLA_QK_PAD, BF16), rows_out(MLA_HEADS * MLA_NOPE, BF16),
            rows_out(LANES, BF16),
            (pl.BlockSpec((1, MLA_HEADS, MLA_VT_ROWS, tm), lambda bi, i: (bi, 0, 0, i)),
             jax.ShapeDtypeStruct((b, MLA_HEADS, MLA_VT_ROWS, s), BF16))]
    w, w_spec = _weight_spec(w)
    cast_in, cast_out = [], []
    for arr, layer in casts:
        rows, cols = arr.shape[1:]
        blk = _cast_rows(rows, b * nt)
        last = rows // blk - 1
        cast_in.append(pl.BlockSpec(
            (None, blk, cols),
            lambda bi, i, layer=layer, last=last: (layer, jnp.minimum(bi * nt + i, last), 0)))
        cast_out.append((pl.BlockSpec(
            (blk, cols), lambda bi, i, last=last: (jnp.minimum(bi * nt + i, last), 0)),
            jax.ShapeDtypeStruct((rows, cols), BF16)))
    outs = outs + cast_out
    return pl.pallas_call(
        functools.partial(_inproj_kernel, n_lat=n_lat, n_cast=len(casts), ctx_queries=ctx_queries),
        grid=(b, nt),
        in_specs=x_specs + [
            pl.BlockSpec((1, 1, 8, d), lambda bi, i: (bi, i // n_lat, 0, 0)),
            _const_spec((1, d)),
            w_spec,
            tab, tab, tab,
        ] + [_const_spec(a.shape) for a in mla_weights] + [tab]
        + [_const_spec(a.shape) for a in gla_weights] + cast_in,
        out_specs=[o[0] for o in outs],
        out_shape=[o[1] for o in outs],
        compiler_params=_cparams(2),
        name="inproj",
    )(xl, xc, modtab, gain, w, rq, rk, rm, *mla_weights, rmq, *gla_weights, *[a for a, _ in casts])


def _gla_kernel(qdf_ref, qdb_ref, kif_ref, kib_ref, v_ref, g_ref, totf_ref, totb_ref, gn_ref,
                sel_ref, o_ref,
                oi_ref, af_ref, ab_ref, st_ref, decf_ref, decb_ref, keepf_ref, keepb_ref,
                *, n_lat, with_ctx):
    c = GLA_CHUNK
    tm = TOKEN_TILE
    per = tm // c

    ri = lax.broadcasted_iota(jnp.int32, (tm, tm), 0)
    ci = lax.broadcasted_iota(jnp.int32, (tm, tm), 1)
    same = (ri // c) == (ci // c)
    keepf_ref[...] = jnp.where(same & (ci <= ri), 1.0, 0.0)
    keepb_ref[...] = jnp.where(same & (ci > ri), 1.0, 0.0)

    def tile_rows(t):
        return pl.ds(pl.multiple_of(t * tm, tm), tm)

    def in_chunk(t, qd_ref, ki_ref, tot_ref, a_ref, dec_ref, scores=True):
        rows = tile_rows(t)
        tot = tot_ref[0, t]
        hi = tot.astype(BF16)
        lo = (tot - hi.astype(F32)).astype(BF16)
        dec_ref[t] = jnp.exp(_dot_tn(jnp.concatenate([hi, lo], axis=0), sel_ref[...]))
        v = v_ref[0, rows, :]
        k_inv = ki_ref[0, rows, :]
        for j in range(per):
            a_ref[t * per + j] = _dot_tn(k_inv[j * c:(j + 1) * c], v[j * c:(j + 1) * c])
        return _dot_nt(qd_ref[0, rows, :], k_inv) if scores else None

    def bulk_body(t, carry):
        s_f = in_chunk(t, qdf_ref, kif_ref, totf_ref, af_ref, decf_ref)
        s_b = in_chunk(t, qdb_ref, kib_ref, totb_ref, ab_ref, decb_ref)
        scores = jnp.where(keepf_ref[...] > 0.5, s_f, jnp.where(keepb_ref[...] > 0.5, s_b, 0.0))
        oi_ref[tile_rows(t), :] = _dot(scores.astype(BF16), v_ref[0, tile_rows(t), :])
        return carry

    if with_ctx:
        lax.fori_loop(0, n_lat + 1, bulk_body, 0, unroll=3)
    else:
        lax.fori_loop(0, n_lat, bulk_body, 0, unroll=4)
        in_chunk(n_lat, qdf_ref, kif_ref, totf_ref, af_ref, decf_ref, scores=False)
        in_chunk(n_lat, qdb_ref, kib_ref, totb_ref, ab_ref, decb_ref, scores=False)

    def scan_tile(t, order, state, a_ref, dec_ref, st_rows):
        for j in order:
            st_ref[t * per + j, st_rows, :] = state.astype(BF16)
            dec = dec_ref[t, :, j * LANES:(j + 1) * LANES]
            state = jnp.concatenate([dec] * (GLA_DV // LANES), axis=1) * (state + a_ref[t * per + j])
        return state

    fwd_order = range(per)
    bwd_order = range(per - 1, -1, -1)
    zero = jnp.zeros((GLA_DK, GLA_DV), F32)
    f_rows, b_rows = slice(0, GLA_DK), slice(GLA_DK, 2 * GLA_DK)
    s_f = scan_tile(n_lat, fwd_order, zero, af_ref, decf_ref, f_rows)
    s_b = scan_tile(n_lat, bwd_order, zero, ab_ref, decb_ref, b_rows)
    lax.fori_loop(0, n_lat, lambda i, s: scan_tile(
        i, fwd_order, s, af_ref, decf_ref, f_rows), s_f)
    lax.fori_loop(0, n_lat, lambda i, s: scan_tile(
        n_lat - 1 - i, bwd_order, s, ab_ref, decb_ref, b_rows), s_b)

    def post_body(t, carry):
        for j in range(per):
            rows = pl.ds(pl.multiple_of(t * tm + j * c, c), c)
            q_both = jnp.concatenate([qdf_ref[0, rows, :], qdb_ref[0, rows, :]], axis=1)
            o = oi_ref[rows, :] + _dot(q_both, st_ref[t * per + j])
            y = _rms(o) * gn_ref[...]
            o_ref[0, rows, :] = (y * g_ref[0, rows, :].astype(F32)).astype(BF16)
        return carry

    if with_ctx:
        lax.fori_loop(0, n_lat + 1, post_body, 0, unroll=3)
    else:
        lax.fori_loop(0, n_lat, post_body, 0, unroll=4)


def _gla_sums():
    tm, c = TOKEN_TILE, GLA_CHUNK
    r = np.arange(tm)[:, None]
    col = np.arange(tm)[None, :]
    same = (r // c) == (col // c)
    return jnp.asarray(np.stack([same & (col <= r), same & (col >= r)]).astype(np.float32), BF16)


def _gla_selector():
    per = TOKEN_TILE // GLA_CHUNK
    sel = np.zeros((16, per * LANES), np.float32)
    for j in range(per):
        sel[j, j * LANES:(j + 1) * LANES] = 1.0
        sel[8 + j, j * LANES:(j + 1) * LANES] = 1.0
    return jnp.asarray(sel, BF16)


def _gla(qd, ki, tot, va, ga, gn, t_lat):
    b, s, _ = va.shape
    s_q = qd.shape[1]
    tm = TOKEN_TILE
    n_lat = t_lat // tm
    n_tiles = s // tm
    n_chunks = s // GLA_CHUNK
    h = GLA_HEADS
    sel = _gla_selector()
    fwd = lambda bi, hi: (bi, 0, hi)
    bwd = lambda bi, hi: (bi, 0, h + hi)
    return pl.pallas_call(
        functools.partial(_gla_kernel, n_lat=n_lat, with_ctx=s_q == s),
        grid=(b, h),
        in_specs=[
            pl.BlockSpec((1, s_q, GLA_DK), fwd), pl.BlockSpec((1, s_q, GLA_DK), bwd),
            pl.BlockSpec((1, s, GLA_DK), fwd), pl.BlockSpec((1, s, GLA_DK), bwd),
            pl.BlockSpec((1, s, GLA_DV), fwd), pl.BlockSpec((1, s_q, GLA_DV), fwd),
            pl.BlockSpec((1, n_tiles, 8, GLA_DK), lambda bi, hi: (bi, 0, 0, hi)),
            pl.BlockSpec((1, n_tiles, 8, GLA_DK), lambda bi, hi: (bi, 0, 0, h + hi)),
            pl.BlockSpec((1, GLA_DV), lambda bi, hi: (0, 0)),
            _const_spec(sel.shape),
        ],
        out_specs=pl.BlockSpec((1, s_q, GLA_DV), fwd),
        out_shape=jax.ShapeDtypeStruct((b, s_q, h * GLA_DV), BF16),
        scratch_shapes=[
            pltpu.VMEM((s, GLA_DV), F32),
            pltpu.VMEM((n_chunks, GLA_DK, GLA_DV), F32), pltpu.VMEM((n_chunks, GLA_DK, GLA_DV), F32),
            pltpu.VMEM((n_chunks, 2 * GLA_DK, GLA_DV), BF16),
            pltpu.VMEM((n_tiles, GLA_DK, tm // GLA_CHUNK * LANES), F32),
            pltpu.VMEM((n_tiles, GLA_DK, tm // GLA_CHUNK * LANES), F32),
            pltpu.VMEM((tm, tm), F32), pltpu.VMEM((tm, tm), F32),
        ],
        compiler_params=_cparams(2),
        name="gla",
    )(qd, qd, ki, ki, va, ga, tot, tot, gn, sel)


def _swa_kernel(sink_ref, q_ref, kx_ref, vx_ref, bias_ref, o_ref, s0_ref, s1_ref, m0_ref, m1_ref,
                *, n_lat, t_lat, with_ctx):
    tq = TOKEN_TILE
    span = tq + 2 * WINDOW
    hd = SWA_HEAD_DIM
    g = pl.program_id(1)
    ka, kb = slice(0, LANES), slice(LANES, 2 * LANES)

    def q_rows(i):
        return pl.ds(pl.multiple_of(i * tq, tq), tq)

    def sink_of(head):
        return sink_ref[g * SWA_GROUP + head] * LOG2_E

    def keys(i):
        ws = pl.multiple_of(jnp.clip(i * tq - WINDOW, 0, t_lat - span), WINDOW)
        win = pl.ds(ws, span)
        kc = [jnp.concatenate([kx_ref[0, win, c], kx_ref[0, t_lat:, c]], axis=0) for c in (ka, kb)]
        vc = jnp.concatenate([vx_ref[0, :, win], vx_ref[0, :, t_lat:]], axis=1)
        return kc, vc, (i * tq - ws) // WINDOW

    def scores(i, kc, bias_idx, head, s_ref, m_ref):
        qp = q_ref[0, q_rows(i), (head // 2) * LANES:(head // 2 + 1) * LANES]
        s_t = _dot_nt(kc[head % 2], qp)
        top = s_t[0:span] + bias_ref[bias_idx]
        bot = s_t[span:]
        s_ref[0:span, :] = top
        s_ref[span:, :] = bot
        m = jnp.maximum(jnp.max(top, axis=0, keepdims=True), jnp.max(bot, axis=0, keepdims=True))
        m_ref[...] = jnp.broadcast_to(jnp.maximum(m, sink_of(head)), m_ref.shape)

    def normalise(acc, m, head):
        return acc[0:hd] / (acc[hd:hd + 1] + jnp.exp2(sink_of(head) - m))

    def values(vc, head, s_ref, m_ref):
        m = m_ref[0:1, :]
        p_t = jnp.exp2(s_ref[...] - m).astype(BF16)
        return normalise(_dot(vc, p_t), m, head)

    def store(i, pair, o_a, o_b):
        o_t = jnp.concatenate([o_a, o_b], axis=0)
        o_ref[0, q_rows(i), pair * LANES:(pair + 1) * LANES] = o_t.T.astype(BF16)

    def scores_all(i, s_ref, m_ref):
        kc, _, bias_idx = keys(i)
        for head in range(SWA_GROUP):
            scores(i, kc, bias_idx, head, s_ref.at[head], m_ref.at[head])

    def values_all(i, s_ref, m_ref):
        _, vc, _ = keys(i)
        for pair in range(SWA_GROUP // 2):
            a, b = 2 * pair, 2 * pair + 1
            store(i, pair, values(vc, a, s_ref.at[a], m_ref.at[a]),
                  values(vc, b, s_ref.at[b], m_ref.at[b]))

    scores_all(0, s0_ref, m0_ref)

    def body(j, carry):
        scores_all(2 * j + 1, s1_ref, m1_ref)
        values_all(2 * j, s0_ref, m0_ref)
        scores_all(2 * j + 2, s0_ref, m0_ref)
        values_all(2 * j + 1, s1_ref, m1_ref)
        return carry

    lax.fori_loop(0, n_lat // 2 - 1, body, 0)
    scores_all(n_lat - 1, s1_ref, m1_ref)
    values_all(n_lat - 2, s0_ref, m0_ref)
    values_all(n_lat - 1, s1_ref, m1_ref)

    if with_ctx:
        rows = slice(t_lat, t_lat + tq)
        for pair in range(SWA_GROUP // 2):
            qp = q_ref[0, rows, pair * LANES:(pair + 1) * LANES]
            outs = []
            for head, kcols in ((2 * pair, ka), (2 * pair + 1, kb)):
                s_t = _dot_nt(kx_ref[0, t_lat:, kcols], qp)
                m = jnp.maximum(jnp.max(s_t, axis=0, keepdims=True), sink_of(head))
                p_t = jnp.exp2(s_t - m).astype(BF16)
                outs.append(normalise(_dot(vx_ref[0, :, t_lat:], p_t), m, head))
            o_t = jnp.concatenate(outs, axis=0)
            o_ref[0, rows, pair * LANES:(pair + 1) * LANES] = o_t.T.astype(BF16)


def _swa_bias():
    tq, span = TOKEN_TILE, TOKEN_TILE + 2 * WINDOW
    r = np.arange(span)[:, None]
    c = np.arange(tq)[None, :]
    tabs = [np.where(np.abs(r - off - c) <= WINDOW, 0.0, -1e30) for off in (0, WINDOW, 2 * WINDOW)]
    return jnp.asarray(np.stack(tabs).astype(np.float32))


def _swa(sinks, qs, kx, vx, t_lat, n_tiles):
    b, s, _ = kx.shape
    tq = TOKEN_TILE
    gw = SWA_GROUP * SWA_HEAD_DIM
    n_keys = tq + 2 * WINDOW + (s - t_lat)
    bias = _swa_bias()
    n_lat = t_lat // tq
    return pl.pallas_call(
        functools.partial(_swa_kernel, n_lat=n_lat, t_lat=t_lat, with_ctx=n_tiles > n_lat),
        grid=(b, SWA_KV_HEADS),
        in_specs=[
            pl.BlockSpec(memory_space=pltpu.SMEM),
            pl.BlockSpec((1, qs.shape[1], gw), lambda bi, gi: (bi, 0, gi)),
            pl.BlockSpec((1, s, 2 * LANES), lambda bi, gi: (bi, 0, gi)),
            pl.BlockSpec((1, SWA_VT_ROWS, s), lambda bi, gi: (bi, gi, 0)),
            _const_spec(bias.shape),
        ],
        out_specs=pl.BlockSpec((1, n_tiles * tq, gw), lambda bi, gi: (bi, 0, gi)),
        out_shape=jax.ShapeDtypeStruct((b, n_tiles * tq, SWA_HEADS * SWA_HEAD_DIM), BF16),
        scratch_shapes=[pltpu.VMEM((SWA_GROUP, n_keys, tq), F32), pltpu.VMEM((SWA_GROUP, n_keys, tq), F32),
                        pltpu.VMEM((SWA_GROUP, 8, tq), F32), pltpu.VMEM((SWA_GROUP, 8, tq), F32)],
        compiler_params=_cparams(2),
        name="swa",
    )(sinks, qs, kx, vx, bias)


def _mla_attn_kernel(q_ref, kn_ref, kr_ref, vt_ref, o_ref, k_ref, s0_ref, s1_ref, m0_ref, m1_ref,
                     *, n_lat, t_lat, with_ctx):
    tq = TOKEN_TILE
    k_ref[:, 0:MLA_NOPE] = kn_ref[0]
    k_ref[:, MLA_NOPE:] = kr_ref[0]

    def q_rows(i):
        return pl.ds(pl.multiple_of(i * tq, tq), tq)

    def finish(acc, rows):
        o_t = acc[0:MLA_V] / acc[MLA_V:MLA_V + 1]
        o_ref[0, rows, :] = o_t.T.astype(BF16)

    def scores(i, s_ref, m_ref):
        q = q_ref[0, q_rows(i), :]
        half = k_ref.shape[0] // 2
        lo = _dot_nt(k_ref[0:half, :], q)
        hi = _dot_nt(k_ref[half:, :], q)
        s_ref[0:half, :] = lo
        s_ref[half:, :] = hi
        m = jnp.maximum(jnp.max(lo, axis=0, keepdims=True), jnp.max(hi, axis=0, keepdims=True))
        m_ref[...] = jnp.broadcast_to(m, m_ref.shape)

    def values(i, s_ref, m_ref):
        p_t = jnp.exp2(s_ref[...] - m_ref[0:1, :]).astype(BF16)
        finish(_dot(vt_ref[0, 0], p_t), q_rows(i))

    scores(0, s0_ref, m0_ref)

    def body(j, carry):
        scores(2 * j + 1, s1_ref, m1_ref)
        values(2 * j, s0_ref, m0_ref)
        scores(2 * j + 2, s0_ref, m0_ref)
        values(2 * j + 1, s1_ref, m1_ref)
        return carry

    lax.fori_loop(0, n_lat // 2 - 1, body, 0)
    scores(n_lat - 1, s1_ref, m1_ref)
    values(n_lat - 2, s0_ref, m0_ref)
    values(n_lat - 1, s1_ref, m1_ref)

    if with_ctx:
        s_t = _dot_nt(k_ref[t_lat:, :], q_ref[0, t_lat:, :])
        p_t = jnp.exp2(s_t - jnp.max(s_t, axis=0, keepdims=True)).astype(BF16)
        finish(_dot(vt_ref[0, 0, :, t_lat:], p_t), slice(t_lat, t_lat + tq))


def _mla_attn(q, kn, kr, vt, t_lat, n_tiles):
    b, s, _ = kn.shape
    tq = TOKEN_TILE
    n_lat = t_lat // tq
    return pl.pallas_call(
        functools.partial(_mla_attn_kernel, n_lat=n_lat, t_lat=t_lat, with_ctx=n_tiles > n_lat),
        grid=(b, MLA_HEADS),
        in_specs=[
            pl.BlockSpec((1, q.shape[1], MLA_QK_PAD), lambda bi, hi: (bi, 0, hi)),
            pl.BlockSpec((1, s, MLA_NOPE), lambda bi, hi: (bi, 0, hi)),
            pl.BlockSpec((1, s, LANES), lambda bi, hi: (bi, 0, 0)),
            pl.BlockSpec((1, 1, MLA_VT_ROWS, s), lambda bi, hi: (bi, hi, 0, 0)),
        ],
        out_specs=pl.BlockSpec((1, n_tiles * tq, MLA_V), lambda bi, hi: (bi, 0, hi)),
        out_shape=jax.ShapeDtypeStruct((b, n_tiles * tq, MLA_HEADS * MLA_V), BF16),
        scratch_shapes=[pltpu.VMEM((s, MLA_QK_PAD), BF16),
                        pltpu.VMEM((s, tq), F32), pltpu.VMEM((s, tq), F32),
                        pltpu.VMEM((8, tq), F32), pltpu.VMEM((8, tq), F32)],
        compiler_params=_cparams(2),
        name="mla_attn",
    )(q, kn, kr, vt)


def _residual_inputs(xs, n_lat):
    tm = TOKEN_TILE
    if isinstance(xs, tuple):
        lat, ctx, ctx_block = xs[0], xs[1], 0
    else:
        lat, ctx, ctx_block = xs, xs, n_lat
    d = lat.shape[-1]
    specs = [pl.BlockSpec((1, tm, d), lambda bi, i: (bi, jnp.minimum(i, n_lat - 1), 0)),
             pl.BlockSpec((1, tm, d), lambda bi, i: (bi, ctx_block, 0))]
    return (lat, ctx), specs


def _pick_tile(xl_ref, xc_ref, n_lat):
    return jnp.where(pl.program_id(1) < n_lat, xl_ref[0], xc_ref[0])


def _mix_ffn_kernel(xl_ref, xc_ref, ya_ref, yb_ref, yc_ref, mg_ref, mod_ref,
                    wa_ref, wb_ref, wc_ref, wo_ref, gain_ref, wi_ref, wd_ref, fin_ref, o_ref,
                    *, n_lat, final):
    d = D_MODEL
    u = mg_ref[0, :, 0:d].astype(F32) * _dot(ya_ref[0], wa_ref[...])
    u += mg_ref[0, :, d:2 * d].astype(F32) * _dot(yb_ref[0], wb_ref[...])
    u += mg_ref[0, :, 2 * d:3 * d].astype(F32) * _dot(yc_ref[0], wc_ref[...])
    x = _pick_tile(xl_ref, xc_ref, n_lat) + mod_ref[0, 0, 2:3, :] * _dot(u.astype(BF16), wo_ref[...])

    y = _rms(x) * gain_ref[...]
    h = (y * (1.0 + mod_ref[0, 0, 4:5, :]) + mod_ref[0, 0, 3:4, :]).astype(BF16)
    gate = _dot(h, wi_ref[:, 0:D_FF])
    up = _dot(h, wi_ref[:, D_FF:2 * D_FF])
    a = (gate * jax.nn.sigmoid(gate) * up).astype(BF16)
    x = x + mod_ref[0, 0, 5:6, :] * _dot(a, wd_ref[...])
    if final:
        x = _rms(x) * fin_ref[...]
    o_ref[0] = x


def _mix_ffn(xs, ya, yb, yc, mg, modtab, wa, wb, wc, wo, gain, wi, wd, fin, n_lat, n_tiles, final):
    (xl, xc), x_specs = _residual_inputs(xs, n_lat)
    b, _, d = xl.shape
    tm = TOKEN_TILE
    row = lambda bi, i: (bi, i, 0)
    weights, w_specs = zip(*[_weight_spec(w) for w in (wa, wb, wc, wo, gain, wi, wd, fin)])
    return pl.pallas_call(
        functools.partial(_mix_ffn_kernel, n_lat=n_lat, final=final),
        grid=(b, n_tiles),
        in_specs=x_specs + [
            pl.BlockSpec((1, tm, d), row), pl.BlockSpec((1, tm, d), row),
            pl.BlockSpec((1, tm, d), row), pl.BlockSpec((1, tm, 3 * d), row),
            pl.BlockSpec((1, 1, 8, d), lambda bi, i: (bi, i // n_lat, 0, 0)),
        ] + list(w_specs),
        out_specs=pl.BlockSpec((1, tm, d), row),
        out_shape=jax.ShapeDtypeStruct((b, n_tiles * tm, d), F32),
        compiler_params=_cparams(2),
        name="mix_ffn",
    )(xl, xc, ya, yb, yc, mg, modtab, *weights)


def _rope_tables(t_lat, s_tot):
    half = SWA_HEAD_DIM // 4
    inv = np.power(ROPE_BASE, -np.arange(half, dtype=np.float32) / half).astype(np.float32)
    pos = np.arange(t_lat)
    lane = np.arange(SWA_HEAD_DIM)
    p = np.where(lane[None, :] < 2 * half, (pos // GRID_W)[:, None], (pos % GRID_W)[:, None])
    ang = p.astype(np.float32) * inv[lane % half][None, :]
    cos, sin = np.cos(ang), np.sin(ang)
    upper = (lane % (2 * half)) >= half
    tabs = np.stack([cos, np.where(upper, sin, 0.0), np.where(upper, 0.0, -sin)]).astype(np.float32)
    ident = np.zeros((3, s_tot - t_lat, SWA_HEAD_DIM), np.float32)
    ident[0] = 1.0
    return np.concatenate([tabs, ident], axis=1)


def kernel(x, c, ctx, c_ctx, w_mod, b_mod, norm_mix, w_in, w_gk_fwd, b_gk_fwd, w_gk_bwd, b_gk_bwd,
           gla_norm, sinks, q_norm, w_q_up, kv_norm, w_kv_up, w_pa, w_pb, w_pc, w_o,
           norm_ffn, w_ffn_in, w_ffn_out, final_norm):
    b, t_lat, d = x.shape
    l_ctx = ctx.shape[1]
    s_tot = t_lat + l_ctx
    depth = w_mod.shape[0]
    tm = TOKEN_TILE
    assert t_lat % tm == 0 and l_ctx == tm and t_lat % GRID_W == 0 and d == D_MODEL
    n_lat = t_lat // tm
    n_all = s_tot // tm

    cc = jnp.concatenate([c, c_ctx[None, :], jnp.zeros((16 - b - 1, d), F32)], axis=0)
    mod = _modulation(cc, w_mod, b_mod)
    mod_lat = mod[:, :b].reshape(depth, b, 1, 6, d)
    mod_ctx = jnp.broadcast_to(mod[:, b].reshape(depth, 1, 1, 6, d), (depth, b, 1, 6, d))
    modtab = jnp.pad(jnp.concatenate([mod_lat, mod_ctx], axis=2),
                     ((0, 0), (0, 0), (0, 0), (0, 2), (0, 0)))

    t64 = _rope_tables(t_lat, s_tot)
    ident = np.zeros_like(t64)
    ident[0] = 1.0
    rk = np.concatenate([t64, t64], axis=2)
    rq = rk * np.float32(SWA_HEAD_DIM ** -0.5 * LOG2_E)
    rm = np.concatenate([t64, ident], axis=2)
    rmq = rm * np.float32((MLA_NOPE + MLA_ROPE) ** -0.5 * LOG2_E)
    rq, rk, rm, rmq = (jnp.asarray(a) for a in (rq, rk, rm, rmq))

    w_in_t = jnp.swapaxes(w_in, 1, 2).astype(BF16)

    gla_sums = _gla_sums()

    xs = (x, ctx)
    for l in range(depth):
        last = l == depth - 1
        n_out = n_lat if last else n_all
        hk = GLA_HEADS * GLA_DK
        wg = jnp.zeros((2 * GLA_GATE_RANK, 2 * hk), F32)
        wg = wg.at[0:GLA_GATE_RANK, 0:hk].set(w_gk_fwd[l])
        wg = wg.at[GLA_GATE_RANK:, hk:].set(w_gk_bwd[l])
        bg = jnp.concatenate([b_gk_fwd[l], b_gk_bwd[l]]).reshape(1, 2 * hk)
        gla_weights = (wg.astype(BF16), bg, gla_sums)
        wq = jnp.pad(w_q_up[l].reshape(MLA_Q_RANK, MLA_HEADS, MLA_NOPE + MLA_ROPE),
                     ((0, 0), (0, 0), (0, MLA_QK_PAD - MLA_NOPE - MLA_ROPE))
                     ).reshape(MLA_Q_RANK, MLA_HEADS * MLA_QK_PAD).astype(BF16)

        wkv = w_kv_up[l].reshape(MLA_KV_RANK, MLA_HEADS, MLA_NOPE + MLA_V)
        wk = wkv[:, :, :MLA_NOPE].reshape(MLA_KV_RANK, MLA_HEADS * MLA_NOPE).astype(BF16)
        wvt = wkv[:, :, MLA_NOPE:].reshape(MLA_KV_RANK, MLA_HEADS * MLA_V).T.astype(BF16)
        mla_weights = (wq, wk, wvt, q_norm[l].reshape(1, -1), kv_norm[l].reshape(1, -1))

        later_weights = [(w, l) for w in (w_pa, w_pb, w_pc, w_o, w_ffn_in, w_ffn_out)]
        (qd, ki, tot, va, ga, qs, kx, vx, mg, qm, kn, kr, vm,
         wa_b, wb_b, wc_b, wo_b, wi_b, wd_b) = _inproj(
            xs, modtab[l], norm_mix[l].reshape(1, d), (w_in_t, l), rq, rk, rm,
            mla_weights, rmq, gla_weights, later_weights, n_lat, ctx_queries=not last)
        ya = _gla(qd, ki, tot, va, ga, gla_norm[l].reshape(1, GLA_DV), t_lat)
        yb = _swa(sinks[l], qs, kx, vx, t_lat, n_out)
        yc = _mla_attn(qm, kn, kr, vm, t_lat, n_out)
        xs = _mix_ffn(xs, ya, yb, yc, mg, modtab[l], wa_b, wb_b, wc_b, wo_b,
                      norm_ffn[l].reshape(1, d), wi_b, wd_b,
                      final_norm.reshape(1, d), n_lat, n_out, last)
    return xs
```

```python
import functools

import jax
import jax.numpy as jnp
import numpy as np
from jax import lax
from jax.experimental import pallas as pl
from jax.experimental.pallas import tpu as pltpu

F32 = jnp.float32
BF16 = jnp.bfloat16

D_MODEL = 1024
GRID_W = 64
EPS = 1e-6
ROPE_BASE = 10000.0

GLA_HEADS = 4
GLA_DK = 128
GLA_DV = 256
GLA_GATE_RANK = 16
GLA_GATE_NORM = 16.0
GLA_CHUNK = 64

SWA_HEADS = 16
SWA_KV_HEADS = 2
SWA_GROUP = SWA_HEADS // SWA_KV_HEADS
SWA_HEAD_DIM = 64
WINDOW = 128
SWA_VT_ROWS = SWA_HEAD_DIM + 16

MLA_HEADS = 8
MLA_Q_RANK = 384
MLA_KV_RANK = 256
MLA_NOPE = 128
MLA_ROPE = 64
MLA_V = 128
MLA_QK_PAD = 256
MLA_VT_ROWS = MLA_V + 16
LOG2_E = 1.4426950408889634

D_FF = -(-(8 * D_MODEL) // (3 * 256)) * 256

IN_SPLITS = (
    GLA_HEADS * GLA_DK, GLA_HEADS * GLA_DK, GLA_HEADS * GLA_DV, GLA_HEADS * GLA_DV,
    GLA_GATE_RANK, GLA_GATE_RANK,
    SWA_HEADS * SWA_HEAD_DIM, SWA_KV_HEADS * SWA_HEAD_DIM, SWA_KV_HEADS * SWA_HEAD_DIM,
    MLA_Q_RANK, MLA_KV_RANK, MLA_ROPE,
    3 * D_MODEL,
)

LANES = 128
TOKEN_TILE = 256
(ROW_QA, ROW_KA, ROW_VA, ROW_GA, ROW_GKF, ROW_GKB, ROW_QS, ROW_KS, ROW_VS,
 ROW_CQ, ROW_CKV, ROW_KR, ROW_MG, ROW_END) = (int(v) for v in np.cumsum((0,) + IN_SPLITS))
VMEM_LIMIT = 56 * 1024 * 1024


def _cparams(n_axes):
    return pltpu.CompilerParams(
        dimension_semantics=("arbitrary",) * n_axes, vmem_limit_bytes=VMEM_LIMIT)


def _const_spec(shape):
    nd = len(shape)
    return pl.BlockSpec(shape, lambda *_: (0,) * nd, pipeline_mode=pl.Buffered(1))


def _weight_spec(w):
    if isinstance(w, tuple):
        arr, layer = w
        nd = arr.ndim
        spec = pl.BlockSpec((None,) + arr.shape[1:], lambda *_: (layer,) + (0,) * (nd - 1),
                            pipeline_mode=pl.Buffered(1))
        return arr, spec
    return w, _const_spec(w.shape)


def _rope(x, tab_ref):
    return (x * tab_ref[0] + pltpu.roll(x, 16, 1) * tab_ref[1]
            + pltpu.roll(x, LANES - 16, 1) * tab_ref[2])


def _rms(x):
    return x * lax.rsqrt(jnp.mean(x * x, axis=-1, keepdims=True) + EPS)


def _dot(a, b):
    return jnp.dot(a, b, preferred_element_type=F32)


def _dot_nt(a, b):
    return lax.dot_general(a, b, (((1,), (1,)), ((), ())), preferred_element_type=F32)


def _dot_tn(a, b):
    return lax.dot_general(a, b, (((0,), (0,)), ((), ())), preferred_element_type=F32)


def _mod_kernel(c_ref, w_ref, b_ref, o_ref):
    c = c_ref[...]
    a = (c * jax.nn.sigmoid(c)).astype(BF16)
    o_ref[0] = _dot(a, w_ref[0].astype(BF16)) + b_ref[0]


def _modulation(cc, w_mod, b_mod):
    depth, d, n = w_mod.shape
    tn = 1536
    return pl.pallas_call(
        _mod_kernel,
        grid=(depth, n // tn),
        in_specs=[
            pl.BlockSpec(cc.shape, lambda l, j: (0, 0)),
            pl.BlockSpec((1, d, tn), lambda l, j: (l, 0, j)),
            pl.BlockSpec((1, 1, tn), lambda l, j: (l, 0, j)),
        ],
        out_specs=pl.BlockSpec((1, cc.shape[0], tn), lambda l, j: (l, 0, j)),
        out_shape=jax.ShapeDtypeStruct((depth, cc.shape[0], n), F32),
        compiler_params=_cparams(2),
        name="modulation",
    )(cc, w_mod, b_mod.reshape(depth, 1, n))


def _mla_keys_values(ckv, tail, wk_ref, wvt_ref, kvn_ref, k_ref, kr_ref, vt_ref):
    lo = lax.broadcasted_iota(jnp.int32, tail.shape, 1) < MLA_ROPE
    kr_ref[0] = jnp.where(lo, tail, 0.0).astype(BF16)
    ckvn = (_rms(ckv) * kvn_ref[...]).astype(BF16)
    k_ref[0] = _dot(ckvn, wk_ref[...]).astype(BF16)
    vt = _dot_nt(wvt_ref[...], ckvn)
    ones = jnp.ones((MLA_VT_ROWS - MLA_V, vt.shape[1]), BF16)
    for h in range(MLA_HEADS):
        vt_ref[0, h, 0:MLA_V, :] = vt[h * MLA_V:(h + 1) * MLA_V, :].astype(BF16)
        vt_ref[0, h, MLA_V:MLA_VT_ROWS, :] = ones


def _mla_queries(cq, wq_ref, qn_ref, rq_ref, q_ref):
    scale = (MLA_NOPE + MLA_ROPE) ** -0.5 * LOG2_E
    qf = _dot((_rms(cq) * qn_ref[...]).astype(BF16), wq_ref[...])
    for h in range(MLA_HEADS):
        c0 = h * MLA_QK_PAD
        q_ref[0, :, c0:c0 + LANES] = (qf[:, c0:c0 + LANES] * scale).astype(BF16)
        q_ref[0, :, c0 + LANES:c0 + 2 * LANES] = _rope(
            qf[:, c0 + LANES:c0 + 2 * LANES], rq_ref).astype(BF16)


def _gla_decays(q, k, gates, wg_ref, bg_ref, sum_ref, qd_ref, ki_ref, tot_ref):
    hk = GLA_HEADS * GLA_DK
    c = GLA_CHUNK
    z = _dot(gates.astype(BF16), wg_ref[...]) + bg_ref[...]
    la = (jnp.minimum(z, 0.0) - jnp.log1p(jnp.exp(-jnp.abs(z)))) * (1.0 / GLA_GATE_NORM)
    hi = la.astype(BF16)
    lo = (la - hi.astype(F32)).astype(BF16)
    totals = []
    for d in range(2):
        sl = slice(d * hk, (d + 1) * hk)
        cr = _dot(sum_ref[d], jnp.concatenate([hi[:, sl], lo[:, sl]], axis=1))
        cum = cr[:, 0:hk] + cr[:, hk:]
        if q is not None:
            qd_ref[0, :, sl] = (q * (GLA_DK ** -0.5) * jnp.exp(cum)).astype(BF16)
        ki_ref[0, :, sl] = (k * jnp.exp(-cum)).astype(BF16)
        last = (c - 1) if d == 0 else 0
        ends = [cum[j * c + last:j * c + last + 1] for j in range(TOKEN_TILE // c)]
        totals.append(jnp.concatenate(ends + [jnp.zeros((8 - len(ends), hk), F32)], axis=0))
    tot_ref[0, 0] = jnp.concatenate(totals, axis=1)


INPROJ_INPUTS = 17
INPROJ_OUTPUTS = 13


def _inproj_kernel(*refs, n_lat, n_cast, ctx_queries):
    n_in = INPROJ_INPUTS + n_cast
    body = functools.partial(_inproj_body, *refs[:INPROJ_INPUTS],
                             *refs[n_in:n_in + INPROJ_OUTPUTS], n_lat=n_lat)
    if ctx_queries:
        body(queries=True)
    else:
        pl.when(pl.program_id(1) < n_lat)(functools.partial(body, queries=True))
        pl.when(pl.program_id(1) >= n_lat)(functools.partial(body, queries=False))
    for src, dst in zip(refs[INPROJ_INPUTS:n_in], refs[n_in + INPROJ_OUTPUTS:]):
        dst[...] = src[...].astype(BF16)


def _inproj_body(xl_ref, xc_ref, mod_ref, gain_ref, w_ref, rq_ref, rk_ref, rm_ref,
                 wq_ref, wk_ref, wmvt_ref, qn_ref, kvn_ref, rmq_ref, wg_ref, bg_ref, sum_ref,
                 qd_ref, ki_ref, tot_ref, va_ref, ga_ref, qs_ref, kx_ref, vx_ref, mg_ref,
                 mq_ref, mk_ref, mkr_ref, mvt_ref, *, n_lat, queries):
    x = _pick_tile(xl_ref, xc_ref, n_lat)
    y = _rms(x) * gain_ref[...]
    h = (y * (1.0 + mod_ref[0, 0, 1:2, :]) + mod_ref[0, 0, 0:1, :]).astype(BF16)
    hk = GLA_HEADS * GLA_DK

    def proj(c0, c1):
        return _dot_nt(h, w_ref[c0:c1, :])

    r = proj(ROW_CQ if queries else ROW_CKV, ROW_KR + LANES)
    kv0 = r.shape[1] - MLA_KV_RANK - LANES
    if queries:
        _mla_queries(r[:, 0:MLA_Q_RANK], wq_ref, qn_ref, rmq_ref, mq_ref)
    _mla_keys_values(r[:, kv0:kv0 + MLA_KV_RANK], _rope(r[:, kv0 + MLA_KV_RANK:], rm_ref),
                     wk_ref, wmvt_ref, kvn_ref, mk_ref, mkr_ref, mvt_ref)

    if queries:
        r = proj(ROW_GA, ROW_QS)
        g = r[:, 0:ROW_GKF - ROW_GA]
        ga_ref[0] = (g * jax.nn.sigmoid(g)).astype(BF16)
        qk = proj(ROW_QA, ROW_VA)
        _gla_decays(qk[:, 0:hk], qk[:, hk:], r[:, ROW_GKF - ROW_GA:], wg_ref, bg_ref, sum_ref,
                    qd_ref, ki_ref, tot_ref)
    else:
        _gla_decays(None, proj(ROW_KA, ROW_VA), proj(ROW_GKF, ROW_QS), wg_ref, bg_ref, sum_ref,
                    qd_ref, ki_ref, tot_ref)
    va_ref[0] = proj(ROW_VA, ROW_GA).astype(BF16)

    if queries:
        r = proj(ROW_QS, ROW_KS)
        for j in range(1024 // LANES):
            sl = slice(j * LANES, (j + 1) * LANES)
            qs_ref[0, :, sl] = _rope(r[:, sl], rq_ref).astype(BF16)

    kk = _rope(proj(ROW_KS, ROW_VS), rk_ref)
    kx = pltpu.roll(kk, 64, 1)
    lo = lax.broadcasted_iota(jnp.int32, kk.shape, 1) < 64
    zero = jnp.zeros_like(kk)
    blocks = (jnp.where(lo, kk, zero), jnp.where(lo, zero, kx),
              jnp.where(lo, kx, zero), jnp.where(lo, zero, kk))
    for j, blk in enumerate(blocks):
        kx_ref[0, :, j * LANES:(j + 1) * LANES] = blk.astype(BF16)

    vt = _dot_nt(w_ref[ROW_VS:ROW_CQ, :], h).astype(BF16)
    hd = SWA_HEAD_DIM
    ones = jnp.ones((SWA_VT_ROWS - hd, vt.shape[1]), BF16)
    for g in range(SWA_KV_HEADS):
        vx_ref[0, g * SWA_VT_ROWS:g * SWA_VT_ROWS + hd, :] = vt[g * hd:(g + 1) * hd, :]
        vx_ref[0, g * SWA_VT_ROWS + hd:(g + 1) * SWA_VT_ROWS, :] = ones

    if queries:
        for j in range(3):
            r = proj(ROW_MG + j * 1024, ROW_MG + (j + 1) * 1024)
            mg_ref[0, :, j * 1024:(j + 1) * 1024] = jax.nn.sigmoid(r).astype(BF16)


def _cast_rows(rows, steps):
    return next(r for r in range(16, rows + 1, 16) if rows % r == 0 and rows // r <= steps)


def _inproj(xs, modtab, gain, w, rq, rk, rm, mla_weights, rmq, gla_weights, casts, n_lat,
            ctx_queries):
    (xl, xc), x_specs = _residual_inputs(xs, n_lat)
    b, _, d = xl.shape
    tm = TOKEN_TILE
    nt = n_lat + 1
    s = nt * tm
    row = lambda bi, i: (bi, i, 0)
    tab = pl.BlockSpec((3, tm, LANES), lambda bi, i: (0, i, 0))
    vx_rows = SWA_KV_HEADS * SWA_VT_ROWS
    gla_w = 2 * GLA_HEADS * GLA_DK
    rows_out = lambda n, dt: (pl.BlockSpec((1, tm, n), row), jax.ShapeDtypeStruct((b, s, n), dt))
    if ctx_queries:
        q_out = rows_out
    else:
        q_out = lambda n, dt: (
            pl.BlockSpec((1, tm, n), lambda bi, i: (bi, jnp.minimum(i, n_lat - 1), 0)),
            jax.ShapeDtypeStruct((b, n_lat * tm, n), dt))
    outs = [q_out(gla_w, BF16), rows_out(gla_w, BF16),
            (pl.BlockSpec((1, 1, 8, gla_w), lambda bi, i: (bi, i, 0, 0)),
             jax.ShapeDtypeStruct((b, nt, 8, gla_w), F32)),
            rows_out(1024, BF16), q_out(1024, BF16), q_out(1024, BF16),
            rows_out(SWA_KV_HEADS * 2 * LANES, BF16),
            (pl.BlockSpec((1, vx_rows, tm), lambda bi, i: (bi, 0, i)),
             jax.ShapeDtypeStruct((b, vx_rows, s), BF16)),
            q_out(3072, BF16),
            q_out(MLA_HEADS * MLA_QK_PAD, BF16), rows_out(MLA_HEADS * MLA_NOPE, BF16),
            rows_out(LANES, BF16),
            (pl.BlockSpec((1, MLA_HEADS, MLA_VT_ROWS, tm), lambda bi, i: (bi, 0, 0, i)),
             jax.ShapeDtypeStruct((b, MLA_HEADS, MLA_VT_ROWS, s), BF16))]
    w, w_spec = _weight_spec(w)
    cast_in, cast_out = [], []
    for arr, layer in casts:
        rows, cols = arr.shape[1:]
        blk = _cast_rows(rows, b * nt)
        last = rows // blk - 1
        cast_in.append(pl.BlockSpec(
            (None, blk, cols),
            lambda bi, i, layer=layer, last=last: (layer, jnp.minimum(bi * nt + i, last), 0)))
        cast_out.append((pl.BlockSpec(
            (blk, cols), lambda bi, i, last=last: (jnp.minimum(bi * nt + i, last), 0)),
            jax.ShapeDtypeStruct((rows, cols), BF16)))
    outs = outs + cast_out
    return pl.pallas_call(
        functools.partial(_inproj_kernel, n_lat=n_lat, n_cast=len(casts), ctx_queries=ctx_queries),
        grid=(b, nt),
        in_specs=x_specs + [
            pl.BlockSpec((1, 1, 8, d), lambda bi, i: (bi, i // n_lat, 0, 0)),
            _const_spec((1, d)),
            w_spec,
            tab, tab, tab,
        ] + [_const_spec(a.shape) for a in mla_weights] + [tab]
        + [_const_spec(a.shape) for a in gla_weights] + cast_in,
        out_specs=[o[0] for o in outs],
        out_shape=[o[1] for o in outs],
        compiler_params=_cparams(2),
        name="inproj",
    )(xl, xc, modtab, gain, w, rq, rk, rm, *mla_weights, rmq, *gla_weights, *[a for a, _ in casts])


def _gla_kernel(qdf_ref, qdb_ref, kif_ref, kib_ref, v_ref, g_ref, totf_ref, totb_ref, gn_ref,
                sel_ref, o_ref,
                oi_ref, af_ref, ab_ref, st_ref, decf_ref, decb_ref, keepf_ref, keepb_ref,
                *, n_lat, with_ctx):
    c = GLA_CHUNK
    tm = TOKEN_TILE
    per = tm // c

    ri = lax.broadcasted_iota(jnp.int32, (tm, tm), 0)
    ci = lax.broadcasted_iota(jnp.int32, (tm, tm), 1)
    same = (ri // c) == (ci // c)
    keepf_ref[...] = jnp.where(same & (ci <= ri), 1.0, 0.0)
    keepb_ref[...] = jnp.where(same & (ci > ri), 1.0, 0.0)

    def tile_rows(t):
        return pl.ds(pl.multiple_of(t * tm, tm), tm)

    def in_chunk(t, qd_ref, ki_ref, tot_ref, a_ref, dec_ref, scores=True):
        rows = tile_rows(t)
        tot = tot_ref[0, t]
        hi = tot.astype(BF16)
        lo = (tot - hi.astype(F32)).astype(BF16)
        dec_ref[t] = jnp.exp(_dot_tn(jnp.concatenate([hi, lo], axis=0), sel_ref[...]))
        v = v_ref[0, rows, :]
        k_inv = ki_ref[0, rows, :]
        for j in range(per):
            a_ref[t * per + j] = _dot_tn(k_inv[j * c:(j + 1) * c], v[j * c:(j + 1) * c])
        return _dot_nt(qd_ref[0, rows, :], k_inv) if scores else None

    def bulk_body(t, carry):
        s_f = in_chunk(t, qdf_ref, kif_ref, totf_ref, af_ref, decf_ref)
        s_b = in_chunk(t, qdb_ref, kib_ref, totb_ref, ab_ref, decb_ref)
        scores = jnp.where(keepf_ref[...] > 0.5, s_f, jnp.where(keepb_ref[...] > 0.5, s_b, 0.0))
        oi_ref[tile_rows(t), :] = _dot(scores.astype(BF16), v_ref[0, tile_rows(t), :])
        return carry

    if with_ctx:
        lax.fori_loop(0, n_lat + 1, bulk_body, 0, unroll=3)
    else:
        lax.fori_loop(0, n_lat, bulk_body, 0, unroll=4)
        in_chunk(n_lat, qdf_ref, kif_ref, totf_ref, af_ref, decf_ref, scores=False)
        in_chunk(n_lat, qdb_ref, kib_ref, totb_ref, ab_ref, decb_ref, scores=False)

    def scan_tile(t, order, state, a_ref, dec_ref, st_rows):
        for j in order:
            st_ref[t * per + j, st_rows, :] = state.astype(BF16)
            dec = dec_ref[t, :, j * LANES:(j + 1) * LANES]
            state = jnp.concatenate([dec] * (GLA_DV // LANES), axis=1) * (state + a_ref[t * per + j])
        return state

    fwd_order = range(per)
    bwd_order = range(per - 1, -1, -1)
    zero = jnp.zeros((GLA_DK, GLA_DV), F32)
    f_rows, b_rows = slice(0, GLA_DK), slice(GLA_DK, 2 * GLA_DK)
    s_f = scan_tile(n_lat, fwd_order, zero, af_ref, decf_ref, f_rows)
    s_b = scan_tile(n_lat, bwd_order, zero, ab_ref, decb_ref, b_rows)
    lax.fori_loop(0, n_lat, lambda i, s: scan_tile(
        i, fwd_order, s, af_ref, decf_ref, f_rows), s_f)
    lax.fori_loop(0, n_lat, lambda i, s: scan_tile(
        n_lat - 1 - i, bwd_order, s, ab_ref, decb_ref, b_rows), s_b)

    def post_body(t, carry):
        for j in range(per):
            rows = pl.ds(pl.multiple_of(t * tm + j * c, c), c)
            q_both = jnp.concatenate([qdf_ref[0, rows, :], qdb_ref[0, rows, :]], axis=1)
            o = oi_ref[rows, :] + _dot(q_both, st_ref[t * per + j])
            y = _rms(o) * gn_ref[...]
            o_ref[0, rows, :] = (y * g_ref[0, rows, :].astype(F32)).astype(BF16)
        return carry

    if with_ctx:
        lax.fori_loop(0, n_lat + 1, post_body, 0, unroll=3)
    else:
        lax.fori_loop(0, n_lat, post_body, 0, unroll=4)


def _gla_sums():
    tm, c = TOKEN_TILE, GLA_CHUNK
    r = np.arange(tm)[:, None]
    col = np.arange(tm)[None, :]
    same = (r // c) == (col // c)
    return jnp.asarray(np.stack([same & (col <= r), same & (col >= r)]).astype(np.float32), BF16)


def _gla_selector():
    per = TOKEN_TILE // GLA_CHUNK
    sel = np.zeros((16, per * LANES), np.float32)
    for j in range(per):
        sel[j, j * LANES:(j + 1) * LANES] = 1.0
        sel[8 + j, j * LANES:(j + 1) * LANES] = 1.0
    return jnp.asarray(sel, BF16)


def _gla(qd, ki, tot, va, ga, gn, t_lat):
    b, s, _ = va.shape
    s_q = qd.shape[1]
    tm = TOKEN_TILE
    n_lat = t_lat // tm
    n_tiles = s // tm
    n_chunks = s // GLA_CHUNK
    h = GLA_HEADS
    sel = _gla_selector()
    fwd = lambda bi, hi: (bi, 0, hi)
    bwd = lambda bi, hi: (bi, 0, h + hi)
    return pl.pallas_call(
        functools.partial(_gla_kernel, n_lat=n_lat, with_ctx=s_q == s),
        grid=(b, h),
        in_specs=[
            pl.BlockSpec((1, s_q, GLA_DK), fwd), pl.BlockSpec((1, s_q, GLA_DK), bwd),
            pl.BlockSpec((1, s, GLA_DK), fwd), pl.BlockSpec((1, s, GLA_DK), bwd),
            pl.BlockSpec((1, s, GLA_DV), fwd), pl.BlockSpec((1, s_q, GLA_DV), fwd),
            pl.BlockSpec((1, n_tiles, 8, GLA_DK), lambda bi, hi: (bi, 0, 0, hi)),
            pl.BlockSpec((1, n_tiles, 8, GLA_DK), lambda bi, hi: (bi, 0, 0, h + hi)),
            pl.BlockSpec((1, GLA_DV), lambda bi, hi: (0, 0)),
            _const_spec(sel.shape),
        ],
        out_specs=pl.BlockSpec((1, s_q, GLA_DV), fwd),
        out_shape=jax.ShapeDtypeStruct((b, s_q, h * GLA_DV), BF16),
        scratch_shapes=[
            pltpu.VMEM((s, GLA_DV), F32),
            pltpu.VMEM((n_chunks, GLA_DK, GLA_DV), F32), pltpu.VMEM((n_chunks, GLA_DK, GLA_DV), F32),
            pltpu.VMEM((n_chunks, 2 * GLA_DK, GLA_DV), BF16),
            pltpu.VMEM((n_tiles, GLA_DK, tm // GLA_CHUNK * LANES), F32),
            pltpu.VMEM((n_tiles, GLA_DK, tm // GLA_CHUNK * LANES), F32),
            pltpu.VMEM((tm, tm), F32), pltpu.VMEM((tm, tm), F32),
        ],
        compiler_params=_cparams(2),
        name="gla",
    )(qd, qd, ki, ki, va, ga, tot, tot, gn, sel)


def _swa_kernel(sink_ref, q_ref, kx_ref, vx_ref, bias_ref, o_ref, s0_ref, s1_ref, m0_ref, m1_ref,
                *, n_lat, t_lat, with_ctx):
    tq = TOKEN_TILE
    span = tq + 2 * WINDOW
    hd = SWA_HEAD_DIM
    g = pl.program_id(1)
    ka, kb = slice(0, LANES), slice(LANES, 2 * LANES)

    def q_rows(i):
        return pl.ds(pl.multiple_of(i * tq, tq), tq)

    def sink_of(head):
        return sink_ref[g * SWA_GROUP + head] * LOG2_E

    def keys(i):
        ws = pl.multiple_of(jnp.clip(i * tq - WINDOW, 0, t_lat - span), WINDOW)
        win = pl.ds(ws, span)
        kc = [jnp.concatenate([kx_ref[0, win, c], kx_ref[0, t_lat:, c]], axis=0) for c in (ka, kb)]
        vc = jnp.concatenate([vx_ref[0, :, win], vx_ref[0, :, t_lat:]], axis=1)
        return kc, vc, (i * tq - ws) // WINDOW

    def scores(i, kc, bias_idx, head, s_ref, m_ref):
        qp = q_ref[0, q_rows(i), (head // 2) * LANES:(head // 2 + 1) * LANES]
        s_t = _dot_nt(kc[head % 2], qp)
        top = s_t[0:span] + bias_ref[bias_idx]
        bot = s_t[span:]
        s_ref[0:span, :] = top
        s_ref[span:, :] = bot
        m = jnp.maximum(jnp.max(top, axis=0, keepdims=True), jnp.max(bot, axis=0, keepdims=True))
        m_ref[...] = jnp.broadcast_to(jnp.maximum(m, sink_of(head)), m_ref.shape)

    def normalise(acc, m, head):
        return acc[0:hd] / (acc[hd:hd + 1] + jnp.exp2(sink_of(head) - m))

    def values(vc, head, s_ref, m_ref):
        m = m_ref[0:1, :]
        p_t = jnp.exp2(s_ref[...] - m).astype(BF16)
        return normalise(_dot(vc, p_t), m, head)

    def store(i, pair, o_a, o_b):
        o_t = jnp.concatenate([o_a, o_b], axis=0)
        o_ref[0, q_rows(i), pair * LANES:(pair + 1) * LANES] = o_t.T.astype(BF16)

    def scores_all(i, s_ref, m_ref):
        kc, _, bias_idx = keys(i)
        for head in range(SWA_GROUP):
            scores(i, kc, bias_idx, head, s_ref.at[head], m_ref.at[head])

    def values_all(i, s_ref, m_ref):
        _, vc, _ = keys(i)
        for pair in range(SWA_GROUP // 2):
            a, b = 2 * pair, 2 * pair + 1
            store(i, pair, values(vc, a, s_ref.at[a], m_ref.at[a]),
                  values(vc, b, s_ref.at[b], m_ref.at[b]))

    scores_all(0, s0_ref, m0_ref)

    def body(j, carry):
        scores_all(2 * j + 1, s1_ref, m1_ref)
        values_all(2 * j, s0_ref, m0_ref)
        scores_all(2 * j + 2, s0_ref, m0_ref)
        values_all(2 * j + 1, s1_ref, m1_ref)
        return carry

    lax.fori_loop(0, n_lat // 2 - 1, body, 0)
    scores_all(n_lat - 1, s1_ref, m1_ref)
    values_all(n_lat - 2, s0_ref, m0_ref)
    values_all(n_lat - 1, s1_ref, m1_ref)

    if with_ctx:
        rows = slice(t_lat, t_lat + tq)
        for pair in range(SWA_GROUP // 2):
            qp = q_ref[0, rows, pair * LANES:(pair + 1) * LANES]
            outs = []
            for head, kcols in ((2 * pair, ka), (2 * pair + 1, kb)):
                s_t = _dot_nt(kx_ref[0, t_lat:, kcols], qp)
                m = jnp.maximum(jnp.max(s_t, axis=0, keepdims=True), sink_of(head))
                p_t = jnp.exp2(s_t - m).astype(BF16)
                outs.append(normalise(_dot(vx_ref[0, :, t_lat:], p_t), m, head))
            o_t = jnp.concatenate(outs, axis=0)
            o_ref[0, rows, pair * LANES:(pair + 1) * LANES] = o_t.T.astype(BF16)


def _swa_bias():
    tq, span = TOKEN_TILE, TOKEN_TILE + 2 * WINDOW
    r = np.arange(span)[:, None]
    c = np.arange(tq)[None, :]
    tabs = [np.where(np.abs(r - off - c) <= WINDOW, 0.0, -1e30) for off in (0, WINDOW, 2 * WINDOW)]
    return jnp.asarray(np.stack(tabs).astype(np.float32))


def _swa(sinks, qs, kx, vx, t_lat, n_tiles):
    b, s, _ = kx.shape
    tq = TOKEN_TILE
    gw = SWA_GROUP * SWA_HEAD_DIM
    n_keys = tq + 2 * WINDOW + (s - t_lat)
    bias = _swa_bias()
    n_lat = t_lat // tq
    return pl.pallas_call(
        functools.partial(_swa_kernel, n_lat=n_lat, t_lat=t_lat, with_ctx=n_tiles > n_lat),
        grid=(b, SWA_KV_HEADS),
        in_specs=[
            pl.BlockSpec(memory_space=pltpu.SMEM),
            pl.BlockSpec((1, qs.shape[1], gw), lambda bi, gi: (bi, 0, gi)),
            pl.BlockSpec((1, s, 2 * LANES), lambda bi, gi: (bi, 0, gi)),
            pl.BlockSpec((1, SWA_VT_ROWS, s), lambda bi, gi: (bi, gi, 0)),
            _const_spec(bias.shape),
        ],
        out_specs=pl.BlockSpec((1, n_tiles * tq, gw), lambda bi, gi: (bi, 0, gi)),
        out_shape=jax.ShapeDtypeStruct((b, n_tiles * tq, SWA_HEADS * SWA_HEAD_DIM), BF16),
        scratch_shapes=[pltpu.VMEM((SWA_GROUP, n_keys, tq), F32), pltpu.VMEM((SWA_GROUP, n_keys, tq), F32),
                        pltpu.VMEM((SWA_GROUP, 8, tq), F32), pltpu.VMEM((SWA_GROUP, 8, tq), F32)],
        compiler_params=_cparams(2),
        name="swa",
    )(sinks, qs, kx, vx, bias)


def _mla_attn_kernel(q_ref, kn_ref, kr_ref, vt_ref, o_ref, k_ref, s0_ref, s1_ref, m0_ref, m1_ref,
                     *, n_lat, t_lat, with_ctx):
    tq = TOKEN_TILE
    k_ref[:, 0:MLA_NOPE] = kn_ref[0]
    k_ref[:, MLA_NOPE:] = kr_ref[0]

    def q_rows(i):
        return pl.ds(pl.multiple_of(i * tq, tq), tq)

    def finish(acc, rows):
        o_t = acc[0:MLA_V] / acc[MLA_V:MLA_V + 1]
        o_ref[0, rows, :] = o_t.T.astype(BF16)

    def scores(i, s_ref, m_ref):
        q = q_ref[0, q_rows(i), :]
        half = k_ref.shape[0] // 2
        lo = _dot_nt(k_ref[0:half, :], q)
        hi = _dot_nt(k_ref[half:, :], q)
        s_ref[0:half, :] = lo
        s_ref[half:, :] = hi
        m = jnp.maximum(jnp.max(lo, axis=0, keepdims=True), jnp.max(hi, axis=0, keepdims=True))
        m_ref[...] = jnp.broadcast_to(m, m_ref.shape)

    def values(i, s_ref, m_ref):
        p_t = jnp.exp2(s_ref[...] - m_ref[0:1, :]).astype(BF16)
        finish(_dot(vt_ref[0, 0], p_t), q_rows(i))

    scores(0, s0_ref, m0_ref)

    def body(j, carry):
        scores(2 * j + 1, s1_ref, m1_ref)
        values(2 * j, s0_ref, m0_ref)
        scores(2 * j + 2, s0_ref, m0_ref)
        values(2 * j + 1, s1_ref, m1_ref)
        return carry

    lax.fori_loop(0, n_lat // 2 - 1, body, 0)
    scores(n_lat - 1, s1_ref, m1_ref)
    values(n_lat - 2, s0_ref, m0_ref)
    values(n_lat - 1, s1_ref, m1_ref)

    if with_ctx:
        s_t = _dot_nt(k_ref[t_lat:, :], q_ref[0, t_lat:, :])
        p_t = jnp.exp2(s_t - jnp.max(s_t, axis=0, keepdims=True)).astype(BF16)
        finish(_dot(vt_ref[0, 0, :, t_lat:], p_t), slice(t_lat, t_lat + tq))


def _mla_attn(q, kn, kr, vt, t_lat, n_tiles):
    b, s, _ = kn.shape
    tq = TOKEN_TILE
    n_lat = t_lat // tq
    return pl.pallas_call(
        functools.partial(_mla_attn_kernel, n_lat=n_lat, t_lat=t_lat, with_ctx=n_tiles > n_lat),
        grid=(b, MLA_HEADS),
        in_specs=[
            pl.BlockSpec((1, q.shape[1], MLA_QK_PAD), lambda bi, hi: (bi, 0, hi)),
            pl.BlockSpec((1, s, MLA_NOPE), lambda bi, hi: (bi, 0, hi)),
            pl.BlockSpec((1, s, LANES), lambda bi, hi: (bi, 0, 0)),
            pl.BlockSpec((1, 1, MLA_VT_ROWS, s), lambda bi, hi: (bi, hi, 0, 0)),
        ],
        out_specs=pl.BlockSpec((1, n_tiles * tq, MLA_V), lambda bi, hi: (bi, 0, hi)),
        out_shape=jax.ShapeDtypeStruct((b, n_tiles * tq, MLA_HEADS * MLA_V), BF16),
        scratch_shapes=[pltpu.VMEM((s, MLA_QK_PAD), BF16),
                        pltpu.VMEM((s, tq), F32), pltpu.VMEM((s, tq), F32),
                        pltpu.VMEM((8, tq), F32), pltpu.VMEM((8, tq), F32)],
        compiler_params=_cparams(2),
        name="mla_attn",
    )(q, kn, kr, vt)


def _residual_inputs(xs, n_lat):
    tm = TOKEN_TILE
    if isinstance(xs, tuple):
        lat, ctx, ctx_block = xs[0], xs[1], 0
    else:
        lat, ctx, ctx_block = xs, xs, n_lat
    d = lat.shape[-1]
    specs = [pl.BlockSpec((1, tm, d), lambda bi, i: (bi, jnp.minimum(i, n_lat - 1), 0)),
             pl.BlockSpec((1, tm, d), lambda bi, i: (bi, ctx_block, 0))]
    return (lat, ctx), specs


def _pick_tile(xl_ref, xc_ref, n_lat):
    return jnp.where(pl.program_id(1) < n_lat, xl_ref[0], xc_ref[0])


def _mix_ffn_kernel(xl_ref, xc_ref, ya_ref, yb_ref, yc_ref, mg_ref, mod_ref,
                    wa_ref, wb_ref, wc_ref, wo_ref, gain_ref, wi_ref, wd_ref, fin_ref, o_ref,
                    *, n_lat, final):
    d = D_MODEL
    u = mg_ref[0, :, 0:d].astype(F32) * _dot(ya_ref[0], wa_ref[...])
    u += mg_ref[0, :, d:2 * d].astype(F32) * _dot(yb_ref[0], wb_ref[...])
    u += mg_ref[0, :, 2 * d:3 * d].astype(F32) * _dot(yc_ref[0], wc_ref[...])
    x = _pick_tile(xl_ref, xc_ref, n_lat) + mod_ref[0, 0, 2:3, :] * _dot(u.astype(BF16), wo_ref[...])

    y = _rms(x) * gain_ref[...]
    h = (y * (1.0 + mod_ref[0, 0, 4:5, :]) + mod_ref[0, 0, 3:4, :]).astype(BF16)
    gate = _dot(h, wi_ref[:, 0:D_FF])
    up = _dot(h, wi_ref[:, D_FF:2 * D_FF])
    a = (gate * jax.nn.sigmoid(gate) * up).astype(BF16)
    x = x + mod_ref[0, 0, 5:6, :] * _dot(a, wd_ref[...])
    if final:
        x = _rms(x) * fin_ref[...]
    o_ref[0] = x


def _mix_ffn(xs, ya, yb, yc, mg, modtab, wa, wb, wc, wo, gain, wi, wd, fin, n_lat, n_tiles, final):
    (xl, xc), x_specs = _residual_inputs(xs, n_lat)
    b, _, d = xl.shape
    tm = TOKEN_TILE
    row = lambda bi, i: (bi, i, 0)
    weights, w_specs = zip(*[_weight_spec(w) for w in (wa, wb, wc, wo, gain, wi, wd, fin)])
    return pl.pallas_call(
        functools.partial(_mix_ffn_kernel, n_lat=n_lat, final=final),
        grid=(b, n_tiles),
        in_specs=x_specs + [
            pl.BlockSpec((1, tm, d), row), pl.BlockSpec((1, tm, d), row),
            pl.BlockSpec((1, tm, d), row), pl.BlockSpec((1, tm, 3 * d), row),
            pl.BlockSpec((1, 1, 8, d), lambda bi, i: (bi, i // n_lat, 0, 0)),
        ] + list(w_specs),
        out_specs=pl.BlockSpec((1, tm, d), row),
        out_shape=jax.ShapeDtypeStruct((b, n_tiles * tm, d), F32),
        compiler_params=_cparams(2),
        name="mix_ffn",
    )(xl, xc, ya, yb, yc, mg, modtab, *weights)


def _rope_tables(t_lat, s_tot):
    half = SWA_HEAD_DIM // 4
    inv = np.power(ROPE_BASE, -np.arange(half, dtype=np.float32) / half).astype(np.float32)
    pos = np.arange(t_lat)
    lane = np.arange(SWA_HEAD_DIM)
    p = np.where(lane[None, :] < 2 * half, (pos // GRID_W)[:, None], (pos % GRID_W)[:, None])
    ang = p.astype(np.float32) * inv[lane % half][None, :]
    cos, sin = np.cos(ang), np.sin(ang)
    upper = (lane % (2 * half)) >= half
    tabs = np.stack([cos, np.where(upper, sin, 0.0), np.where(upper, 0.0, -sin)]).astype(np.float32)
    ident = np.zeros((3, s_tot - t_lat, SWA_HEAD_DIM), np.float32)
    ident[0] = 1.0
    return np.concatenate([tabs, ident], axis=1)


def kernel(x, c, ctx, c_ctx, w_mod, b_mod, norm_mix, w_in, w_gk_fwd, b_gk_fwd, w_gk_bwd, b_gk_bwd,
           gla_norm, sinks, q_norm, w_q_up, kv_norm, w_kv_up, w_pa, w_pb, w_pc, w_o,
           norm_ffn, w_ffn_in, w_ffn_out, final_norm):
    b, t_lat, d = x.shape
    l_ctx = ctx.shape[1]
    s_tot = t_lat + l_ctx
    depth = w_mod.shape[0]
    tm = TOKEN_TILE
    assert t_lat % tm == 0 and l_ctx == tm and t_lat % GRID_W == 0 and d == D_MODEL
    n_lat = t_lat // tm
    n_all = s_tot // tm

    cc = jnp.concatenate([c, c_ctx[None, :], jnp.zeros((16 - b - 1, d), F32)], axis=0)
    mod = _modulation(cc, w_mod, b_mod)
    mod_lat = mod[:, :b].reshape(depth, b, 1, 6, d)
    mod_ctx = jnp.broadcast_to(mod[:, b].reshape(depth, 1, 1, 6, d), (depth, b, 1, 6, d))
    modtab = jnp.pad(jnp.concatenate([mod_lat, mod_ctx], axis=2),
                     ((0, 0), (0, 0), (0, 0), (0, 2), (0, 0)))

    t64 = _rope_tables(t_lat, s_tot)
    ident = np.zeros_like(t64)
    ident[0] = 1.0
    rk = np.concatenate([t64, t64], axis=2)
    rq = rk * np.float32(SWA_HEAD_DIM ** -0.5 * LOG2_E)
    rm = np.concatenate([t64, ident], axis=2)
    rmq = rm * np.float32((MLA_NOPE + MLA_ROPE) ** -0.5 * LOG2_E)
    rq, rk, rm, rmq = (jnp.asarray(a) for a in (rq, rk, rm, rmq))

    w_in_t = jnp.swapaxes(w_in, 1, 2).astype(BF16)

    gla_sums = _gla_sums()

    xs = (x, ctx)
    for l in range(depth):
        last = l == depth - 1
        n_out = n_lat if last else n_all
        hk = GLA_HEADS * GLA_DK
        wg = jnp.zeros((2 * GLA_GATE_RANK, 2 * hk), F32)
        wg = wg.at[0:GLA_GATE_RANK, 0:hk].set(w_gk_fwd[l])
        wg = wg.at[GLA_GATE_RANK:, hk:].set(w_gk_bwd[l])
        bg = jnp.concatenate([b_gk_fwd[l], b_gk_bwd[l]]).reshape(1, 2 * hk)
        gla_weights = (wg.astype(BF16), bg, gla_sums)
        wq = jnp.pad(w_q_up[l].reshape(MLA_Q_RANK, MLA_HEADS, MLA_NOPE + MLA_ROPE),
                     ((0, 0), (0, 0), (0, MLA_QK_PAD - MLA_NOPE - MLA_ROPE))
                     ).reshape(MLA_Q_RANK, MLA_HEADS * MLA_QK_PAD).astype(BF16)

        wkv = w_kv_up[l].reshape(MLA_KV_RANK, MLA_HEADS, MLA_NOPE + MLA_V)
        wk = wkv[:, :, :MLA_NOPE].reshape(MLA_KV_RANK, MLA_HEADS * MLA_NOPE).astype(BF16)
        wvt = wkv[:, :, MLA_NOPE:].reshape(MLA_KV_RANK, MLA_HEADS * MLA_V).T.astype(BF16)
        mla_weights = (wq, wk, wvt, q_norm[l].reshape(1, -1), kv_norm[l].reshape(1, -1))

        later_weights = [(w, l) for w in (w_pa, w_pb, w_pc, w_o, w_ffn_in, w_ffn_out)]
        (qd, ki, tot, va, ga, qs, kx, vx, mg, qm, kn, kr, vm,
         wa_b, wb_b, wc_b, wo_b, wi_b, wd_b) = _inproj(
            xs, modtab[l], norm_mix[l].reshape(1, d), (w_in_t, l), rq, rk, rm,
            mla_weights, rmq, gla_weights, later_weights, n_lat, ctx_queries=not last)
        ya = _gla(qd, ki, tot, va, ga, gla_norm[l].reshape(1, GLA_DV), t_lat)
        yb = _swa(sinks[l], qs, kx, vx, t_lat, n_out)
        yc = _mla_attn(qm, kn, kr, vm, t_lat, n_out)
        xs = _mix_ffn(xs, ya, yb, yc, mg, modtab[l], wa_b, wb_b, wc_b, wo_b,
                      norm_ffn[l].reshape(1, d), wi_b, wd_b,
                      final_norm.reshape(1, d), n_lat, n_out, last)
    return xs
```

```python
import functools

import jax
import jax.numpy as jnp
import numpy as np
from jax import lax
from jax.experimental import pallas as pl
from jax.experimental.pallas import tpu as pltpu

F32 = jnp.float32
BF16 = jnp.bfloat16

D_MODEL = 1024
GRID_W = 64
EPS = 1e-6
ROPE_BASE = 10000.0

GLA_HEADS = 4
GLA_DK = 128
GLA_DV = 256
GLA_GATE_RANK = 16
GLA_GATE_NORM = 16.0
GLA_CHUNK = 64

SWA_HEADS = 16
SWA_KV_HEADS = 2
SWA_GROUP = SWA_HEADS // SWA_KV_HEADS
SWA_HEAD_DIM = 64
WINDOW = 128
SWA_VT_ROWS = SWA_HEAD_DIM + 16

MLA_HEADS = 8
MLA_Q_RANK = 384
MLA_KV_RANK = 256
MLA_NOPE = 128
MLA_ROPE = 64
MLA_V = 128
MLA_QK_PAD = 256
MLA_VT_ROWS = MLA_V + 16
LOG2_E = 1.4426950408889634

D_FF = -(-(8 * D_MODEL) // (3 * 256)) * 256
FFN_SLAB = 256

IN_SPLITS = (
    GLA_HEADS * GLA_DK, GLA_HEADS * GLA_DK, GLA_HEADS * GLA_DV, GLA_HEADS * GLA_DV,
    GLA_GATE_RANK, GLA_GATE_RANK,
    SWA_HEADS * SWA_HEAD_DIM, SWA_KV_HEADS * SWA_HEAD_DIM, SWA_KV_HEADS * SWA_HEAD_DIM,
    MLA_Q_RANK, MLA_KV_RANK, MLA_ROPE,
    3 * D_MODEL,
)

LANES = 128
TOKEN_TILE = 256
(ROW_QA, ROW_KA, ROW_VA, ROW_GA, ROW_GKF, ROW_GKB, ROW_QS, ROW_KS, ROW_VS,
 ROW_CQ, ROW_CKV, ROW_KR, ROW_MG, ROW_END) = (int(v) for v in np.cumsum((0,) + IN_SPLITS))
VMEM_LIMIT = 56 * 1024 * 1024


def _cparams(n_axes):
    return pltpu.CompilerParams(
        dimension_semantics=("arbitrary",) * n_axes, vmem_limit_bytes=VMEM_LIMIT)


def _const_spec(shape):
    nd = len(shape)
    return pl.BlockSpec(shape, lambda *_: (0,) * nd, pipeline_mode=pl.Buffered(1))


def _weight_spec(w):
    if isinstance(w, tuple):
        arr, layer = w
        nd = arr.ndim
        spec = pl.BlockSpec((None,) + arr.shape[1:], lambda *_: (layer,) + (0,) * (nd - 1),
                            pipeline_mode=pl.Buffered(1))
        return arr, spec
    return w, _const_spec(w.shape)


def _rope(x, tab_ref):
    return (x * tab_ref[0] + pltpu.roll(x, 16, 1) * tab_ref[1]
            + pltpu.roll(x, LANES - 16, 1) * tab_ref[2])


def _rms(x):
    return x * lax.rsqrt(jnp.mean(x * x, axis=-1, keepdims=True) + EPS)


def _dot(a, b):
    return jnp.dot(a, b, preferred_element_type=F32)


def _dot_nt(a, b):
    return lax.dot_general(a, b, (((1,), (1,)), ((), ())), preferred_element_type=F32)


def _dot_tn(a, b):
    return lax.dot_general(a, b, (((0,), (0,)), ((), ())), preferred_element_type=F32)


def _mod_kernel(c_ref, w_ref, b_ref, o_ref):
    c = c_ref[...]
    a = (c * jax.nn.sigmoid(c)).astype(BF16)
    o_ref[0] = _dot(a, w_ref[0].astype(BF16)) + b_ref[0]


def _modulation(cc, w_mod, b_mod):
    depth, d, n = w_mod.shape
    tn = 1536
    return pl.pallas_call(
        _mod_kernel,
        grid=(depth, n // tn),
        in_specs=[
            pl.BlockSpec(cc.shape, lambda l, j: (0, 0)),
            pl.BlockSpec((1, d, tn), lambda l, j: (l, 0, j)),
            pl.BlockSpec((1, 1, tn), lambda l, j: (l, 0, j)),
        ],
        out_specs=pl.BlockSpec((1, cc.shape[0], tn), lambda l, j: (l, 0, j)),
        out_shape=jax.ShapeDtypeStruct((depth, cc.shape[0], n), F32),
        compiler_params=_cparams(2),
        name="modulation",
    )(cc, w_mod, b_mod.reshape(depth, 1, n))


def _mla_keys_values(ckv, tail, wk_ref, wvt_ref, kvn_ref, k_ref, kr_ref, vt_ref):
    lo = lax.broadcasted_iota(jnp.int32, tail.shape, 1) < MLA_ROPE
    kr_ref[0] = jnp.where(lo, tail, 0.0).astype(BF16)
    ckvn = (_rms(ckv) * kvn_ref[...]).astype(BF16)
    k_ref[0] = _dot(ckvn, wk_ref[...]).astype(BF16)
    vt = _dot_nt(wvt_ref[...], ckvn)
    ones = jnp.ones((MLA_VT_ROWS - MLA_V, vt.shape[1]), BF16)
    for h in range(MLA_HEADS):
        vt_ref[0, h, 0:MLA_V, :] = vt[h * MLA_V:(h + 1) * MLA_V, :].astype(BF16)
        vt_ref[0, h, MLA_V:MLA_VT_ROWS, :] = ones


def _mla_queries(cq, wq_ref, qn_ref, rq_ref, q_ref):
    scale = (MLA_NOPE + MLA_ROPE) ** -0.5 * LOG2_E
    qf = _dot((_rms(cq) * qn_ref[...]).astype(BF16), wq_ref[...])
    for h in range(MLA_HEADS):
        c0 = h * MLA_QK_PAD
        q_ref[0, :, c0:c0 + LANES] = (qf[:, c0:c0 + LANES] * scale).astype(BF16)
        q_ref[0, :, c0 + LANES:c0 + 2 * LANES] = _rope(
            qf[:, c0 + LANES:c0 + 2 * LANES], rq_ref).astype(BF16)


def _gla_decays(q, k, gates, wg_ref, bg_ref, sum_ref, qd_ref, ki_ref, tot_ref):
    hk = GLA_HEADS * GLA_DK
    c = GLA_CHUNK
    z = _dot(gates.astype(BF16), wg_ref[...]) + bg_ref[...]
    la = (jnp.minimum(z, 0.0) - jnp.log1p(jnp.exp(-jnp.abs(z)))) * (1.0 / GLA_GATE_NORM)
    hi = la.astype(BF16)
    lo = (la - hi.astype(F32)).astype(BF16)
    totals = []
    for d in range(2):
        sl = slice(d * hk, (d + 1) * hk)
        cr = _dot(sum_ref[d], jnp.concatenate([hi[:, sl], lo[:, sl]], axis=1))
        cum = cr[:, 0:hk] + cr[:, hk:]
        if q is not None:
            qd_ref[0, :, sl] = (q * (GLA_DK ** -0.5) * jnp.exp(cum)).astype(BF16)
        ki_ref[0, :, sl] = (k * jnp.exp(-cum)).astype(BF16)
        last = (c - 1) if d == 0 else 0
        ends = [cum[j * c + last:j * c + last + 1] for j in range(TOKEN_TILE // c)]
        totals.append(jnp.concatenate(ends + [jnp.zeros((8 - len(ends), hk), F32)], axis=0))
    tot_ref[0, 0] = jnp.concatenate(totals, axis=1)


INPROJ_INPUTS = 17
INPROJ_OUTPUTS = 13


def _inproj_kernel(*refs, n_lat, n_cast, ctx_queries):
    n_in = INPROJ_INPUTS + n_cast
    body = functools.partial(_inproj_body, *refs[:INPROJ_INPUTS],
                             *refs[n_in:n_in + INPROJ_OUTPUTS], n_lat=n_lat)
    if ctx_queries:
        body(queries=True)
    else:
        pl.when(pl.program_id(1) < n_lat)(functools.partial(body, queries=True))
        pl.when(pl.program_id(1) >= n_lat)(functools.partial(body, queries=False))
    for src, dst in zip(refs[INPROJ_INPUTS:n_in], refs[n_in + INPROJ_OUTPUTS:]):
        dst[...] = src[...].astype(BF16)


def _inproj_body(xl_ref, xc_ref, mod_ref, gain_ref, w_ref, rq_ref, rk_ref, rm_ref,
                 wq_ref, wk_ref, wmvt_ref, qn_ref, kvn_ref, rmq_ref, wg_ref, bg_ref, sum_ref,
                 qd_ref, ki_ref, tot_ref, va_ref, ga_ref, qs_ref, kx_ref, vx_ref, mg_ref,
                 mq_ref, mk_ref, mkr_ref, mvt_ref, *, n_lat, queries):
    x = _pick_tile(xl_ref, xc_ref, n_lat)
    y = _rms(x) * gain_ref[...]
    h = (y * (1.0 + mod_ref[0, 0, 1:2, :]) + mod_ref[0, 0, 0:1, :]).astype(BF16)
    hk = GLA_HEADS * GLA_DK

    def proj(c0, c1):
        return _dot_nt(h, w_ref[c0:c1, :])

    r = proj(ROW_CQ if queries else ROW_CKV, ROW_KR + LANES)
    kv0 = r.shape[1] - MLA_KV_RANK - LANES
    if queries:
        _mla_queries(r[:, 0:MLA_Q_RANK], wq_ref, qn_ref, rmq_ref, mq_ref)
    _mla_keys_values(r[:, kv0:kv0 + MLA_KV_RANK], _rope(r[:, kv0 + MLA_KV_RANK:], rm_ref),
                     wk_ref, wmvt_ref, kvn_ref, mk_ref, mkr_ref, mvt_ref)

    if queries:
        r = proj(ROW_GA, ROW_QS)
        g = r[:, 0:ROW_GKF - ROW_GA]
        ga_ref[0] = (g * jax.nn.sigmoid(g)).astype(BF16)
        qk = proj(ROW_QA, ROW_VA)
        _gla_decays(qk[:, 0:hk], qk[:, hk:], r[:, ROW_GKF - ROW_GA:], wg_ref, bg_ref, sum_ref,
                    qd_ref, ki_ref, tot_ref)
    else:
        _gla_decays(None, proj(ROW_KA, ROW_VA), proj(ROW_GKF, ROW_QS), wg_ref, bg_ref, sum_ref,
                    qd_ref, ki_ref, tot_ref)
    va_ref[0] = proj(ROW_VA, ROW_GA).astype(BF16)

    if queries:
        r = proj(ROW_QS, ROW_KS)
        for j in range(1024 // LANES):
            sl = slice(j * LANES, (j + 1) * LANES)
            qs_ref[0, :, sl] = _rope(r[:, sl], rq_ref).astype(BF16)

    kk = _rope(proj(ROW_KS, ROW_VS), rk_ref)
    kx = pltpu.roll(kk, 64, 1)
    lo = lax.broadcasted_iota(jnp.int32, kk.shape, 1) < 64
    zero = jnp.zeros_like(kk)
    blocks = (jnp.where(lo, kk, zero), jnp.where(lo, zero, kx),
              jnp.where(lo, kx, zero), jnp.where(lo, zero, kk))
    for j, blk in enumerate(blocks):
        kx_ref[0, :, j * LANES:(j + 1) * LANES] = blk.astype(BF16)

    vt = _dot_nt(w_ref[ROW_VS:ROW_CQ, :], h).astype(BF16)
    hd = SWA_HEAD_DIM
    ones = jnp.ones((SWA_VT_ROWS - hd, vt.shape[1]), BF16)
    for g in range(SWA_KV_HEADS):
        vx_ref[0, g * SWA_VT_ROWS:g * SWA_VT_ROWS + hd, :] = vt[g * hd:(g + 1) * hd, :]
        vx_ref[0, g * SWA_VT_ROWS + hd:(g + 1) * SWA_VT_ROWS, :] = ones

    if queries:
        for j in range(3):
            r = proj(ROW_MG + j * 1024, ROW_MG + (j + 1) * 1024)
            mg_ref[0, :, j * 1024:(j + 1) * 1024] = jax.nn.sigmoid(r).astype(BF16)


def _cast_rows(rows, steps):
    return next(r for r in range(16, rows + 1, 16) if rows % r == 0 and rows // r <= steps)


def _inproj(xs, modtab, gain, w, rq, rk, rm, mla_weights, rmq, gla_weights, casts, n_lat,
            ctx_queries):
    (xl, xc), x_specs = _residual_inputs(xs, n_lat)
    b, _, d = xl.shape
    tm = TOKEN_TILE
    nt = n_lat + 1
    s = nt * tm
    row = lambda bi, i: (bi, i, 0)
    tab = pl.BlockSpec((3, tm, LANES), lambda bi, i: (0, i, 0))
    vx_rows = SWA_KV_HEADS * SWA_VT_ROWS
    gla_w = 2 * GLA_HEADS * GLA_DK
    rows_out = lambda n, dt: (pl.BlockSpec((1, tm, n), row), jax.ShapeDtypeStruct((b, s, n), dt))
    if ctx_queries:
        q_out = rows_out
    else:
        q_out = lambda n, dt: (
            pl.BlockSpec((1, tm, n), lambda bi, i: (bi, jnp.minimum(i, n_lat - 1), 0)),
            jax.ShapeDtypeStruct((b, n_lat * tm, n), dt))
    outs = [q_out(gla_w, BF16), rows_out(gla_w, BF16),
            (pl.BlockSpec((1, 1, 8, gla_w), lambda bi, i: (bi, i, 0, 0)),
             jax.ShapeDtypeStruct((b, nt, 8, gla_w), F32)),
            rows_out(1024, BF16), q_out(1024, BF16), q_out(1024, BF16),
            rows_out(SWA_KV_HEADS * 2 * LANES, BF16),
            (pl.BlockSpec((1, vx_rows, tm), lambda bi, i: (bi, 0, i)),
             jax.ShapeDtypeStruct((b, vx_rows, s), BF16)),
            q_out(3072, BF16),
            q_out(MLA_HEADS * MLA_QK_PAD, BF16), rows_out(MLA_HEADS * MLA_NOPE, BF16),
            rows_out(LANES, BF16),
            (pl.BlockSpec((1, MLA_HEADS, MLA_VT_ROWS, tm), lambda bi, i: (bi, 0, 0, i)),
             jax.ShapeDtypeStruct((b, MLA_HEADS, MLA_VT_ROWS, s), BF16))]
    w, w_spec = _weight_spec(w)
    cast_in, cast_out = [], []
    for arr, layer in casts:
        rows, cols = arr.shape[1:]
        blk = _cast_rows(rows, b * nt)
        last = rows // blk - 1
        cast_in.append(pl.BlockSpec(
            (None, blk, cols),
            lambda bi, i, layer=layer, last=last: (layer, jnp.minimum(bi * nt + i, last), 0)))
        cast_out.append((pl.BlockSpec(
            (blk, cols), lambda bi, i, last=last: (jnp.minimum(bi * nt + i, last), 0)),
            jax.ShapeDtypeStruct((rows, cols), BF16)))
    outs = outs + cast_out
    return pl.pallas_call(
        functools.partial(_inproj_kernel, n_lat=n_lat, n_cast=len(casts), ctx_queries=ctx_queries),
        grid=(b, nt),
        in_specs=x_specs + [
            pl.BlockSpec((1, 1, 8, d), lambda bi, i: (bi, i // n_lat, 0, 0)),
            _const_spec((1, d)),
            w_spec,
            tab, tab, tab,
        ] + [_const_spec(a.shape) for a in mla_weights] + [tab]
        + [_const_spec(a.shape) for a in gla_weights] + cast_in,
        out_specs=[o[0] for o in outs],
        out_shape=[o[1] for o in outs],
        compiler_params=_cparams(2),
        name="inproj",
    )(xl, xc, modtab, gain, w, rq, rk, rm, *mla_weights, rmq, *gla_weights, *[a for a, _ in casts])


def _gla_kernel(qdf_ref, qdb_ref, kif_ref, kib_ref, v_ref, g_ref, totf_ref, totb_ref, gn_ref,
                sel_ref, o_ref,
                oi_ref, af_ref, ab_ref, st_ref, decf_ref, decb_ref, keepf_ref, keepb_ref,
                *, n_lat, with_ctx):
    c = GLA_CHUNK
    tm = TOKEN_TILE
    per = tm // c

    ri = lax.broadcasted_iota(jnp.int32, (tm, tm), 0)
    ci = lax.broadcasted_iota(jnp.int32, (tm, tm), 1)
    same = (ri // c) == (ci // c)
    keepf_ref[...] = jnp.where(same & (ci <= ri), 1.0, 0.0)
    keepb_ref[...] = jnp.where(same & (ci > ri), 1.0, 0.0)

    def tile_rows(t):
        return pl.ds(pl.multiple_of(t * tm, tm), tm)

    def in_chunk(t, qd_ref, ki_ref, tot_ref, a_ref, dec_ref, scores=True):
        rows = tile_rows(t)
        tot = tot_ref[0, t]
        hi = tot.astype(BF16)
        lo = (tot - hi.astype(F32)).astype(BF16)
        dec_ref[t] = jnp.exp(_dot_tn(jnp.concatenate([hi, lo], axis=0), sel_ref[...]))
        v = v_ref[0, rows, :]
        k_inv = ki_ref[0, rows, :]
        for j in range(per):
            a_ref[t * per + j] = _dot_tn(k_inv[j * c:(j + 1) * c], v[j * c:(j + 1) * c])
        return _dot_nt(qd_ref[0, rows, :], k_inv) if scores else None

    def bulk_body(t, carry):
        s_f = in_chunk(t, qdf_ref, kif_ref, totf_ref, af_ref, decf_ref)
        s_b = in_chunk(t, qdb_ref, kib_ref, totb_ref, ab_ref, decb_ref)
        scores = jnp.where(keepf_ref[...] > 0.5, s_f, jnp.where(keepb_ref[...] > 0.5, s_b, 0.0))
        oi_ref[tile_rows(t), :] = _dot(scores.astype(BF16), v_ref[0, tile_rows(t), :])
        return carry

    if with_ctx:
        lax.fori_loop(0, n_lat + 1, bulk_body, 0, unroll=3)
    else:
        lax.fori_loop(0, n_lat, bulk_body, 0, unroll=4)
        in_chunk(n_lat, qdf_ref, kif_ref, totf_ref, af_ref, decf_ref, scores=False)
        in_chunk(n_lat, qdb_ref, kib_ref, totb_ref, ab_ref, decb_ref, scores=False)

    def scan_tile(t, order, state, a_ref, dec_ref, st_rows):
        for j in order:
            st_ref[t * per + j, st_rows, :] = state.astype(BF16)
            dec = dec_ref[t, :, j * LANES:(j + 1) * LANES]
            state = jnp.concatenate([dec] * (GLA_DV // LANES), axis=1) * (state + a_ref[t * per + j])
        return state

    fwd_order = range(per)
    bwd_order = range(per - 1, -1, -1)
    zero = jnp.zeros((GLA_DK, GLA_DV), F32)
    f_rows, b_rows = slice(0, GLA_DK), slice(GLA_DK, 2 * GLA_DK)
    s_f = scan_tile(n_lat, fwd_order, zero, af_ref, decf_ref, f_rows)
    s_b = scan_tile(n_lat, bwd_order, zero, ab_ref, decb_ref, b_rows)
    lax.fori_loop(0, n_lat, lambda i, s: scan_tile(
        i, fwd_order, s, af_ref, decf_ref, f_rows), s_f)
    lax.fori_loop(0, n_lat, lambda i, s: scan_tile(
        n_lat - 1 - i, bwd_order, s, ab_ref, decb_ref, b_rows), s_b)

    def post_body(t, carry):
        for j in range(per):
            rows = pl.ds(pl.multiple_of(t * tm + j * c, c), c)
            q_both = jnp.concatenate([qdf_ref[0, rows, :], qdb_ref[0, rows, :]], axis=1)
            o = oi_ref[rows, :] + _dot(q_both, st_ref[t * per + j])
            y = _rms(o) * gn_ref[...]
            o_ref[0, rows, :] = (y * g_ref[0, rows, :].astype(F32)).astype(BF16)
        return carry

    if with_ctx:
        lax.fori_loop(0, n_lat + 1, post_body, 0, unroll=3)
    else:
        lax.fori_loop(0, n_lat, post_body, 0, unroll=4)


def _gla_sums():
    tm, c = TOKEN_TILE, GLA_CHUNK
    r = np.arange(tm)[:, None]
    col = np.arange(tm)[None, :]
    same = (r // c) == (col // c)
    return jnp.asarray(np.stack([same & (col <= r), same & (col >= r)]).astype(np.float32), BF16)


def _gla_selector():
    per = TOKEN_TILE // GLA_CHUNK
    sel = np.zeros((16, per * LANES), np.float32)
    for j in range(per):
        sel[j, j * LANES:(j + 1) * LANES] = 1.0
        sel[8 + j, j * LANES:(j + 1) * LANES] = 1.0
    return jnp.asarray(sel, BF16)


def _gla(qd, ki, tot, va, ga, gn, t_lat):
    b, s, _ = va.shape
    s_q = qd.shape[1]
    tm = TOKEN_TILE
    n_lat = t_lat // tm
    n_tiles = s // tm
    n_chunks = s // GLA_CHUNK
    h = GLA_HEADS
    sel = _gla_selector()
    fwd = lambda bi, hi: (bi, 0, hi)
    bwd = lambda bi, hi: (bi, 0, h + hi)
    return pl.pallas_call(
        functools.partial(_gla_kernel, n_lat=n_lat, with_ctx=s_q == s),
        grid=(b, h),
        in_specs=[
            pl.BlockSpec((1, s_q, GLA_DK), fwd), pl.BlockSpec((1, s_q, GLA_DK), bwd),
            pl.BlockSpec((1, s, GLA_DK), fwd), pl.BlockSpec((1, s, GLA_DK), bwd),
            pl.BlockSpec((1, s, GLA_DV), fwd), pl.BlockSpec((1, s_q, GLA_DV), fwd),
            pl.BlockSpec((1, n_tiles, 8, GLA_DK), lambda bi, hi: (bi, 0, 0, hi)),
            pl.BlockSpec((1, n_tiles, 8, GLA_DK), lambda bi, hi: (bi, 0, 0, h + hi)),
            pl.BlockSpec((1, GLA_DV), lambda bi, hi: (0, 0)),
            _const_spec(sel.shape),
        ],
        out_specs=pl.BlockSpec((1, s_q, GLA_DV), fwd),
        out_shape=jax.ShapeDtypeStruct((b, s_q, h * GLA_DV), BF16),
        scratch_shapes=[
            pltpu.VMEM((s, GLA_DV), F32),
            pltpu.VMEM((n_chunks, GLA_DK, GLA_DV), F32), pltpu.VMEM((n_chunks, GLA_DK, GLA_DV), F32),
            pltpu.VMEM((n_chunks, 2 * GLA_DK, GLA_DV), BF16),
            pltpu.VMEM((n_tiles, GLA_DK, tm // GLA_CHUNK * LANES), F32),
            pltpu.VMEM((n_tiles, GLA_DK, tm // GLA_CHUNK * LANES), F32),
            pltpu.VMEM((tm, tm), F32), pltpu.VMEM((tm, tm), F32),
        ],
        compiler_params=_cparams(2),
        name="gla",
    )(qd, qd, ki, ki, va, ga, tot, tot, gn, sel)


def _swa_kernel(sink_ref, q_ref, kx_ref, vx_ref, bias_ref, o_ref, s0_ref, s1_ref, m0_ref, m1_ref,
                *, n_lat, t_lat, with_ctx):
    tq = TOKEN_TILE
    span = tq + 2 * WINDOW
    hd = SWA_HEAD_DIM
    g = pl.program_id(1)
    ka, kb = slice(0, LANES), slice(LANES, 2 * LANES)

    def q_rows(i):
        return pl.ds(pl.multiple_of(i * tq, tq), tq)

    def sink_of(head):
        return sink_ref[g * SWA_GROUP + head] * LOG2_E

    def keys(i):
        ws = pl.multiple_of(jnp.clip(i * tq - WINDOW, 0, t_lat - span), WINDOW)
        win = pl.ds(ws, span)
        kc = [jnp.concatenate([kx_ref[0, win, c], kx_ref[0, t_lat:, c]], axis=0) for c in (ka, kb)]
        vc = jnp.concatenate([vx_ref[0, :, win], vx_ref[0, :, t_lat:]], axis=1)
        return kc, vc, (i * tq - ws) // WINDOW

    def scores(i, kc, bias_idx, head, s_ref, m_ref):
        qp = q_ref[0, q_rows(i), (head // 2) * LANES:(head // 2 + 1) * LANES]
        s_t = _dot_nt(kc[head % 2], qp)
        top = s_t[0:span] + bias_ref[bias_idx]
        bot = s_t[span:]
        s_ref[0:span, :] = top
        s_ref[span:, :] = bot
        m = jnp.maximum(jnp.max(top, axis=0, keepdims=True), jnp.max(bot, axis=0, keepdims=True))
        m_ref[...] = jnp.broadcast_to(jnp.maximum(m, sink_of(head)), m_ref.shape)

    def normalise(acc, m, head):
        return acc[0:hd] / (acc[hd:hd + 1] + jnp.exp2(sink_of(head) - m))

    def values(vc, head, s_ref, m_ref):
        m = m_ref[0:1, :]
        p_t = jnp.exp2(s_ref[...] - m).astype(BF16)
        return normalise(_dot(vc, p_t), m, head)

    def store(i, pair, o_a, o_b):
        o_t = jnp.concatenate([o_a, o_b], axis=0)
        o_ref[0, q_rows(i), pair * LANES:(pair + 1) * LANES] = o_t.T.astype(BF16)

    def scores_all(i, s_ref, m_ref):
        kc, _, bias_idx = keys(i)
        for head in range(SWA_GROUP):
            scores(i, kc, bias_idx, head, s_ref.at[head], m_ref.at[head])

    def values_all(i, s_ref, m_ref):
        _, vc, _ = keys(i)
        for pair in range(SWA_GROUP // 2):
            a, b = 2 * pair, 2 * pair + 1
            store(i, pair, values(vc, a, s_ref.at[a], m_ref.at[a]),
                  values(vc, b, s_ref.at[b], m_ref.at[b]))

    scores_all(0, s0_ref, m0_ref)

    def body(j, carry):
        scores_all(2 * j + 1, s1_ref, m1_ref)
        values_all(2 * j, s0_ref, m0_ref)
        scores_all(2 * j + 2, s0_ref, m0_ref)
        values_all(2 * j + 1, s1_ref, m1_ref)
        return carry

    lax.fori_loop(0, n_lat // 2 - 1, body, 0)
    scores_all(n_lat - 1, s1_ref, m1_ref)
    values_all(n_lat - 2, s0_ref, m0_ref)
    values_all(n_lat - 1, s1_ref, m1_ref)

    if with_ctx:
        rows = slice(t_lat, t_lat + tq)
        for pair in range(SWA_GROUP // 2):
            qp = q_ref[0, rows, pair * LANES:(pair + 1) * LANES]
            outs = []
            for head, kcols in ((2 * pair, ka), (2 * pair + 1, kb)):
                s_t = _dot_nt(kx_ref[0, t_lat:, kcols], qp)
                m = jnp.maximum(jnp.max(s_t, axis=0, keepdims=True), sink_of(head))
                p_t = jnp.exp2(s_t - m).astype(BF16)
                outs.append(normalise(_dot(vx_ref[0, :, t_lat:], p_t), m, head))
            o_t = jnp.concatenate(outs, axis=0)
            o_ref[0, rows, pair * LANES:(pair + 1) * LANES] = o_t.T.astype(BF16)


def _swa_bias():
    tq, span = TOKEN_TILE, TOKEN_TILE + 2 * WINDOW
    r = np.arange(span)[:, None]
    c = np.arange(tq)[None, :]
    tabs = [np.where(np.abs(r - off - c) <= WINDOW, 0.0, -1e30) for off in (0, WINDOW, 2 * WINDOW)]
    return jnp.asarray(np.stack(tabs).astype(np.float32))


def _swa(sinks, qs, kx, vx, t_lat, n_tiles):
    b, s, _ = kx.shape
    tq = TOKEN_TILE
    gw = SWA_GROUP * SWA_HEAD_DIM
    n_keys = tq + 2 * WINDOW + (s - t_lat)
    bias = _swa_bias()
    n_lat = t_lat // tq
    return pl.pallas_call(
        functools.partial(_swa_kernel, n_lat=n_lat, t_lat=t_lat, with_ctx=n_tiles > n_lat),
        grid=(b, SWA_KV_HEADS),
        in_specs=[
            pl.BlockSpec(memory_space=pltpu.SMEM),
            pl.BlockSpec((1, qs.shape[1], gw), lambda bi, gi: (bi, 0, gi)),
            pl.BlockSpec((1, s, 2 * LANES), lambda bi, gi: (bi, 0, gi)),
            pl.BlockSpec((1, SWA_VT_ROWS, s), lambda bi, gi: (bi, gi, 0)),
            _const_spec(bias.shape),
        ],
        out_specs=pl.BlockSpec((1, n_tiles * tq, gw), lambda bi, gi: (bi, 0, gi)),
        out_shape=jax.ShapeDtypeStruct((b, n_tiles * tq, SWA_HEADS * SWA_HEAD_DIM), BF16),
        scratch_shapes=[pltpu.VMEM((SWA_GROUP, n_keys, tq), F32), pltpu.VMEM((SWA_GROUP, n_keys, tq), F32),
                        pltpu.VMEM((SWA_GROUP, 8, tq), F32), pltpu.VMEM((SWA_GROUP, 8, tq), F32)],
        compiler_params=_cparams(2),
        name="swa",
    )(sinks, qs, kx, vx, bias)


def _mla_attn_kernel(q_ref, kn_ref, kr_ref, vt_ref, o_ref, k_ref, s0_ref, s1_ref, m0_ref, m1_ref,
                     *, n_lat, t_lat, with_ctx):
    tq = TOKEN_TILE
    k_ref[:, 0:MLA_NOPE] = kn_ref[0]
    k_ref[:, MLA_NOPE:] = kr_ref[0]

    def q_rows(i):
        return pl.ds(pl.multiple_of(i * tq, tq), tq)

    def finish(acc, rows):
        o_t = acc[0:MLA_V] / acc[MLA_V:MLA_V + 1]
        o_ref[0, rows, :] = o_t.T.astype(BF16)

    def scores(i, s_ref, m_ref):
        q = q_ref[0, q_rows(i), :]
        half = k_ref.shape[0] // 2
        lo = _dot_nt(k_ref[0:half, :], q)
        hi = _dot_nt(k_ref[half:, :], q)
        s_ref[0:half, :] = lo
        s_ref[half:, :] = hi
        m = jnp.maximum(jnp.max(lo, axis=0, keepdims=True), jnp.max(hi, axis=0, keepdims=True))
        m_ref[...] = jnp.broadcast_to(m, m_ref.shape)

    def values(i, s_ref, m_ref):
        p_t = jnp.exp2(s_ref[...] - m_ref[0:1, :]).astype(BF16)
        finish(_dot(vt_ref[0, 0], p_t), q_rows(i))

    scores(0, s0_ref, m0_ref)

    def body(j, carry):
        scores(2 * j + 1, s1_ref, m1_ref)
        values(2 * j, s0_ref, m0_ref)
        scores(2 * j + 2, s0_ref, m0_ref)
        values(2 * j + 1, s1_ref, m1_ref)
        return carry

    lax.fori_loop(0, n_lat // 2 - 1, body, 0)
    scores(n_lat - 1, s1_ref, m1_ref)
    values(n_lat - 2, s0_ref, m0_ref)
    values(n_lat - 1, s1_ref, m1_ref)

    if with_ctx:
        s_t = _dot_nt(k_ref[t_lat:, :], q_ref[0, t_lat:, :])
        p_t = jnp.exp2(s_t - jnp.max(s_t, axis=0, keepdims=True)).astype(BF16)
        finish(_dot(vt_ref[0, 0, :, t_lat:], p_t), slice(t_lat, t_lat + tq))


def _mla_attn(q, kn, kr, vt, t_lat, n_tiles):
    b, s, _ = kn.shape
    tq = TOKEN_TILE
    n_lat = t_lat // tq
    return pl.pallas_call(
        functools.partial(_mla_attn_kernel, n_lat=n_lat, t_lat=t_lat, with_ctx=n_tiles > n_lat),
        grid=(b, MLA_HEADS),
        in_specs=[
            pl.BlockSpec((1, q.shape[1], MLA_QK_PAD), lambda bi, hi: (bi, 0, hi)),
            pl.BlockSpec((1, s, MLA_NOPE), lambda bi, hi: (bi, 0, hi)),
            pl.BlockSpec((1, s, LANES), lambda bi, hi: (bi, 0, 0)),
            pl.BlockSpec((1, 1, MLA_VT_ROWS, s), lambda bi, hi: (bi, hi, 0, 0)),
        ],
        out_specs=pl.BlockSpec((1, n_tiles * tq, MLA_V), lambda bi, hi: (bi, 0, hi)),
        out_shape=jax.ShapeDtypeStruct((b, n_tiles * tq, MLA_HEADS * MLA_V), BF16),
        scratch_shapes=[pltpu.VMEM((s, MLA_QK_PAD), BF16),
                        pltpu.VMEM((s, tq), F32), pltpu.VMEM((s, tq), F32),
                        pltpu.VMEM((8, tq), F32), pltpu.VMEM((8, tq), F32)],
        compiler_params=_cparams(2),
        name="mla_attn",
    )(q, kn, kr, vt)


def _residual_inputs(xs, n_lat):
    tm = TOKEN_TILE
    if isinstance(xs, tuple):
        lat, ctx, ctx_block = xs[0], xs[1], 0
    else:
        lat, ctx, ctx_block = xs, xs, n_lat
    d = lat.shape[-1]
    specs = [pl.BlockSpec((1, tm, d), lambda bi, i: (bi, jnp.minimum(i, n_lat - 1), 0)),
             pl.BlockSpec((1, tm, d), lambda bi, i: (bi, ctx_block, 0))]
    return (lat, ctx), specs


def _pick_tile(xl_ref, xc_ref, n_lat):
    return jnp.where(pl.program_id(1) < n_lat, xl_ref[0], xc_ref[0])


def _mix_ffn_kernel(xl_ref, xc_ref, ya_ref, yb_ref, yc_ref, mg_ref, mod_ref,
                    wa_ref, wb_ref, wc_ref, wo_ref, gain_ref, wi_ref, wd_ref, fin_ref, o_ref,
                    *, n_lat, final):
    d = D_MODEL
    u = mg_ref[0, :, 0:d].astype(F32) * _dot(ya_ref[0], wa_ref[...])
    u += mg_ref[0, :, d:2 * d].astype(F32) * _dot(yb_ref[0], wb_ref[...])
    u += mg_ref[0, :, 2 * d:3 * d].astype(F32) * _dot(yc_ref[0], wc_ref[...])
    x = _pick_tile(xl_ref, xc_ref, n_lat) + mod_ref[0, 0, 2:3, :] * _dot(u.astype(BF16), wo_ref[...])

    y = _rms(x) * gain_ref[...]
    h = (y * (1.0 + mod_ref[0, 0, 4:5, :]) + mod_ref[0, 0, 3:4, :]).astype(BF16)
    ffn = jnp.zeros_like(x)
    for c0 in range(0, D_FF, FFN_SLAB):
        gate = _dot(h, wi_ref[:, c0:c0 + FFN_SLAB])
        up = _dot(h, wi_ref[:, D_FF + c0:D_FF + c0 + FFN_SLAB])
        a = (gate * jax.nn.sigmoid(gate) * up).astype(BF16)
        ffn += _dot(a, wd_ref[c0:c0 + FFN_SLAB, :])
    x = x + mod_ref[0, 0, 5:6, :] * ffn
    if final:
        x = _rms(x) * fin_ref[...]
    o_ref[0] = x


def _mix_ffn(xs, ya, yb, yc, mg, modtab, wa, wb, wc, wo, gain, wi, wd, fin, n_lat, n_tiles, final):
    (xl, xc), x_specs = _residual_inputs(xs, n_lat)
    b, _, d = xl.shape
    tm = TOKEN_TILE
    row = lambda bi, i: (bi, i, 0)
    weights, w_specs = zip(*[_weight_spec(w) for w in (wa, wb, wc, wo, gain, wi, wd, fin)])
    return pl.pallas_call(
        functools.partial(_mix_ffn_kernel, n_lat=n_lat, final=final),
        grid=(b, n_tiles),
        in_specs=x_specs + [
            pl.BlockSpec((1, tm, d), row), pl.BlockSpec((1, tm, d), row),
            pl.BlockSpec((1, tm, d), row), pl.BlockSpec((1, tm, 3 * d), row),
            pl.BlockSpec((1, 1, 8, d), lambda bi, i: (bi, i // n_lat, 0, 0)),
        ] + list(w_specs),
        out_specs=pl.BlockSpec((1, tm, d), row),
        out_shape=jax.ShapeDtypeStruct((b, n_tiles * tm, d), F32),
        compiler_params=_cparams(2),
        name="mix_ffn",
    )(xl, xc, ya, yb, yc, mg, modtab, *weights)


def _rope_tables(t_lat, s_tot):
    half = SWA_HEAD_DIM // 4
    inv = np.power(ROPE_BASE, -np.arange(half, dtype=np.float32) / half).astype(np.float32)
    pos = np.arange(t_lat)
    lane = np.arange(SWA_HEAD_DIM)
    p = np.where(lane[None, :] < 2 * half, (pos // GRID_W)[:, None], (pos % GRID_W)[:, None])
    ang = p.astype(np.float32) * inv[lane % half][None, :]
    cos, sin = np.cos(ang), np.sin(ang)
    upper = (lane % (2 * half)) >= half
    tabs = np.stack([cos, np.where(upper, sin, 0.0), np.where(upper, 0.0, -sin)]).astype(np.float32)
    ident = np.zeros((3, s_tot - t_lat, SWA_HEAD_DIM), np.float32)
    ident[0] = 1.0
    return np.concatenate([tabs, ident], axis=1)


def kernel(x, c, ctx, c_ctx, w_mod, b_mod, norm_mix, w_in, w_gk_fwd, b_gk_fwd, w_gk_bwd, b_gk_bwd,
           gla_norm, sinks, q_norm, w_q_up, kv_norm, w_kv_up, w_pa, w_pb, w_pc, w_o,
           norm_ffn, w_ffn_in, w_ffn_out, final_norm):
    b, t_lat, d = x.shape
    l_ctx = ctx.shape[1]
    s_tot = t_lat + l_ctx
    depth = w_mod.shape[0]
    tm = TOKEN_TILE
    assert t_lat % tm == 0 and l_ctx == tm and t_lat % GRID_W == 0 and d == D_MODEL
    n_lat = t_lat // tm
    n_all = s_tot // tm

    cc = jnp.concatenate([c, c_ctx[None, :], jnp.zeros((16 - b - 1, d), F32)], axis=0)
    mod = _modulation(cc, w_mod, b_mod)
    mod_lat = mod[:, :b].reshape(depth, b, 1, 6, d)
    mod_ctx = jnp.broadcast_to(mod[:, b].reshape(depth, 1, 1, 6, d), (depth, b, 1, 6, d))
    modtab = jnp.pad(jnp.concatenate([mod_lat, mod_ctx], axis=2),
                     ((0, 0), (0, 0), (0, 0), (0, 2), (0, 0)))

    t64 = _rope_tables(t_lat, s_tot)
    ident = np.zeros_like(t64)
    ident[0] = 1.0
    rk = np.concatenate([t64, t64], axis=2)
    rq = rk * np.float32(SWA_HEAD_DIM ** -0.5 * LOG2_E)
    rm = np.concatenate([t64, ident], axis=2)
    rmq = rm * np.float32((MLA_NOPE + MLA_ROPE) ** -0.5 * LOG2_E)
    rq, rk, rm, rmq = (jnp.asarray(a) for a in (rq, rk, rm, rmq))

    w_in_t = jnp.swapaxes(w_in, 1, 2).astype(BF16)

    gla_sums = _gla_sums()

    xs = (x, ctx)
    for l in range(depth):
        last = l == depth - 1
        n_out = n_lat if last else n_all
        hk = GLA_HEADS * GLA_DK
        wg = jnp.zeros((2 * GLA_GATE_RANK, 2 * hk), F32)
        wg = wg.at[0:GLA_GATE_RANK, 0:hk].set(w_gk_fwd[l])
        wg = wg.at[GLA_GATE_RANK:, hk:].set(w_gk_bwd[l])
        bg = jnp.concatenate([b_gk_fwd[l], b_gk_bwd[l]]).reshape(1, 2 * hk)
        gla_weights = (wg.astype(BF16), bg, gla_sums)
        wq = jnp.pad(w_q_up[l].reshape(MLA_Q_RANK, MLA_HEADS, MLA_NOPE + MLA_ROPE),
                     ((0, 0), (0, 0), (0, MLA_QK_PAD - MLA_NOPE - MLA_ROPE))
                     ).reshape(MLA_Q_RANK, MLA_HEADS * MLA_QK_PAD).astype(BF16)

        wkv = w_kv_up[l].reshape(MLA_KV_RANK, MLA_HEADS, MLA_NOPE + MLA_V)
        wk = wkv[:, :, :MLA_NOPE].reshape(MLA_KV_RANK, MLA_HEADS * MLA_NOPE).astype(BF16)
        wvt = wkv[:, :, MLA_NOPE:].reshape(MLA_KV_RANK, MLA_HEADS * MLA_V).T.astype(BF16)
        mla_weights = (wq, wk, wvt, q_norm[l].reshape(1, -1), kv_norm[l].reshape(1, -1))

        later_weights = [(w, l) for w in (w_pa, w_pb, w_pc, w_o, w_ffn_in, w_ffn_out)]
        (qd, ki, tot, va, ga, qs, kx, vx, mg, qm, kn, kr, vm,
         wa_b, wb_b, wc_b, wo_b, wi_b, wd_b) = _inproj(
            xs, modtab[l], norm_mix[l].reshape(1, d), (w_in_t, l), rq, rk, rm,
            mla_weights, rmq, gla_weights, later_weights, n_lat, ctx_queries=not last)
        ya = _gla(qd, ki, tot, va, ga, gla_norm[l].reshape(1, GLA_DV), t_lat)
        yb = _swa(sinks[l], qs, kx, vx, t_lat, n_out)
        yc = _mla_attn(qm, kn, kr, vm, t_lat, n_out)
        xs = _mix_ffn(xs, ya, yb, yc, mg, modtab[l], wa_b, wb_b, wc_b, wo_b,
                      norm_ffn[l].reshape(1, d), wi_b, wd_b,
                      final_norm.reshape(1, d), n_lat, n_out, last)
    return xs
```
